```python
import jax, jax.numpy as jnp
from jax import lax
import numpy as np

D_MODEL = 2048
BATCH = 8
SEQ = 4096
DEPTH = 2
DEC_BATCH = 1
DEC_SEQ = 16384
PAST_LEN = 128

GRID_W = 64
HEAD_DIM = 64
NA_HEADS = 8
NA_WIN_H = 8
NA_WIN_W = 16
MLA_HEADS = 8
MLA_NOPE = 64
MLA_ROPE = 32
MLA_V = 64
MLA_Q_RANK = 512
MLA_KV_RANK = 256
GQA_HEADS = 8
GQA_KV_HEADS = 2
N_BRANCH = 3
BRANCH_WIDTH = 512
N_EXPERTS = 16
EC_CAPACITY = 2
D_EXPERT = 1024
Q_BLOCK = 128
ROPE_THETA = 10000.0
EPS = 1e-6

IN_SEGMENTS = (
    3 * NA_HEADS * HEAD_DIM,
    MLA_Q_RANK,
    MLA_KV_RANK,
    MLA_ROPE,
    GQA_HEADS * HEAD_DIM,
    GQA_KV_HEADS * HEAD_DIM,
    GQA_KV_HEADS * HEAD_DIM,
    N_BRANCH * D_MODEL,
)
IN_COLS = sum(IN_SEGMENTS)

kernel_name = 'hybrid_na_mla_gqa_ec_encoder'


def rms_norm(x, g):
    xf = x.astype(jnp.float32)
    y = xf * lax.rsqrt(jnp.mean(xf * xf, axis=-1, keepdims=True) + EPS)
    return (y * g.astype(jnp.float32)).astype(x.dtype)


def rope_cos_sin(pos, dim):
    inv_freq = ROPE_THETA ** (-jnp.arange(0, dim, 2, dtype=jnp.float32) / dim)
    ang = pos.astype(jnp.float32)[:, None] * inv_freq[None, :]
    ang = jnp.concatenate([ang, ang], axis=-1)
    return jnp.cos(ang), jnp.sin(ang)


def apply_rope(x, cos, sin):
    xf = x.astype(jnp.float32)
    x1, x2 = jnp.split(xf, 2, axis=-1)
    rot = jnp.concatenate([-x2, x1], axis=-1)
    return (xf * cos[:, None, :] + rot * sin[:, None, :]).astype(x.dtype)


def axial_rope(x, row, col):
    half = x.shape[-1] // 2
    cr, sr = rope_cos_sin(row, half)
    cc, sc = rope_cos_sin(col, half)
    return jnp.concatenate([apply_rope(x[..., :half], cr, sr), apply_rope(x[..., half:], cc, sc)], axis=-1)


def blocked_attention(q, k, v):
    B, T, Hk, G, dk = q.shape
    dv = v.shape[-1]
    nb = T // Q_BLOCK
    qb = jnp.moveaxis(q.reshape(B, nb, Q_BLOCK, Hk, G, dk), 1, 0)
    scale = dk ** -0.5

    def step(q_blk):
        s = jnp.einsum('bqhgd,bshd->bhgqs', q_blk, k).astype(jnp.float32) * scale
        p = jax.nn.softmax(s, axis=-1).astype(v.dtype)
        return jnp.einsum('bhgqs,bshd->bqhgd', p, v)

    o = lax.map(step, qb)
    return jnp.moveaxis(o, 0, 1).reshape(B, T, Hk * G * dv)


def neighborhood_attention(q, k, v, rel_bias):
    B, T, H, Dh = q.shape
    rows = T // GRID_W
    kh = min(NA_WIN_H, rows)
    qg = q.reshape(B, rows, GRID_W, H, Dh)
    kg = k.reshape(B, rows, GRID_W, H, Dh)
    vg = v.reshape(B, rows, GRID_W, H, Dh)
    row_start = jnp.clip(jnp.arange(rows) - kh // 2, 0, rows - kh)
    col_start = np.clip(np.arange(GRID_W) - NA_WIN_W // 2, 0, GRID_W - NA_WIN_W)
    col_idx = col_start[:, None] + np.arange(NA_WIN_W)[None, :]
    dc_idx = col_idx - np.arange(GRID_W)[:, None] + (NA_WIN_W - 1)
    scale = Dh ** -0.5

    def row_block(r):
        rs = row_start[r]
        q_r = lax.dynamic_index_in_dim(qg, r, axis=1, keepdims=False)
        k_n = lax.dynamic_slice_in_dim(kg, rs, kh, axis=1)[:, :, col_idx]
        v_n = lax.dynamic_slice_in_dim(vg, rs, kh, axis=1)[:, :, col_idx]
        dr_idx = rs + jnp.arange(kh) - r + (NA_WIN_H - 1)
        bias = jnp.transpose(rel_bias[:, dr_idx[:, None, None], dc_idx[None]], (0, 2, 1, 3))
        s = jnp.einsum('bqhd,brqwhd->bhqrw', q_r, k_n).astype(jnp.float32) * scale + bias.astype(jnp.float32)
        p = jax.nn.softmax(s.reshape(B, H, GRID_W, kh * NA_WIN_W), axis=-1).reshape(s.shape).astype(v.dtype)
        return jnp.einsum('bhqrw,brqwhd->bqhd', p, v_n)

    o = lax.map(row_block, jnp.arange(rows))
    return jnp.moveaxis(o, 0, 1).reshape(B, T, H * Dh)


def token_mix(h, w_in, na_rel_bias, mla_g_q, mla_w_uq, mla_g_kv, mla_w_ukv, gqa_g_q, gqa_g_k, w_branch, w_out):
    B, T, _ = h.shape
    proj = h @ w_in
    offs = [int(o) for o in np.cumsum(IN_SEGMENTS)[:-1]]
    na_qkv, mla_cq, mla_ckv, mla_kr, gqa_q, gqa_k, gqa_v, gate_logits = jnp.split(proj, offs, axis=-1)
    pos = jnp.arange(T)
    row = pos // GRID_W
    col = pos % GRID_W

    na = na_qkv.reshape(B, T, 3, NA_HEADS, HEAD_DIM)
    o_a = neighborhood_attention(na[:, :, 0], na[:, :, 1], na[:, :, 2], na_rel_bias)

    cos1, sin1 = rope_cos_sin(pos, MLA_ROPE)
    qm = (rms_norm(mla_cq, mla_g_q) @ mla_w_uq).reshape(B, T, MLA_HEADS, MLA_NOPE + MLA_ROPE)
    qm = jnp.concatenate([qm[..., :MLA_NOPE], apply_rope(qm[..., MLA_NOPE:], cos1, sin1)], axis=-1)
    kv = (rms_norm(mla_ckv, mla_g_kv) @ mla_w_ukv).reshape(B, T, MLA_HEADS, MLA_NOPE + MLA_V)
    k_rope = apply_rope(mla_kr[:, :, None, :], cos1, sin1)
    km = jnp.concatenate([kv[..., :MLA_NOPE], jnp.broadcast_to(k_rope, (B, T, MLA_HEADS, MLA_ROPE))], axis=-1)
    o_b = blocked_attention(qm[:, :, :, None, :], km, kv[..., MLA_NOPE:])

    qc = axial_rope(rms_norm(gqa_q.reshape(B, T, GQA_HEADS, HEAD_DIM), gqa_g_q), row, col)
    kc = axial_rope(rms_norm(gqa_k.reshape(B, T, GQA_KV_HEADS, HEAD_DIM), gqa_g_k), row, col)
    vc = gqa_v.reshape(B, T, GQA_KV_HEADS, HEAD_DIM)
    o_c = blocked_attention(qc.reshape(B, T, GQA_KV_HEADS, GQA_HEADS // GQA_KV_HEADS, HEAD_DIM), kc, vc)

    merged = None
    for i, o in enumerate((o_a, o_b, o_c)):
        term = jax.nn.sigmoid(gate_logits[..., i * D_MODEL:(i + 1) * D_MODEL]) * (o @ w_branch[i])
        merged = term if merged is None else merged + term
    return merged @ w_out


def expert_choice_ffn(h, w_router, w_gate, w_up, w_down):
    B, T, D = h.shape
    n = B * T
    cap = EC_CAPACITY * n // N_EXPERTS
    hf = h.reshape(n, D)
    aff = jax.nn.softmax((hf @ w_router).astype(jnp.float32), axis=-1)
    g, idx = lax.top_k(aff.T, cap)
    xe = hf[idx]
    a = jnp.einsum('ecd,edf->ecf', xe, w_gate)
    u = jnp.einsum('ecd,edf->ecf', xe, w_up)
    ye = jnp.einsum('ecf,efd->ecd', jax.nn.silu(a) * u, w_down)
    ye = ye * g[..., None].astype(ye.dtype)
    y = jnp.zeros_like(hf).at[idx.reshape(-1)].add(ye.reshape(-1, D))
    return y.reshape(B, T, D)


def trunk(x, c, w_ada, b_ada, g_mix, g_ffn, w_in, na_rel_bias, mla_g_q, mla_w_uq, mla_g_kv, mla_w_ukv,
          gqa_g_q, gqa_g_k, w_branch, w_out, w_router, w_exp_gate, w_exp_up, w_exp_down, g_final):
    for l in range(DEPTH):
        mod = (jax.nn.silu(c) @ w_ada[l] + b_ada[l])[:, None, :]
        sh1, sc1, gt1, sh2, sc2, gt2 = jnp.split(mod, 6, axis=-1)
        h = rms_norm(x, g_mix[l]) * (1 + sc1) + sh1
        x = x + gt1 * token_mix(h, w_in[l], na_rel_bias[l], mla_g_q[l], mla_w_uq[l], mla_g_kv[l], mla_w_ukv[l],
                                gqa_g_q[l], gqa_g_k[l], w_branch[l], w_out[l])
        h = rms_norm(x, g_ffn[l]) * (1 + sc2) + sh2
        x = x + gt2 * expert_choice_ffn(h, w_router[l], w_exp_gate[l], w_exp_up[l], w_exp_down[l])
    return rms_norm(x, g_final)


def setup_inputs(seed: int = 0) -> dict:
    key = jax.random.key(seed)
    ks = jax.random.split(key, 24)

    def nrm(k, shape, scale):
        return jax.random.normal(k, shape, dtype=jnp.float32) * scale

    def gain(k, shape):
        return 1.0 + 0.05 * jax.random.normal(k, shape, dtype=jnp.float32)

    L, D = DEPTH, D_MODEL
    return {
        'x_prompt': nrm(ks[0], (BATCH, SEQ, D), 1.0),
        'x_sample': nrm(ks[1], (DEC_BATCH, DEC_SEQ, D), 1.0),
        'c_prompt': nrm(ks[2], (BATCH, D), 1.0),
        'c_sample': nrm(ks[3], (DEC_BATCH, D), 1.0),
        'w_ada': nrm(ks[4], (L, D, 6 * D), 0.5 * D ** -0.5),
        'b_ada': nrm(ks[5], (L, 6 * D), 0.01),
        'g_mix': gain(ks[6], (L, D)),
        'g_ffn': gain(ks[7], (L, D)),
        'w_in': nrm(ks[8], (L, D, IN_COLS), D ** -0.5),
        'na_rel_bias': nrm(ks[9], (L, NA_HEADS, 2 * NA_WIN_H - 1, 2 * NA_WIN_W - 1), 0.1),
        'mla_g_q': gain(ks[10], (L, MLA_Q_RANK)),
        'mla_w_uq': nrm(ks[11], (L, MLA_Q_RANK, MLA_HEADS * (MLA_NOPE + MLA_ROPE)), MLA_Q_RANK ** -0.5),
        'mla_g_kv': gain(ks[12], (L, MLA_KV_RANK)),
        'mla_w_ukv': nrm(ks[13], (L, MLA_KV_RANK, MLA_HEADS * (MLA_NOPE + MLA_V)), MLA_KV_RANK ** -0.5),
        'gqa_g_q': gain(ks[14], (L, HEAD_DIM)),
        'gqa_g_k': gain(ks[15], (L, HEAD_DIM)),
        'w_branch': nrm(ks[16], (L, N_BRANCH, BRANCH_WIDTH, D), BRANCH_WIDTH ** -0.5),
        'w_out': nrm(ks[17], (L, D, D), D ** -0.5),
        'w_router': nrm(ks[18], (L, D, N_EXPERTS), D ** -0.5),
        'w_exp_gate': nrm(ks[19], (L, N_EXPERTS, D, D_EXPERT), D ** -0.5),
        'w_exp_up': nrm(ks[20], (L, N_EXPERTS, D, D_EXPERT), D ** -0.5),
        'w_exp_down': nrm(ks[21], (L, N_EXPERTS, D_EXPERT, D), D_EXPERT ** -0.5),
        'g_final': gain(ks[22], (D,)),
    }


def reference(x_prompt, x_sample, c_prompt, c_sample, w_ada, b_ada, g_mix, g_ffn, w_in, na_rel_bias,
              mla_g_q, mla_w_uq, mla_g_kv, mla_w_ukv, gqa_g_q, gqa_g_k, w_branch, w_out, w_router,
              w_exp_gate, w_exp_up, w_exp_down, g_final):
    y_prompt = trunk(x_prompt, c_prompt, w_ada, b_ada, g_mix, g_ffn, w_in, na_rel_bias, mla_g_q, mla_w_uq,
                     mla_g_kv, mla_w_ukv, gqa_g_q, gqa_g_k, w_branch, w_out, w_router, w_exp_gate, w_exp_up,
                     w_exp_down, g_final)
    y_sample = trunk(x_sample, c_sample, w_ada, b_ada, g_mix, g_ffn, w_in, na_rel_bias, mla_g_q, mla_w_uq,
                     mla_g_kv, mla_w_ukv, gqa_g_q, gqa_g_k, w_branch, w_out, w_router, w_exp_gate, w_exp_up,
                     w_exp_down, g_final)
    return (y_prompt, y_sample)
```

```python
import functools
import math

import numpy as np
import jax
import jax.numpy as jnp
from jax import lax
from jax.experimental import pallas as pl
from jax.experimental.pallas import tpu as pltpu

F32 = jnp.float32
BF16 = jnp.bfloat16

GRID_W = 64
HEAD_DIM = 64
N_HEADS = 8
NA_WIN_H = 8
NA_WIN_W = 16
MLA_NOPE = 64
MLA_ROPE = 32
MLA_V = 64
GQA_KV_HEADS = 2
N_BRANCH = 3
N_EXPERTS = 16
EC_CAPACITY = 2
ROPE_THETA = 10000.0
EPS = 1e-6
LOG2E = 1.4426950408889634

LANE = 128
NA_ROWS_PER_BLOCK = 4
NA_BLOCK = NA_ROWS_PER_BLOCK * GRID_W
NA_KEY_BLOCKS = 3
NEG = -1e30
VMEM_LIMIT = 56 * 1024 * 1024

GQA_HEAD_ORDER = (0, 4, 1, 5, 2, 6, 3, 7)


def _params(sem):
    return pltpu.CompilerParams(dimension_semantics=sem, vmem_limit_bytes=VMEM_LIMIT)


def _tile(n, pref):
    t = min(n, pref)
    while n % t:
        t //= 2
    return t


def _ada_kernel(c_ref, w_ref, b_ref, o_ref):
    c = c_ref[...]
    a = (c * jax.nn.sigmoid(c)).astype(BF16)
    o_ref[...] = jnp.dot(a, w_ref[...].astype(BF16), preferred_element_type=F32) + b_ref[...]


def ada_modulation(c, w_ada, b_ada):
    L, D, D6 = w_ada.shape
    R = c.shape[0]
    tn = _tile(D6, 1024)
    return pl.pallas_call(
        _ada_kernel,
        grid=(L, D6 // tn),
        in_specs=[
            pl.BlockSpec((R, D), lambda l, j: (0, 0)),
            pl.BlockSpec((None, D, tn), lambda l, j: (l, 0, j)),
            pl.BlockSpec((None, 1, tn), lambda l, j: (l, 0, j)),
        ],
        out_specs=pl.BlockSpec((None, R, tn), lambda l, j: (l, 0, j)),
        out_shape=jax.ShapeDtypeStruct((L, R, D6), F32),
        compiler_params=_params(("parallel", "parallel")),
        name="ada_modulation",
    )(c, w_ada, b_ada.reshape(L, 1, D6))


def _norm_mod(x, g, sc, sh):
    ms = jnp.mean(x * x, axis=-1, keepdims=True)
    y = x * lax.rsqrt(ms + EPS)
    return (y * g) * (1.0 + sc) + sh


def _in_proj_kernel(x_ref, g_ref, sc_ref, sh_ref, w_ref, o_ref, h_sc, *, sigmoid_out):
    @pl.when(pl.program_id(1) == 0)
    def _():
        h_sc[...] = _norm_mod(x_ref[...], g_ref[...], sc_ref[...], sh_ref[...]).astype(BF16)

    acc = jnp.dot(h_sc[...], w_ref[...], preferred_element_type=F32)
    if sigmoid_out:
        acc = jax.nn.sigmoid(acc)
    o_ref[...] = acc.astype(o_ref.dtype)


def in_projection(x, g, sc, sh, w, seq, *, tn, sigmoid_out, name):
    N, D = x.shape
    C = w.shape[1]
    tm = _tile(seq, 1024)
    per_seq = seq // tm
    return pl.pallas_call(
        functools.partial(_in_proj_kernel, sigmoid_out=sigmoid_out),
        grid=(N // tm, C // tn),
        in_specs=[
            pl.BlockSpec((tm, D), lambda i, j: (i, 0)),
            pl.BlockSpec((1, D), lambda i, j: (0, 0)),
            pl.BlockSpec((None, 1, D), lambda i, j: (i // per_seq, 0, 0)),
            pl.BlockSpec((None, 1, D), lambda i, j: (i // per_seq, 0, 0)),
            pl.BlockSpec((D, tn), lambda i, j: (0, j)),
        ],
        out_specs=pl.BlockSpec((tm, tn), lambda i, j: (i, j)),
        out_shape=jax.ShapeDtypeStruct((N, C), BF16),
        scratch_shapes=[pltpu.VMEM((tm, D), BF16)],
        compiler_params=_params(("parallel", "arbitrary")),
        name=name,
    )(x, g.reshape(1, D), sc, sh, w)


def _na_bias_table(rel_bias):
    R = NA_ROWS_PER_BLOCK
    KR = NA_KEY_BLOCKS * R
    q_rows = np.stack([np.arange(R), R + np.arange(R), 2 * R + np.arange(R)])
    w_start = np.stack([np.zeros(R, np.int64), np.arange(R), np.full(R, KR - NA_WIN_H)])
    col = np.arange(GRID_W)
    col_start = np.clip(col - NA_WIN_W // 2, 0, GRID_W - NA_WIN_W)
    k_row = np.arange(KR)
    k_col = np.arange(GRID_W)
    row_ok = (k_row[None, None, :] >= w_start[:, :, None]) & (k_row[None, None, :] < w_start[:, :, None] + NA_WIN_H)
    col_ok = (k_col[None, :] >= col_start[:, None]) & (k_col[None, :] < col_start[:, None] + NA_WIN_W)
    dr = np.clip(k_row[None, None, :] - q_rows[:, :, None] + (NA_WIN_H - 1), 0, 2 * NA_WIN_H - 2)
    dc = np.clip(k_col[None, :] - col[:, None] + (NA_WIN_W - 1), 0, 2 * NA_WIN_W - 2)
    valid = row_ok[:, :, None, :, None] & col_ok[None, None, :, None, :]
    dr_b = np.broadcast_to(dr[:, :, None, :, None], valid.shape)
    dc_b = np.broadcast_to(dc[None, None, :, None, :], valid.shape)
    bias = rel_bias.astype(F32)[:, dr_b, dc_b]
    bias = jnp.where(valid[None], bias, NEG)
    H = rel_bias.shape[0]
    return jnp.transpose(bias, (1, 0, 2, 3, 4, 5)).reshape(3, H, R * GRID_W, KR * GRID_W)


def _na_kernel(q_ref, k0_ref, k1_ref, k2_ref, v0_ref, v1_ref, v2_ref, bias_ref, o_ref):
    lane = lax.broadcasted_iota(jnp.int32, (NA_BLOCK, LANE), 1)
    lo = lane < HEAD_DIM
    scale = HEAD_DIM ** -0.5
    for u in range(N_HEADS // 2):
        cs = slice(u * LANE, (u + 1) * LANE)
        q = q_ref[:, cs] * scale
        ks = (k0_ref[:, cs], k1_ref[:, cs], k2_ref[:, cs])
        vs = (v0_ref[:, cs], v1_ref[:, cs], v2_ref[:, cs])
        outs = []
        for hh in range(2):
            qh = jnp.where(lo if hh == 0 else ~lo, q, jnp.zeros_like(q))
            s = jnp.concatenate(
                [lax.dot_general(qh, k, (((1,), (1,)), ((), ())), preferred_element_type=F32) for k in ks], axis=1)
            s = s + bias_ref[2 * u + hh]
            m = jnp.max(s, axis=1, keepdims=True)
            p = jnp.exp(s - m)
            l = jnp.sum(p, axis=1, keepdims=True)
            pb = p.astype(BF16)
            o = None
            for j in range(NA_KEY_BLOCKS):
                t = jnp.dot(pb[:, j * NA_BLOCK:(j + 1) * NA_BLOCK], vs[j], preferred_element_type=F32)
                o = t if o is None else o + t
            outs.append(o / l)
        o_ref[:, cs] = jnp.where(lo, outs[0], outs[1]).astype(o_ref.dtype)


def neighborhood_attention(mix, bias_table, batch, seq):
    N = mix.shape[0]
    W = N_HEADS * HEAD_DIM
    nb = seq // NA_BLOCK
    assert seq % NA_BLOCK == 0 and nb >= NA_KEY_BLOCKS

    def kv_spec(colblk, j):
        return pl.BlockSpec(
            (NA_BLOCK, W),
            lambda b, i: (b * nb + jnp.clip(i - 1, 0, nb - NA_KEY_BLOCKS) + j, colblk))

    def bias_idx(b, i):
        return (jnp.where(i == 0, 0, jnp.where(i == nb - 1, 2, 1)), 0, 0, 0)

    return pl.pallas_call(
        _na_kernel,
        grid=(batch, nb),
        in_specs=[pl.BlockSpec((NA_BLOCK, W), lambda b, i: (b * nb + i, 0))]
        + [kv_spec(1, j) for j in range(NA_KEY_BLOCKS)]
        + [kv_spec(2, j) for j in range(NA_KEY_BLOCKS)]
        + [pl.BlockSpec((None, N_HEADS, NA_BLOCK, NA_KEY_BLOCKS * NA_BLOCK), bias_idx)],
        out_specs=pl.BlockSpec((NA_BLOCK, W), lambda b, i: (b * nb + i, 0)),
        out_shape=jax.ShapeDtypeStruct((N, W), BF16),
        compiler_params=_params(("parallel", "arbitrary")),
        name="neighborhood_attention",
    )(mix, mix, mix, mix, mix, mix, mix, bias_table)


def _rope_angles(pos, dim):
    inv_freq = ROPE_THETA ** (-jnp.arange(0, dim, 2, dtype=F32) / dim)
    ang = pos.astype(F32)[:, None] * inv_freq[None, :]
    return jnp.concatenate([ang, ang], axis=-1)


def _rot_tables(cos, sin, scale):
    n = cos.shape[1]
    first = (np.arange(n) % 32) < 16
    sa = jnp.where(first[None, :], -sin, 0.0)
    sb = jnp.where(first[None, :], 0.0, sin)
    return cos * scale, sa * scale, sb * scale


def _mla_tables(seq):
    pos = jnp.arange(seq)
    ang = _rope_angles(pos, MLA_ROPE)
    cos, sin = jnp.cos(ang), jnp.sin(ang)
    pad_hi = LANE - MLA_NOPE - MLA_ROPE

    def place(t, nope_val):
        return jnp.concatenate(
            [jnp.full((seq, MLA_NOPE), nope_val, F32), t, jnp.zeros((seq, pad_hi), F32)], axis=1)

    qs = (MLA_NOPE + MLA_ROPE) ** -0.5 * LOG2E
    qc, qa, qb = _rot_tables(cos, sin, qs)
    kc, ka, kb = _rot_tables(cos, sin, 1.0)
    q_tab = jnp.stack([place(qc, qs), place(qa, 0.0), place(qb, 0.0)])
    k_tab = jnp.stack([place(kc, 0.0), place(ka, 0.0), place(kb, 0.0)])
    return q_tab, k_tab


def _gqa_tables(seq):
    pos = jnp.arange(seq)
    half = HEAD_DIM // 2
    ar = _rope_angles(pos // GRID_W, half)
    ac = _rope_angles(pos % GRID_W, half)
    cos = jnp.concatenate([jnp.cos(ar), jnp.cos(ac)] * 2, axis=1)
    sin = jnp.concatenate([jnp.sin(ar), jnp.sin(ac)] * 2, axis=1)
    qs = HEAD_DIM ** -0.5 * LOG2E
    return jnp.stack(_rot_tables(cos, sin, qs)), jnp.stack(_rot_tables(cos, sin, 1.0))


def _apply_rot(x, tab_ref):
    n = x.shape[1]
    return (x * tab_ref[0] + pltpu.roll(x, n - 16, 1) * tab_ref[1] + pltpu.roll(x, 16, 1) * tab_ref[2])


def _rms(x, g):
    ms = jnp.mean(x * x, axis=-1, keepdims=True)
    return x * lax.rsqrt(ms + EPS) * g


def _mla_prep_kernel(cq_ref, ckv_ref, kr_ref, gq_ref, gkv_ref, wq_ref, wkn_ref, wv_ref, qtab_ref, ktab_ref,
                     q_out, k_out, v_out):
    cqn = _rms(cq_ref[...].astype(F32), gq_ref[...]).astype(BF16)
    ckvn = _rms(ckv_ref[...].astype(F32), gkv_ref[...]).astype(BF16)
    qf = jnp.dot(cqn, wq_ref[...], preferred_element_type=F32)
    kn = jnp.dot(ckvn, wkn_ref[...], preferred_element_type=F32)
    v_out[...] = jnp.dot(ckvn, wv_ref[...], preferred_element_type=F32).astype(v_out.dtype)
    k_rope = _apply_rot(kr_ref[...].astype(F32), ktab_ref)
    for h in range(N_HEADS):
        cs = slice(h * LANE, (h + 1) * LANE)
        q_out[:, cs] = _apply_rot(qf[:, cs], qtab_ref).astype(q_out.dtype)
        k_out[:, cs] = (kn[:, cs] + k_rope).astype(k_out.dtype)


def mla_prepare(mix, g_q, g_kv, wq, wkn, wv, q_tab, k_tab, seq):
    N = mix.shape[0]
    tm = _tile(seq, 512)
    per_seq = seq // tm
    qr, kvr = wq.shape[0], wkn.shape[0]
    cq_blk = 1536 // qr
    ckv_blk = 2560 // kvr
    kr_blk = 2816 // LANE
    const = lambda i: (0, 0)
    tab = pl.BlockSpec((3, tm, LANE), lambda i: (0, i % per_seq, 0))
    return pl.pallas_call(
        _mla_prep_kernel,
        grid=(N // tm,),
        in_specs=[
            pl.BlockSpec((tm, qr), lambda i: (i, cq_blk)),
            pl.BlockSpec((tm, kvr), lambda i: (i, ckv_blk)),
            pl.BlockSpec((tm, LANE), lambda i: (i, kr_blk)),
            pl.BlockSpec((1, qr), const),
            pl.BlockSpec((1, kvr), const),
            pl.BlockSpec(wq.shape, const),
            pl.BlockSpec(wkn.shape, const),
            pl.BlockSpec(wv.shape, const),
            tab, tab,
        ],
        out_specs=[
            pl.BlockSpec((tm, N_HEADS * LANE), lambda i: (i, 0)),
            pl.BlockSpec((tm, N_HEADS * LANE), lambda i: (i, 0)),
            pl.BlockSpec((tm, N_HEADS * MLA_V), lambda i: (i, 0)),
        ],
        out_shape=[
            jax.ShapeDtypeStruct((N, N_HEADS * LANE), BF16),
            jax.ShapeDtypeStruct((N, N_HEADS * LANE), BF16),
            jax.ShapeDtypeStruct((N, N_HEADS * MLA_V), BF16),
        ],
        compiler_params=_params(("parallel",)),
        name="mla_prepare",
    )(mix, mix, mix, g_q.reshape(1, qr), g_kv.reshape(1, kvr), wq, wkn, wv, q_tab, k_tab)


def _head_rms(x, g, seg_ref):
    sq = x * x
    hi = sq.astype(BF16)
    lo = (sq - hi.astype(F32)).astype(BF16)
    ms = (jnp.dot(hi, seg_ref[...], preferred_element_type=F32)
          + jnp.dot(lo, seg_ref[...], preferred_element_type=F32)) * (1.0 / HEAD_DIM)
    return x * lax.rsqrt(ms + EPS) * g


def _gqa_prep_kernel(q_ref, k_ref, gq_ref, gk_ref, segq_ref, segk_ref, qtab_ref, ktab_ref, q_out, k_out):
    tm = q_ref.shape[0]
    lane = lax.broadcasted_iota(jnp.int32, (tm, LANE), 1)
    lo = lane < HEAD_DIM
    qn = _head_rms(q_ref[...].astype(F32), gq_ref[...], segq_ref)
    for u in range(N_HEADS // 2):
        y = _apply_rot(qn[:, u * LANE:(u + 1) * LANE], qtab_ref)
        q_out[:, (2 * u) * LANE:(2 * u + 1) * LANE] = jnp.where(lo, y, 0.0).astype(q_out.dtype)
        q_out[:, (2 * u + 1) * LANE:(2 * u + 2) * LANE] = jnp.where(lo, 0.0, y).astype(q_out.dtype)
    kn = _head_rms(k_ref[...].astype(F32), gk_ref[...], segk_ref)
    k_out[...] = _apply_rot(kn, ktab_ref).astype(k_out.dtype)


def gqa_prepare(mix, g_q, g_k, q_tab, k_tab, seq):
    N = mix.shape[0]
    tm = _tile(seq, 512)
    per_seq = seq // tm
    W = N_HEADS * HEAD_DIM
    KW = GQA_KV_HEADS * HEAD_DIM
    const = lambda i: (0, 0)
    seg = lambda n: jnp.asarray(np.kron(np.eye(n // HEAD_DIM), np.ones((HEAD_DIM, HEAD_DIM))), BF16)
    tab = pl.BlockSpec((3, tm, LANE), lambda i: (0, i % per_seq, 0))
    return pl.pallas_call(
        _gqa_prep_kernel,
        grid=(N // tm,),
        in_specs=[
            pl.BlockSpec((tm, W), lambda i: (i, 2048 // W)),
            pl.BlockSpec((tm, KW), lambda i: (i, 2944 // KW)),
            pl.BlockSpec((1, W), const),
            pl.BlockSpec((1, KW), const),
            pl.BlockSpec((W, W), const),
            pl.BlockSpec((KW, KW), const),
            tab, tab,
        ],
        out_specs=[
            pl.BlockSpec((tm, N_HEADS * LANE), lambda i: (i, 0)),
            pl.BlockSpec((tm, KW), lambda i: (i, 0)),
        ],
        out_shape=[
            jax.ShapeDtypeStruct((N, N_HEADS * LANE), BF16),
            jax.ShapeDtypeStruct((N, KW), BF16),
        ],
        compiler_params=_params(("parallel",)),
        name="gqa_prepare",
    )(mix, mix, jnp.tile(g_q, N_HEADS).reshape(1, W), jnp.tile(g_k, GQA_KV_HEADS).reshape(1, KW),
      seg(W), seg(KW), q_tab, k_tab)


def _flash_kernel(q_ref, k_ref, v_ref, o_ref, m_sc, l_sc, acc_sc, *, shared_kv):
    kv = pl.program_id(2)
    tq = q_ref.shape[0]

    @pl.when(kv == 0)
    def _():
        m_sc[...] = jnp.full(m_sc.shape, -jnp.inf, F32)
        l_sc[...] = jnp.zeros(l_sc.shape, F32)
        acc_sc[...] = jnp.zeros(acc_sc.shape, F32)

    for u in range(N_HEADS // 2):
        v = v_ref[...] if shared_kv else v_ref[:, u * LANE:(u + 1) * LANE]
        for hh in range(2):
            h = 2 * u + hh
            q = q_ref[:, h * LANE:(h + 1) * LANE]
            k = k_ref[...] if shared_kv else k_ref[:, h * LANE:(h + 1) * LANE]
            s = lax.dot_general(q, k, (((1,), (1,)), ((), ())), preferred_element_type=F32)
            m_prev = m_sc[h][:, :1]
            l_prev = l_sc[h][:, :1]
            m_new = jnp.maximum(m_prev, jnp.max(s, axis=1, keepdims=True))
            alpha = jnp.exp2(m_prev - m_new)
            p = jnp.exp2(s - m_new)
            l_new = alpha * l_prev + jnp.sum(p, axis=1, keepdims=True)
            acc_sc[h] = alpha * acc_sc[h] + jnp.dot(p.astype(BF16), v, preferred_element_type=F32)
            m_sc[h] = jnp.broadcast_to(m_new, (tq, LANE))
            l_sc[h] = jnp.broadcast_to(l_new, (tq, LANE))

    @pl.when(kv == pl.num_programs(2) - 1)
    def _():
        lane = lax.broadcasted_iota(jnp.int32, (tq, LANE), 1)
        lo = lane < HEAD_DIM
        for u in range(N_HEADS // 2):
            a = acc_sc[2 * u] / l_sc[2 * u]
            b = acc_sc[2 * u + 1] / l_sc[2 * u + 1]
            o_ref[:, u * LANE:(u + 1) * LANE] = jnp.where(lo, a, b).astype(o_ref.dtype)


def flash_attention(q, k, v, batch, seq, *, k_col, v_col, shared_kv, name):
    N = q.shape[0]
    tq = _tile(seq, 512)
    tk = _tile(seq, 512)
    nq, nk = seq // tq, seq // tk
    kw = LANE if shared_kv else N_HEADS * LANE
    vw = LANE if shared_kv else N_HEADS * MLA_V
    ow = N_HEADS * HEAD_DIM
    return pl.pallas_call(
        functools.partial(_flash_kernel, shared_kv=shared_kv),
        grid=(batch, nq, nk),
        in_specs=[
            pl.BlockSpec((tq, N_HEADS * LANE), lambda b, i, j: (b * nq + i, 0)),
            pl.BlockSpec((tk, kw), lambda b, i, j: (b * nk + j, k_col)),
            pl.BlockSpec((tk, vw), lambda b, i, j: (b * nk + j, v_col)),
        ],
        out_specs=pl.BlockSpec((tq, ow), lambda b, i, j: (b * nq + i, 0)),
        out_shape=jax.ShapeDtypeStruct((N, ow), BF16),
        scratch_shapes=[pltpu.VMEM((N_HEADS, tq, LANE), F32)] * 3,
        compiler_params=_params(("parallel", "parallel", "arbitrary")),
        name=name,
    )(q, k, v)


def _merge_kernel(oa_ref, ob_ref, oc_ref, sg_ref, x_ref, gt_ref, wb_ref, wo_ref, o_ref):
    D = x_ref.shape[1]
    merged = None
    for i, o_r in enumerate((oa_ref, ob_ref, oc_ref)):
        t = jnp.dot(o_r[...], wb_ref[i], preferred_element_type=F32)
        t = sg_ref[:, i * D:(i + 1) * D].astype(F32) * t
        merged = t if merged is None else merged + t
    mix = jnp.dot(merged.astype(BF16), wo_ref[...], preferred_element_type=F32)
    o_ref[...] = x_ref[...] + gt_ref[...] * mix


def merge_branches(o_a, o_b, o_c, sgate, x, gt, w_branch, w_out, seq):
    N, D = x.shape
    W = o_a.shape[1]
    tm = _tile(seq, 256)
    per_seq = seq // tm
    row = lambda i: (i, 0)
    return pl.pallas_call(
        _merge_kernel,
        grid=(N // tm,),
        in_specs=[
            pl.BlockSpec((tm, W), row), pl.BlockSpec((tm, W), row), pl.BlockSpec((tm, W), row),
            pl.BlockSpec((tm, N_BRANCH * D), row),
            pl.BlockSpec((tm, D), row),
            pl.BlockSpec((None, 1, D), lambda i: (i // per_seq, 0, 0)),
            pl.BlockSpec((N_BRANCH, W, D), lambda i: (0, 0, 0)),
            pl.BlockSpec((D, D), lambda i: (0, 0)),
        ],
        out_specs=pl.BlockSpec((tm, D), row),
        out_shape=jax.ShapeDtypeStruct((N, D), F32),
        compiler_params=_params(("parallel",)),
        name="merge_branches",
    )(o_a, o_b, o_c, sgate, x, gt, w_branch, w_out)


def _router_kernel(x_ref, g_ref, sc_ref, sh_ref, wr_ref, h_out, aff_out):
    h = _norm_mod(x_ref[...], g_ref[...], sc_ref[...], sh_ref[...]).astype(BF16)
    h_out[...] = h
    logits = lax.dot_general(wr_ref[...], h, (((1,), (1,)), ((), ())), preferred_element_type=F32)
    m = jnp.max(logits, axis=0, keepdims=True)
    p = jnp.exp(logits - m)
    aff_out[...] = p / jnp.sum(p, axis=0, keepdims=True)


def router(x, g, sc, sh, w_router_t, seq):
    N, D = x.shape
    E = w_router_t.shape[0]
    tm = _tile(seq, 512)
    per_seq = seq // tm
    return pl.pallas_call(
        _router_kernel,
        grid=(N // tm,),
        in_specs=[
            pl.BlockSpec((tm, D), lambda i: (i, 0)),
            pl.BlockSpec((1, D), lambda i: (0, 0)),
            pl.BlockSpec((None, 1, D), lambda i: (i // per_seq, 0, 0)),
            pl.BlockSpec((None, 1, D), lambda i: (i // per_seq, 0, 0)),
            pl.BlockSpec((E, D), lambda i: (0, 0)),
        ],
        out_specs=[pl.BlockSpec((tm, D), lambda i: (i, 0)), pl.BlockSpec((E, tm), lambda i: (0, i))],
        out_shape=[jax.ShapeDtypeStruct((N, D), BF16), jax.ShapeDtypeStruct((E, N), F32)],
        compiler_params=_params(("parallel",)),
        name="router",
    )(x, g.reshape(1, D), sc, sh, w_router_t)


def _expert_kernel(x_ref, g_ref, wg_ref, wu_ref, wd_ref, o_ref):
    x = x_ref[...]
    a = jnp.dot(x, wg_ref[...], preferred_element_type=F32)
    u = jnp.dot(x, wu_ref[...], preferred_element_type=F32)
    hmid = (a * jax.nn.sigmoid(a) * u).astype(BF16)
    y = jnp.dot(hmid, wd_ref[...], preferred_element_type=F32)
    o_ref[...] = (y * g_ref[...]).astype(o_ref.dtype)


def expert_ffn(xe, ge, w_gate, w_up, w_down):
    E, C, D = xe.shape
    F = w_gate.shape[2]
    tc = _tile(C, 512)
    return pl.pallas_call(
        _expert_kernel,
        grid=(E, C // tc),
        in_specs=[
            pl.BlockSpec((None, tc, D), lambda e, i: (e, i, 0)),
            pl.BlockSpec((None, tc, 1), lambda e, i: (e, i, 0)),
            pl.BlockSpec((None, D, F), lambda e, i: (e, 0, 0)),
            pl.BlockSpec((None, D, F), lambda e, i: (e, 0, 0)),
            pl.BlockSpec((None, F, D), lambda e, i: (e, 0, 0)),
        ],
        out_specs=pl.BlockSpec((None, tc, D), lambda e, i: (e, i, 0)),
        out_shape=jax.ShapeDtypeStruct((E, C, D), F32),
        compiler_params=_params(("parallel", "arbitrary")),
        name="expert_ffn",
    )(xe, ge, w_gate, w_up, w_down)


def _residual_kernel(x_ref, y_ref, gt_ref, o_ref):
    o_ref[...] = x_ref[...] + gt_ref[...] * y_ref[...]


def _residual_final_kernel(x_ref, y_ref, gt_ref, g_ref, o_ref):
    x = x_ref[...] + gt_ref[...] * y_ref[...]
    ms = jnp.mean(x * x, axis=-1, keepdims=True)
    o_ref[...] = x * lax.rsqrt(ms + EPS) * g_ref[...]


def residual(x, y, gt, seq, g_final=None):
    N, D = x.shape
    tm = _tile(seq, 512)
    per_seq = seq // tm
    row = lambda i: (i, 0)
    in_specs = [pl.BlockSpec((tm, D), row), pl.BlockSpec((tm, D), row),
                pl.BlockSpec((None, 1, D), lambda i: (i // per_seq, 0, 0))]
    args = [x, y, gt]
    body = _residual_kernel
    if g_final is not None:
        in_specs.append(pl.BlockSpec((1, D), lambda i: (0, 0)))
        args.append(g_final.reshape(1, D))
        body = _residual_final_kernel
    return pl.pallas_call(
        body,
        grid=(N // tm,),
        in_specs=in_specs,
        out_specs=pl.BlockSpec((tm, D), row),
        out_shape=jax.ShapeDtypeStruct((N, D), F32),
        compiler_params=_params(("parallel",)),
        name="residual",
    )(*args)


def _prepare_layer(l, p, D):
    w_in = p['w_in'][l]
    W = N_HEADS * HEAD_DIM
    KW = GQA_KV_HEADS * HEAD_DIM
    qr = p['mla_w_uq'].shape[1]
    kvr = p['mla_w_ukv'].shape[1]
    o = 0
    seg = {}
    for name, width in (('a', 3 * W), ('cq', qr), ('ckv', kvr), ('kr', MLA_ROPE), ('gq', W), ('gk', KW), ('gv', KW),
                        ('gate', N_BRANCH * D)):
        seg[name] = w_in[:, o:o + width]
        o += width
    gq_cols = np.concatenate([np.arange(h * HEAD_DIM, (h + 1) * HEAD_DIM) for h in GQA_HEAD_ORDER])
    zeros = lambda n: jnp.zeros((D, n), w_in.dtype)
    w_mix = jnp.concatenate(
        [seg['a'], seg['cq'], seg['gq'][:, gq_cols], seg['ckv'],
         zeros(MLA_NOPE), seg['kr'], zeros(LANE - MLA_NOPE - MLA_ROPE), seg['gk'], seg['gv']], axis=1).astype(BF16)

    w_uq = p['mla_w_uq'][l].reshape(qr, N_HEADS, MLA_NOPE + MLA_ROPE)
    wq = jnp.pad(w_uq, ((0, 0), (0, 0), (0, LANE - MLA_NOPE - MLA_ROPE))).reshape(qr, N_HEADS * LANE).astype(BF16)
    w_ukv = p['mla_w_ukv'][l].reshape(kvr, N_HEADS, MLA_NOPE + MLA_V)
    wkn = jnp.pad(w_ukv[:, :, :MLA_NOPE], ((0, 0), (0, 0), (0, LANE - MLA_NOPE))).reshape(kvr, N_HEADS * LANE)
    wv = w_ukv[:, :, MLA_NOPE:].reshape(kvr, N_HEADS * MLA_V)

    w_branch = p['w_branch'][l]
    w_branch = jnp.stack([w_branch[0], w_branch[1], w_branch[2][gq_cols]]).astype(BF16)
    return dict(
        w_mix=w_mix, w_gate_logits=seg['gate'].astype(BF16),
        wq=wq, wkn=wkn.astype(BF16), wv=wv.astype(BF16),
        w_branch=w_branch, w_out=p['w_out'][l].astype(BF16),
        w_router_t=p['w_router'][l].T.astype(BF16),
        w_exp_gate=p['w_exp_gate'][l].astype(BF16), w_exp_up=p['w_exp_up'][l].astype(BF16),
        w_exp_down=p['w_exp_down'][l].astype(BF16),
    )


def _trunk(x3, mod, p, layers, tables):
    B, T, D = x3.shape
    N = B * T
    x = x3.reshape(N, D)
    depth = len(layers)
    for l in range(depth):
        w = layers[l]
        sh1, sc1, gt1, sh2, sc2, gt2 = [mod[l][:, None, i * D:(i + 1) * D] for i in range(6)]
        mix = in_projection(x, p['g_mix'][l], sc1, sh1, w['w_mix'], T, tn=640, sigmoid_out=False, name="mix_projection")
        sgate = in_projection(x, p['g_mix'][l], sc1, sh1, w['w_gate_logits'], T, tn=_tile(N_BRANCH * D, 768),
                              sigmoid_out=True, name="gate_projection")
        o_a = neighborhood_attention(mix, tables['na_bias'][l], B, T)
        qm, km, vm = mla_prepare(mix, p['mla_g_q'][l], p['mla_g_kv'][l], w['wq'], w['wkn'], w['wv'],
                                 tables['mla_q'][T], tables['mla_k'][T], T)
        o_b = flash_attention(qm, km, vm, B, T, k_col=0, v_col=0, shared_kv=False, name="mla_attention")
        qg, kg = gqa_prepare(mix, p['gqa_g_q'][l], p['gqa_g_k'][l], tables['gqa_q'][T], tables['gqa_k'][T], T)
        o_c = flash_attention(qg, kg, mix, B, T, k_col=0, v_col=3072 // LANE, shared_kv=True, name="gqa_attention")
        x = merge_branches(o_a, o_b, o_c, sgate, x, gt1, w['w_branch'], w['w_out'], T)

        h, aff = router(x, p['g_ffn'][l], sc2, sh2, w['w_router_t'], T)
        cap = EC_CAPACITY * N // N_EXPERTS
        g, idx = lax.top_k(aff, cap)
        xe = h[idx]
        ye = expert_ffn(xe, g[..., None], w['w_exp_gate'], w['w_exp_up'], w['w_exp_down'])
        y = jnp.zeros((N, D), F32).at[idx.reshape(-1)].add(ye.reshape(-1, D))
        x = residual(x, y, gt2, T, g_final=p['g_final'] if l == depth - 1 else None)
    return x.reshape(B, T, D)


def kernel(x_prompt, x_sample, c_prompt, c_sample, w_ada, b_ada, g_mix, g_ffn, w_in, na_rel_bias, mla_g_q, mla_w_uq,
           mla_g_kv, mla_w_ukv, gqa_g_q, gqa_g_k, w_branch, w_out, w_router, w_exp_gate, w_exp_up, w_exp_down,
           g_final):
    p = dict(g_mix=g_mix, g_ffn=g_ffn, w_in=w_in, mla_g_q=mla_g_q, mla_w_uq=mla_w_uq, mla_g_kv=mla_g_kv,
             mla_w_ukv=mla_w_ukv, gqa_g_q=gqa_g_q, gqa_g_k=gqa_g_k, w_branch=w_branch, w_out=w_out,
             w_router=w_router, w_exp_gate=w_exp_gate, w_exp_up=w_exp_up, w_exp_down=w_exp_down, g_final=g_final)
    depth, D = g_mix.shape
    bp, bs = c_prompt.shape[0], c_sample.shape[0]
    rows = -(-(bp + bs) // 8) * 8
    c_all = jnp.concatenate([c_prompt, c_sample, jnp.zeros((rows - bp - bs, D), F32)], axis=0)
    mod = ada_modulation(c_all, w_ada, b_ada)
    layers = [_prepare_layer(l, p, D) for l in range(depth)]
    tables = dict(na_bias=[_na_bias_table(na_rel_bias[l]) for l in range(depth)],
                  mla_q={}, mla_k={}, gqa_q={}, gqa_k={})
    for T in {x_prompt.shape[1], x_sample.shape[1]}:
        tables['mla_q'][T], tables['mla_k'][T] = _mla_tables(T)
        tables['gqa_q'][T], tables['gqa_k'][T] = _gqa_tables(T)
    y_prompt = _trunk(x_prompt, mod[:, :bp], p, layers, tables)
    y_sample = _trunk(x_sample, mod[:, bp:bp + bs], p, layers, tables)
    return (y_prompt, y_sample)
```

```python
import functools
import math

import numpy as np
import jax
import jax.numpy as jnp
from jax import lax
from jax.experimental import pallas as pl
from jax.experimental.pallas import tpu as pltpu

F32 = jnp.float32
BF16 = jnp.bfloat16

GRID_W = 64
HEAD_DIM = 64
N_HEADS = 8
NA_WIN_H = 8
NA_WIN_W = 16
MLA_NOPE = 64
MLA_ROPE = 32
MLA_V = 64
GQA_KV_HEADS = 2
N_BRANCH = 3
N_EXPERTS = 16
EC_CAPACITY = 2
ROPE_THETA = 10000.0
EPS = 1e-6
LOG2E = 1.4426950408889634

LANE = 128
NA_ROWS_PER_BLOCK = 4
NA_BLOCK = NA_ROWS_PER_BLOCK * GRID_W
NA_KEY_BLOCKS = 3
NEG = -1e30
VMEM_LIMIT = 56 * 1024 * 1024

GQA_HEAD_ORDER = (0, 4, 1, 5, 2, 6, 3, 7)


def _params(sem):
    return pltpu.CompilerParams(dimension_semantics=sem, vmem_limit_bytes=VMEM_LIMIT)


def _tile(n, pref):
    t = min(n, pref)
    while n % t:
        t //= 2
    return t


def _ada_kernel(c_ref, w_ref, b_ref, o_ref):
    c = c_ref[...]
    a = (c * jax.nn.sigmoid(c)).astype(BF16)
    o_ref[...] = jnp.dot(a, w_ref[...].astype(BF16), preferred_element_type=F32) + b_ref[...]


def ada_modulation(c, w_ada, b_ada):
    L, D, D6 = w_ada.shape
    R = c.shape[0]
    tn = _tile(D6, 1024)
    return pl.pallas_call(
        _ada_kernel,
        grid=(L, D6 // tn),
        in_specs=[
            pl.BlockSpec((R, D), lambda l, j: (0, 0)),
            pl.BlockSpec((None, D, tn), lambda l, j: (l, 0, j)),
            pl.BlockSpec((None, 1, tn), lambda l, j: (l, 0, j)),
        ],
        out_specs=pl.BlockSpec((None, R, tn), lambda l, j: (l, 0, j)),
        out_shape=jax.ShapeDtypeStruct((L, R, D6), F32),
        compiler_params=_params(("parallel", "parallel")),
        name="ada_modulation",
    )(c, w_ada, b_ada.reshape(L, 1, D6))


def _norm_mod(x, g, sc, sh):
    ms = jnp.mean(x * x, axis=-1, keepdims=True)
    y = x * lax.rsqrt(ms + EPS)
    return (y * g) * (1.0 + sc) + sh


def _in_proj_kernel(x_ref, g_ref, sc_ref, sh_ref, w_ref, o_ref, h_sc, *, sigmoid_out):
    @pl.when(pl.program_id(1) == 0)
    def _():
        h_sc[...] = _norm_mod(x_ref[...], g_ref[...], sc_ref[...], sh_ref[...]).astype(BF16)

    acc = jnp.dot(h_sc[...], w_ref[...], preferred_element_type=F32)
    if sigmoid_out:
        acc = jax.nn.sigmoid(acc)
    o_ref[...] = acc.astype(o_ref.dtype)


def in_projection(x, g, sc, sh, w, seq, *, tn, sigmoid_out, name):
    N, D = x.shape
    C = w.shape[1]
    tm = _tile(seq, 1024)
    per_seq = seq // tm
    return pl.pallas_call(
        functools.partial(_in_proj_kernel, sigmoid_out=sigmoid_out),
        grid=(N // tm, C // tn),
        in_specs=[
            pl.BlockSpec((tm, D), lambda i, j: (i, 0)),
            pl.BlockSpec((1, D), lambda i, j: (0, 0)),
            pl.BlockSpec((None, 1, D), lambda i, j: (i // per_seq, 0, 0)),
            pl.BlockSpec((None, 1, D), lambda i, j: (i // per_seq, 0, 0)),
            pl.BlockSpec((D, tn), lambda i, j: (0, j)),
        ],
        out_specs=pl.BlockSpec((tm, tn), lambda i, j: (i, j)),
        out_shape=jax.ShapeDtypeStruct((N, C), BF16),
        scratch_shapes=[pltpu.VMEM((tm, D), BF16)],
        compiler_params=_params(("parallel", "arbitrary")),
        name=name,
    )(x, g.reshape(1, D), sc, sh, w)


def _na_bias_table(rel_bias):
    R = NA_ROWS_PER_BLOCK
    KR = NA_KEY_BLOCKS * R
    q_rows = np.stack([np.arange(R), R + np.arange(R), 2 * R + np.arange(R)])
    w_start = np.stack([np.zeros(R, np.int64), np.arange(R), np.full(R, KR - NA_WIN_H)])
    col = np.arange(GRID_W)
    col_start = np.clip(col - NA_WIN_W // 2, 0, GRID_W - NA_WIN_W)
    k_row = np.arange(KR)
    k_col = np.arange(GRID_W)
    row_ok = (k_row[None, None, :] >= w_start[:, :, None]) & (k_row[None, None, :] < w_start[:, :, None] + NA_WIN_H)
    col_ok = (k_col[None, :] >= col_start[:, None]) & (k_col[None, :] < col_start[:, None] + NA_WIN_W)
    dr = np.clip(k_row[None, None, :] - q_rows[:, :, None] + (NA_WIN_H - 1), 0, 2 * NA_WIN_H - 2)
    dc = np.clip(k_col[None, :] - col[:, None] + (NA_WIN_W - 1), 0, 2 * NA_WIN_W - 2)
    valid = row_ok[:, :, None, :, None] & col_ok[None, None, :, None, :]
    dr_b = np.broadcast_to(dr[:, :, None, :, None], valid.shape)
    dc_b = np.broadcast_to(dc[None, None, :, None, :], valid.shape)
    bias = rel_bias.astype(F32)[:, dr_b, dc_b]
    bias = jnp.where(valid[None], bias, NEG)
    H = rel_bias.shape[0]
    return jnp.transpose(bias, (1, 0, 2, 3, 4, 5)).reshape(3, H, R * GRID_W, KR * GRID_W)


def _na_kernel(q_ref, k0_ref, k1_ref, k2_ref, v0_ref, v1_ref, v2_ref, bias_ref, o_ref):
    lane = lax.broadcasted_iota(jnp.int32, (NA_BLOCK, LANE), 1)
    lo = lane < HEAD_DIM
    scale = HEAD_DIM ** -0.5
    for u in range(N_HEADS // 2):
        cs = slice(u * LANE, (u + 1) * LANE)
        q = q_ref[:, cs] * scale
        ks = (k0_ref[:, cs], k1_ref[:, cs], k2_ref[:, cs])
        vs = (v0_ref[:, cs], v1_ref[:, cs], v2_ref[:, cs])
        outs = []
        for hh in range(2):
            qh = jnp.where(lo if hh == 0 else ~lo, q, jnp.zeros_like(q))
            s = jnp.concatenate(
                [lax.dot_general(qh, k, (((1,), (1,)), ((), ())), preferred_element_type=F32) for k in ks], axis=1)
            s = s + bias_ref[2 * u + hh]
            m = jnp.max(s, axis=1, keepdims=True)
            p = jnp.exp(s - m)
            l = jnp.sum(p, axis=1, keepdims=True)
            pb = p.astype(BF16)
            o = None
            for j in range(NA_KEY_BLOCKS):
                t = jnp.dot(pb[:, j * NA_BLOCK:(j + 1) * NA_BLOCK], vs[j], preferred_element_type=F32)
                o = t if o is None else o + t
            outs.append(o / l)
        o_ref[:, cs] = jnp.where(lo, outs[0], outs[1]).astype(o_ref.dtype)


def neighborhood_attention(mix, bias_table, batch, seq):
    N = mix.shape[0]
    W = N_HEADS * HEAD_DIM
    nb = seq // NA_BLOCK
    assert seq % NA_BLOCK == 0 and nb >= NA_KEY_BLOCKS

    def kv_spec(colblk, j):
        return pl.BlockSpec(
            (NA_BLOCK, W),
            lambda b, i: (b * nb + jnp.clip(i - 1, 0, nb - NA_KEY_BLOCKS) + j, colblk))

    def bias_idx(b, i):
        return (jnp.where(i == 0, 0, jnp.where(i == nb - 1, 2, 1)), 0, 0, 0)

    return pl.pallas_call(
        _na_kernel,
        grid=(batch, nb),
        in_specs=[pl.BlockSpec((NA_BLOCK, W), lambda b, i: (b * nb + i, 0))]
        + [kv_spec(1, j) for j in range(NA_KEY_BLOCKS)]
        + [kv_spec(2, j) for j in range(NA_KEY_BLOCKS)]
        + [pl.BlockSpec((None, N_HEADS, NA_BLOCK, NA_KEY_BLOCKS * NA_BLOCK), bias_idx)],
        out_specs=pl.BlockSpec((NA_BLOCK, W), lambda b, i: (b * nb + i, 0)),
        out_shape=jax.ShapeDtypeStruct((N, W), BF16),
        compiler_params=_params(("parallel", "arbitrary")),
        name="neighborhood_attention",
    )(mix, mix, mix, mix, mix, mix, mix, bias_table)


def _rope_angles(pos, dim):
    inv_freq = ROPE_THETA ** (-jnp.arange(0, dim, 2, dtype=F32) / dim)
    ang = pos.astype(F32)[:, None] * inv_freq[None, :]
    return jnp.concatenate([ang, ang], axis=-1)


def _rot_tables(cos, sin, scale):
    n = cos.shape[1]
    first = (np.arange(n) % 32) < 16
    sa = jnp.where(first[None, :], -sin, 0.0)
    sb = jnp.where(first[None, :], 0.0, sin)
    return cos * scale, sa * scale, sb * scale


def _mla_tables(seq):
    pos = jnp.arange(seq)
    ang = _rope_angles(pos, MLA_ROPE)
    cos, sin = jnp.cos(ang), jnp.sin(ang)
    pad_hi = LANE - MLA_NOPE - MLA_ROPE

    def place(t, nope_val):
        return jnp.concatenate(
            [jnp.full((seq, MLA_NOPE), nope_val, F32), t, jnp.zeros((seq, pad_hi), F32)], axis=1)

    qs = (MLA_NOPE + MLA_ROPE) ** -0.5 * LOG2E
    qc, qa, qb = _rot_tables(cos, sin, qs)
    kc, ka, kb = _rot_tables(cos, sin, 1.0)
    q_tab = jnp.stack([place(qc, qs), place(qa, 0.0), place(qb, 0.0)])
    k_tab = jnp.stack([place(kc, 0.0), place(ka, 0.0), place(kb, 0.0)])
    return q_tab, k_tab


def _gqa_tables(seq):
    pos = jnp.arange(seq)
    half = HEAD_DIM // 2
    ar = _rope_angles(pos // GRID_W, half)
    ac = _rope_angles(pos % GRID_W, half)
    cos = jnp.concatenate([jnp.cos(ar), jnp.cos(ac)] * 2, axis=1)
    sin = jnp.concatenate([jnp.sin(ar), jnp.sin(ac)] * 2, axis=1)
    qs = HEAD_DIM ** -0.5 * LOG2E
    return jnp.stack(_rot_tables(cos, sin, qs)), jnp.stack(_rot_tables(cos, sin, 1.0))


def _apply_rot(x, tab_ref):
    n = x.shape[1]
    return (x * tab_ref[0] + pltpu.roll(x, n - 16, 1) * tab_ref[1] + pltpu.roll(x, 16, 1) * tab_ref[2])


def _rms(x, g):
    ms = jnp.mean(x * x, axis=-1, keepdims=True)
    return x * lax.rsqrt(ms + EPS) * g


def _mla_prep_kernel(cq_ref, ckv_ref, kr_ref, gq_ref, gkv_ref, wq_ref, wkn_ref, wv_ref, qtab_ref, ktab_ref,
                     q_out, k_out, v_out):
    cqn = _rms(cq_ref[...].astype(F32), gq_ref[...]).astype(BF16)
    ckvn = _rms(ckv_ref[...].astype(F32), gkv_ref[...]).astype(BF16)
    qf = jnp.dot(cqn, wq_ref[...], preferred_element_type=F32)
    kn = jnp.dot(ckvn, wkn_ref[...], preferred_element_type=F32)
    tm = cq_ref.shape[0]
    hi_lane = lax.broadcasted_iota(jnp.int32, (tm, LANE), 1) >= MLA_V
    vf = jnp.dot(ckvn, wv_ref[...], preferred_element_type=F32)
    k_rope = _apply_rot(kr_ref[...].astype(F32), ktab_ref)
    for h in range(N_HEADS):
        cs = slice(h * LANE, (h + 1) * LANE)
        q_out[:, cs] = _apply_rot(qf[:, cs], qtab_ref).astype(q_out.dtype)
        k_out[:, cs] = (kn[:, cs] + k_rope).astype(k_out.dtype)
        v_out[:, cs] = jnp.where(hi_lane, 1.0, vf[:, cs]).astype(v_out.dtype)


def mla_prepare(mix, g_q, g_kv, wq, wkn, wv, q_tab, k_tab, seq):
    N = mix.shape[0]
    tm = _tile(seq, 512)
    per_seq = seq // tm
    qr, kvr = wq.shape[0], wkn.shape[0]
    cq_blk = 1536 // qr
    ckv_blk = 2560 // kvr
    kr_blk = 2816 // LANE
    const = lambda i: (0, 0)
    tab = pl.BlockSpec((3, tm, LANE), lambda i: (0, i % per_seq, 0))
    return pl.pallas_call(
        _mla_prep_kernel,
        grid=(N // tm,),
        in_specs=[
            pl.BlockSpec((tm, qr), lambda i: (i, cq_blk)),
            pl.BlockSpec((tm, kvr), lambda i: (i, ckv_blk)),
            pl.BlockSpec((tm, LANE), lambda i: (i, kr_blk)),
            pl.BlockSpec((1, qr), const),
            pl.BlockSpec((1, kvr), const),
            pl.BlockSpec(wq.shape, const),
            pl.BlockSpec(wkn.shape, const),
            pl.BlockSpec(wv.shape, const),
            tab, tab,
        ],
        out_specs=[
            pl.BlockSpec((tm, N_HEADS * LANE), lambda i: (i, 0)),
            pl.BlockSpec((tm, N_HEADS * LANE), lambda i: (i, 0)),
            pl.BlockSpec((tm, N_HEADS * LANE), lambda i: (i, 0)),
        ],
        out_shape=[jax.ShapeDtypeStruct((N, N_HEADS * LANE), BF16)] * 3,
        compiler_params=_params(("parallel",)),
        name="mla_prepare",
    )(mix, mix, mix, g_q.reshape(1, qr), g_kv.reshape(1, kvr), wq, wkn, wv, q_tab, k_tab)


def _head_rms(x, g, seg_ref):
    sq = x * x
    hi = sq.astype(BF16)
    lo = (sq - hi.astype(F32)).astype(BF16)
    ms = (jnp.dot(hi, seg_ref[...], preferred_element_type=F32)
          + jnp.dot(lo, seg_ref[...], preferred_element_type=F32)) * (1.0 / HEAD_DIM)
    return x * lax.rsqrt(ms + EPS) * g


def _gqa_prep_kernel(q_ref, k_ref, v_ref, gq_ref, gk_ref, segq_ref, segk_ref, qtab_ref, ktab_ref,
                     q_out, k_out, v_out):
    tm = q_ref.shape[0]
    lane = lax.broadcasted_iota(jnp.int32, (tm, LANE), 1)
    lo = lane < HEAD_DIM
    for g in range(GQA_KV_HEADS):
        cs = slice(g * LANE, (g + 1) * LANE)
        v_out[:, cs] = jnp.where(lo, v_ref[:, cs], jnp.ones((tm, LANE), v_ref.dtype))
    qn = _head_rms(q_ref[...].astype(F32), gq_ref[...], segq_ref)
    for u in range(N_HEADS // 2):
        y = _apply_rot(qn[:, u * LANE:(u + 1) * LANE], qtab_ref)
        q_out[:, (2 * u) * LANE:(2 * u + 1) * LANE] = jnp.where(lo, y, 0.0).astype(q_out.dtype)
        q_out[:, (2 * u + 1) * LANE:(2 * u + 2) * LANE] = jnp.where(lo, 0.0, y).astype(q_out.dtype)
    kn = _head_rms(k_ref[...].astype(F32), gk_ref[...], segk_ref)
    k_out[...] = _apply_rot(kn, ktab_ref).astype(k_out.dtype)


def gqa_prepare(mix, g_q, g_k, q_tab, k_tab, seq):
    N = mix.shape[0]
    tm = _tile(seq, 512)
    per_seq = seq // tm
    W = N_HEADS * HEAD_DIM
    KW = GQA_KV_HEADS * HEAD_DIM
    VW = GQA_KV_HEADS * LANE
    const = lambda i: (0, 0)
    seg = lambda n: jnp.asarray(np.kron(np.eye(n // HEAD_DIM), np.ones((HEAD_DIM, HEAD_DIM))), BF16)
    tab = pl.BlockSpec((3, tm, LANE), lambda i: (0, i % per_seq, 0))
    return pl.pallas_call(
        _gqa_prep_kernel,
        grid=(N // tm,),
        in_specs=[
            pl.BlockSpec((tm, W), lambda i: (i, 2048 // W)),
            pl.BlockSpec((tm, KW), lambda i: (i, 2944 // KW)),
            pl.BlockSpec((tm, VW), lambda i: (i, 3072 // VW)),
            pl.BlockSpec((1, W), const),
            pl.BlockSpec((1, KW), const),
            pl.BlockSpec((W, W), const),
            pl.BlockSpec((KW, KW), const),
            tab, tab,
        ],
        out_specs=[
            pl.BlockSpec((tm, N_HEADS * LANE), lambda i: (i, 0)),
            pl.BlockSpec((tm, KW), lambda i: (i, 0)),
            pl.BlockSpec((tm, VW), lambda i: (i, 0)),
        ],
        out_shape=[
            jax.ShapeDtypeStruct((N, N_HEADS * LANE), BF16),
            jax.ShapeDtypeStruct((N, KW), BF16),
            jax.ShapeDtypeStruct((N, VW), BF16),
        ],
        compiler_params=_params(("parallel",)),
        name="gqa_prepare",
    )(mix, mix, mix, jnp.tile(g_q, N_HEADS).reshape(1, W), jnp.tile(g_k, GQA_KV_HEADS).reshape(1, KW),
      seg(W), seg(KW), q_tab, k_tab)


def _flash_kernel(q_ref, k_ref, v_ref, o_ref, m_sc, acc_sc, *, shared_k, v_groups):
    kv = pl.program_id(2)
    tq = q_ref.shape[0]
    tk = k_ref.shape[0]

    @pl.when(kv == 0)
    def _():
        m_sc[...] = jnp.full(m_sc.shape, -jnp.inf, F32)
        acc_sc[...] = jnp.zeros(acc_sc.shape, F32)

    for h in range(N_HEADS):
        q = q_ref[:, h * LANE:(h + 1) * LANE]
        k = k_ref[...] if shared_k else k_ref[:, h * LANE:(h + 1) * LANE]
        g = h % v_groups
        v = v_ref[:, g * LANE:(g + 1) * LANE]
        s = lax.dot_general(q, k, (((1,), (1,)), ((), ())), preferred_element_type=F32)
        m_prev = m_sc[h]
        m_new = jnp.maximum(m_prev, jnp.max(s, axis=1, keepdims=True))
        alpha = jnp.exp2(m_prev - m_new)
        p = jnp.exp2(s - pltpu.repeat(m_new, tk // LANE, axis=1))
        acc_sc[h] = alpha * acc_sc[h] + jnp.dot(p.astype(BF16), v, preferred_element_type=F32)
        m_sc[h] = m_new

    @pl.when(kv == pl.num_programs(2) - 1)
    def _():
        lane = lax.broadcasted_iota(jnp.int32, (tq, LANE), 1)
        lo = lane < HEAD_DIM
        for u in range(N_HEADS // 2):
            a = acc_sc[2 * u]
            b = acc_sc[2 * u + 1]
            a = a / pltpu.roll(a, HEAD_DIM, 1)
            b = pltpu.roll(b, HEAD_DIM, 1) / b
            o_ref[:, u * LANE:(u + 1) * LANE] = jnp.where(lo, a, b).astype(o_ref.dtype)


def flash_attention(q, k, v, batch, seq, *, shared_k, name):
    N = q.shape[0]
    tq = _tile(seq, 1024)
    tk = _tile(seq, 1024)
    nq, nk = seq // tq, seq // tk
    kw, vw = k.shape[1], v.shape[1]
    ow = N_HEADS * HEAD_DIM
    return pl.pallas_call(
        functools.partial(_flash_kernel, shared_k=shared_k, v_groups=vw // LANE),
        grid=(batch, nq, nk),
        in_specs=[
            pl.BlockSpec((tq, N_HEADS * LANE), lambda b, i, j: (b * nq + i, 0)),
            pl.BlockSpec((tk, kw), lambda b, i, j: (b * nk + j, 0)),
            pl.BlockSpec((tk, vw), lambda b, i, j: (b * nk + j, 0)),
        ],
        out_specs=pl.BlockSpec((tq, ow), lambda b, i, j: (b * nq + i, 0)),
        out_shape=jax.ShapeDtypeStruct((N, ow), BF16),
        scratch_shapes=[pltpu.VMEM((N_HEADS, tq, LANE), F32)] * 2,
        compiler_params=_params(("parallel", "parallel", "arbitrary")),
        name=name,
    )(q, k, v)


def _merge_kernel(oa_ref, ob_ref, oc_ref, sg_ref, x_ref, gt_ref, wb_ref, wo_ref, o_ref):
    D = x_ref.shape[1]
    merged = None
    for i, o_r in enumerate((oa_ref, ob_ref, oc_ref)):
        t = jnp.dot(o_r[...], wb_ref[i], preferred_element_type=F32)
        t = sg_ref[:, i * D:(i + 1) * D].astype(F32) * t
        merged = t if merged is None else merged + t
    mix = jnp.dot(merged.astype(BF16), wo_ref[...], preferred_element_type=F32)
    o_ref[...] = x_ref[...] + gt_ref[...] * mix


def merge_branches(o_a, o_b, o_c, sgate, x, gt, w_branch, w_out, seq):
    N, D = x.shape
    W = o_a.shape[1]
    tm = _tile(seq, 256)
    per_seq = seq // tm
    row = lambda i: (i, 0)
    return pl.pallas_call(
        _merge_kernel,
        grid=(N // tm,),
        in_specs=[
            pl.BlockSpec((tm, W), row), pl.BlockSpec((tm, W), row), pl.BlockSpec((tm, W), row),
            pl.BlockSpec((tm, N_BRANCH * D), row),
            pl.BlockSpec((tm, D), row),
            pl.BlockSpec((None, 1, D), lambda i: (i // per_seq, 0, 0)),
            pl.BlockSpec((N_BRANCH, W, D), lambda i: (0, 0, 0)),
            pl.BlockSpec((D, D), lambda i: (0, 0)),
        ],
        out_specs=pl.BlockSpec((tm, D), row),
        out_shape=jax.ShapeDtypeStruct((N, D), F32),
        compiler_params=_params(("parallel",)),
        name="merge_branches",
    )(o_a, o_b, o_c, sgate, x, gt, w_branch, w_out)


def _router_kernel(x_ref, g_ref, sc_ref, sh_ref, wr_ref, h_out, aff_out):
    h = _norm_mod(x_ref[...], g_ref[...], sc_ref[...], sh_ref[...]).astype(BF16)
    h_out[...] = h
    logits = lax.dot_general(wr_ref[...], h, (((1,), (1,)), ((), ())), preferred_element_type=F32)
    m = jnp.max(logits, axis=0, keepdims=True)
    p = jnp.exp(logits - m)
    aff_out[...] = p / jnp.sum(p, axis=0, keepdims=True)


def router(x, g, sc, sh, w_router_t, seq):
    N, D = x.shape
    E = w_router_t.shape[0]
    tm = _tile(seq, 512)
    per_seq = seq // tm
    return pl.pallas_call(
        _router_kernel,
        grid=(N // tm,),
        in_specs=[
            pl.BlockSpec((tm, D), lambda i: (i, 0)),
            pl.BlockSpec((1, D), lambda i: (0, 0)),
            pl.BlockSpec((None, 1, D), lambda i: (i // per_seq, 0, 0)),
            pl.BlockSpec((None, 1, D), lambda i: (i // per_seq, 0, 0)),
            pl.BlockSpec((E, D), lambda i: (0, 0)),
        ],
        out_specs=[pl.BlockSpec((tm, D), lambda i: (i, 0)), pl.BlockSpec((E, tm), lambda i: (0, i))],
        out_shape=[jax.ShapeDtypeStruct((N, D), BF16), jax.ShapeDtypeStruct((E, N), F32)],
        compiler_params=_params(("parallel",)),
        name="router",
    )(x, g.reshape(1, D), sc, sh, w_router_t)


def _expert_kernel(x_ref, g_ref, wg_ref, wu_ref, wd_ref, o_ref):
    x = x_ref[...]
    a = jnp.dot(x, wg_ref[...], preferred_element_type=F32)
    u = jnp.dot(x, wu_ref[...], preferred_element_type=F32)
    hmid = (a * jax.nn.sigmoid(a) * u).astype(BF16)
    y = jnp.dot(hmid, wd_ref[...], preferred_element_type=F32)
    o_ref[...] = (y * g_ref[...]).astype(o_ref.dtype)


def expert_ffn(xe, ge, w_gate, w_up, w_down):
    E, C, D = xe.shape
    F = w_gate.shape[2]
    tc = _tile(C, 512)
    return pl.pallas_call(
        _expert_kernel,
        grid=(E, C // tc),
        in_specs=[
            pl.BlockSpec((None, tc, D), lambda e, i: (e, i, 0)),
            pl.BlockSpec((None, tc, 1), lambda e, i: (e, i, 0)),
            pl.BlockSpec((None, D, F), lambda e, i: (e, 0, 0)),
            pl.BlockSpec((None, D, F), lambda e, i: (e, 0, 0)),
            pl.BlockSpec((None, F, D), lambda e, i: (e, 0, 0)),
        ],
        out_specs=pl.BlockSpec((None, tc, D), lambda e, i: (e, i, 0)),
        out_shape=jax.ShapeDtypeStruct((E, C, D), F32),
        compiler_params=_params(("parallel", "arbitrary")),
        name="expert_ffn",
    )(xe, ge, w_gate, w_up, w_down)


def _residual_kernel(x_ref, y_ref, gt_ref, o_ref):
    o_ref[...] = x_ref[...] + gt_ref[...] * y_ref[...]


def _residual_final_kernel(x_ref, y_ref, gt_ref, g_ref, o_ref):
    x = x_ref[...] + gt_ref[...] * y_ref[...]
    ms = jnp.mean(x * x, axis=-1, keepdims=True)
    o_ref[...] = x * lax.rsqrt(ms + EPS) * g_ref[...]


def residual(x, y, gt, seq, g_final=None):
    N, D = x.shape
    tm = _tile(seq, 512)
    per_seq = seq // tm
    row = lambda i: (i, 0)
    in_specs = [pl.BlockSpec((tm, D), row), pl.BlockSpec((tm, D), row),
                pl.BlockSpec((None, 1, D), lambda i: (i // per_seq, 0, 0))]
    args = [x, y, gt]
    body = _residual_kernel
    if g_final is not None:
        in_specs.append(pl.BlockSpec((1, D), lambda i: (0, 0)))
        args.append(g_final.reshape(1, D))
        body = _residual_final_kernel
    return pl.pallas_call(
        body,
        grid=(N // tm,),
        in_specs=in_specs,
        out_specs=pl.BlockSpec((tm, D), row),
        out_shape=jax.ShapeDtypeStruct((N, D), F32),
        compiler_params=_params(("parallel",)),
        name="residual",
    )(*args)


def _prepare_layer(l, p, D):
    w_in = p['w_in'][l]
    W = N_HEADS * HEAD_DIM
    KW = GQA_KV_HEADS * HEAD_DIM
    qr = p['mla_w_uq'].shape[1]
    kvr = p['mla_w_ukv'].shape[1]
    o = 0
    seg = {}
    for name, width in (('a', 3 * W), ('cq', qr), ('ckv', kvr), ('kr', MLA_ROPE), ('gq', W), ('gk', KW), ('gv', KW),
                        ('gate', N_BRANCH * D)):
        seg[name] = w_in[:, o:o + width]
        o += width
    gq_cols = np.concatenate([np.arange(h * HEAD_DIM, (h + 1) * HEAD_DIM) for h in GQA_HEAD_ORDER])
    zeros = lambda n: jnp.zeros((D, n), w_in.dtype)
    gv = [p_ for g in range(GQA_KV_HEADS)
          for p_ in (seg['gv'][:, g * HEAD_DIM:(g + 1) * HEAD_DIM], zeros(LANE - HEAD_DIM))]
    w_mix = jnp.concatenate(
        [seg['a'], seg['cq'], seg['gq'][:, gq_cols], seg['ckv'],
         zeros(MLA_NOPE), seg['kr'], zeros(LANE - MLA_NOPE - MLA_ROPE), seg['gk']] + gv, axis=1).astype(BF16)

    w_uq = p['mla_w_uq'][l].reshape(qr, N_HEADS, MLA_NOPE + MLA_ROPE)
    wq = jnp.pad(w_uq, ((0, 0), (0, 0), (0, LANE - MLA_NOPE - MLA_ROPE))).reshape(qr, N_HEADS * LANE).astype(BF16)
    w_ukv = p['mla_w_ukv'][l].reshape(kvr, N_HEADS, MLA_NOPE + MLA_V)
    wkn = jnp.pad(w_ukv[:, :, :MLA_NOPE], ((0, 0), (0, 0), (0, LANE - MLA_NOPE))).reshape(kvr, N_HEADS * LANE)
    wv = jnp.pad(w_ukv[:, :, MLA_NOPE:], ((0, 0), (0, 0), (0, LANE - MLA_V))).reshape(kvr, N_HEADS * LANE)

    w_branch = p['w_branch'][l]
    w_branch = jnp.stack([w_branch[0], w_branch[1], w_branch[2][gq_cols]]).astype(BF16)
    return dict(
        w_mix=w_mix, w_gate_logits=seg['gate'].astype(BF16),
        wq=wq, wkn=wkn.astype(BF16), wv=wv.astype(BF16),
        w_branch=w_branch, w_out=p['w_out'][l].astype(BF16),
        w_router_t=p['w_router'][l].T.astype(BF16),
        w_exp_gate=p['w_exp_gate'][l].astype(BF16), w_exp_up=p['w_exp_up'][l].astype(BF16),
        w_exp_down=p['w_exp_down'][l].astype(BF16),
    )


def _trunk(x3, mod, p, layers, tables):
    B, T, D = x3.shape
    N = B * T
    x = x3.reshape(N, D)
    depth = len(layers)
    for l in range(depth):
        w = layers[l]
        sh1, sc1, gt1, sh2, sc2, gt2 = [mod[l][:, None, i * D:(i + 1) * D] for i in range(6)]
        mix = in_projection(x, p['g_mix'][l], sc1, sh1, w['w_mix'], T, tn=w['w_mix'].shape[1] // 2,
                            sigmoid_out=False, name="mix_projection")
        sgate = in_projection(x, p['g_mix'][l], sc1, sh1, w['w_gate_logits'], T, tn=_tile(N_BRANCH * D, 768),
                              sigmoid_out=True, name="gate_projection")
        o_a = neighborhood_attention(mix, tables['na_bias'][l], B, T)
        qm, km, vm = mla_prepare(mix, p['mla_g_q'][l], p['mla_g_kv'][l], w['wq'], w['wkn'], w['wv'],
                                 tables['mla_q'][T], tables['mla_k'][T], T)
        o_b = flash_attention(qm, km, vm, B, T, shared_k=False, name="mla_attention")
        qg, kg, vg = gqa_prepare(mix, p['gqa_g_q'][l], p['gqa_g_k'][l], tables['gqa_q'][T], tables['gqa_k'][T], T)
        o_c = flash_attention(qg, kg, vg, B, T, shared_k=True, name="gqa_attention")
        x = merge_branches(o_a, o_b, o_c, sgate, x, gt1, w['w_branch'], w['w_out'], T)

        h, aff = router(x, p['g_ffn'][l], sc2, sh2, w['w_router_t'], T)
        cap = EC_CAPACITY * N // N_EXPERTS
        g, idx = lax.top_k(aff, cap)
        xe = h[idx]
        ye = expert_ffn(xe, g[..., None], w['w_exp_gate'], w['w_exp_up'], w['w_exp_down'])
        y = jnp.zeros((N, D), F32).at[idx.reshape(-1)].add(ye.reshape(-1, D))
        x = residual(x, y, gt2, T, g_final=p['g_final'] if l == depth - 1 else None)
    return x.reshape(B, T, D)


def kernel(x_prompt, x_sample, c_prompt, c_sample, w_ada, b_ada, g_mix, g_ffn, w_in, na_rel_bias, mla_g_q, mla_w_uq,
           mla_g_kv, mla_w_ukv, gqa_g_q, gqa_g_k, w_branch, w_out, w_router, w_exp_gate, w_exp_up, w_exp_down,
           g_final):
    p = dict(g_mix=g_mix, g_ffn=g_ffn, w_in=w_in, mla_g_q=mla_g_q, mla_w_uq=mla_w_uq, mla_g_kv=mla_g_kv,
             mla_w_ukv=mla_w_ukv, gqa_g_q=gqa_g_q, gqa_g_k=gqa_g_k, w_branch=w_branch, w_out=w_out,
             w_router=w_router, w_exp_gate=w_exp_gate, w_exp_up=w_exp_up, w_exp_down=w_exp_down, g_final=g_final)
    depth, D = g_mix.shape
    bp, bs = c_prompt.shape[0], c_sample.shape[0]
    rows = -(-(bp + bs) // 8) * 8
    c_all = jnp.concatenate([c_prompt, c_sample, jnp.zeros((rows - bp - bs, D), F32)], axis=0)
    mod = ada_modulation(c_all, w_ada, b_ada)
    layers = [_prepare_layer(l, p, D) for l in range(depth)]
    tables = dict(na_bias=[_na_bias_table(na_rel_bias[l]) for l in range(depth)],
                  mla_q={}, mla_k={}, gqa_q={}, gqa_k={})
    for T in {x_prompt.shape[1], x_sample.shape[1]}:
        tables['mla_q'][T], tables['mla_k'][T] = _mla_tables(T)
        tables['gqa_q'][T], tables['gqa_k'][T] = _gqa_tables(T)
    y_prompt = _trunk(x_prompt, mod[:, :bp], p, layers, tables)
    y_sample = _trunk(x_sample, mod[:, bp:bp + bs], p, layers, tables)
    return (y_prompt, y_sample)
```

```python
import functools
import math

import numpy as np
import jax
import jax.numpy as jnp
from jax import lax
from jax.experimental import pallas as pl
from jax.experimental.pallas import tpu as pltpu

F32 = jnp.float32
BF16 = jnp.bfloat16

GRID_W = 64
HEAD_DIM = 64
N_HEADS = 8
NA_WIN_H = 8
NA_WIN_W = 16
MLA_NOPE = 64
MLA_ROPE = 32
MLA_V = 64
GQA_KV_HEADS = 2
N_BRANCH = 3
N_EXPERTS = 16
EC_CAPACITY = 2
ROPE_THETA = 10000.0
EPS = 1e-6
LOG2E = 1.4426950408889634

LANE = 128
NA_ROWS_PER_BLOCK = 4
NA_BLOCK = NA_ROWS_PER_BLOCK * GRID_W
NA_KEY_BLOCKS = 3
NEG = -1e30
VMEM_LIMIT = 56 * 1024 * 1024

GQA_HEAD_ORDER = (0, 4, 1, 5, 2, 6, 3, 7)


def _params(sem):
    return pltpu.CompilerParams(dimension_semantics=sem, vmem_limit_bytes=VMEM_LIMIT)


def _tile(n, pref):
    t = min(n, pref)
    while n % t:
        t //= 2
    return t


def _ada_kernel(c_ref, w_ref, b_ref, o_ref):
    c = c_ref[...]
    a = (c * jax.nn.sigmoid(c)).astype(BF16)
    o_ref[...] = jnp.dot(a, w_ref[...].astype(BF16), preferred_element_type=F32) + b_ref[...]


def ada_modulation(c, w_ada, b_ada):
    L, D, D6 = w_ada.shape
    R = c.shape[0]
    tn = _tile(D6, 1024)
    return pl.pallas_call(
        _ada_kernel,
        grid=(L, D6 // tn),
        in_specs=[
            pl.BlockSpec((R, D), lambda l, j: (0, 0)),
            pl.BlockSpec((None, D, tn), lambda l, j: (l, 0, j)),
            pl.BlockSpec((None, 1, tn), lambda l, j: (l, 0, j)),
        ],
        out_specs=pl.BlockSpec((None, R, tn), lambda l, j: (l, 0, j)),
        out_shape=jax.ShapeDtypeStruct((L, R, D6), F32),
        compiler_params=_params(("parallel", "parallel")),
        name="ada_modulation",
    )(c, w_ada, b_ada.reshape(L, 1, D6))


def _norm_mod(x, g, sc, sh):
    ms = jnp.mean(x * x, axis=-1, keepdims=True)
    y = x * lax.rsqrt(ms + EPS)
    return (y * g) * (1.0 + sc) + sh


def _in_proj_kernel(x_ref, g_ref, sc_ref, sh_ref, w_ref, o_ref, h_sc, *, sigmoid_out):
    @pl.when(pl.program_id(1) == 0)
    def _():
        h_sc[...] = _norm_mod(x_ref[...], g_ref[...], sc_ref[...], sh_ref[...]).astype(BF16)

    acc = jnp.dot(h_sc[...], w_ref[...], preferred_element_type=F32)
    if sigmoid_out:
        acc = jax.nn.sigmoid(acc)
    o_ref[...] = acc.astype(o_ref.dtype)


def in_projection(x, g, sc, sh, w, seq, *, tn, sigmoid_out, name):
    N, D = x.shape
    C = w.shape[1]
    tm = _tile(seq, 1024)
    per_seq = seq // tm
    return pl.pallas_call(
        functools.partial(_in_proj_kernel, sigmoid_out=sigmoid_out),
        grid=(N // tm, C // tn),
        in_specs=[
            pl.BlockSpec((tm, D), lambda i, j: (i, 0)),
            pl.BlockSpec((1, D), lambda i, j: (0, 0)),
            pl.BlockSpec((None, 1, D), lambda i, j: (i // per_seq, 0, 0)),
            pl.BlockSpec((None, 1, D), lambda i, j: (i // per_seq, 0, 0)),
            pl.BlockSpec((D, tn), lambda i, j: (0, j)),
        ],
        out_specs=pl.BlockSpec((tm, tn), lambda i, j: (i, j)),
        out_shape=jax.ShapeDtypeStruct((N, C), BF16),
        scratch_shapes=[pltpu.VMEM((tm, D), BF16)],
        compiler_params=_params(("parallel", "arbitrary")),
        name=name,
    )(x, g.reshape(1, D), sc, sh, w)


def _na_bias_table(rel_bias):
    R = NA_ROWS_PER_BLOCK
    KR = NA_KEY_BLOCKS * R
    q_rows = np.stack([np.arange(R), R + np.arange(R), 2 * R + np.arange(R)])
    w_start = np.stack([np.zeros(R, np.int64), np.arange(R), np.full(R, KR - NA_WIN_H)])
    col = np.arange(GRID_W)
    col_start = np.clip(col - NA_WIN_W // 2, 0, GRID_W - NA_WIN_W)
    k_row = np.arange(KR)
    k_col = np.arange(GRID_W)
    row_ok = (k_row[None, None, :] >= w_start[:, :, None]) & (k_row[None, None, :] < w_start[:, :, None] + NA_WIN_H)
    col_ok = (k_col[None, :] >= col_start[:, None]) & (k_col[None, :] < col_start[:, None] + NA_WIN_W)
    dr = np.clip(k_row[None, None, :] - q_rows[:, :, None] + (NA_WIN_H - 1), 0, 2 * NA_WIN_H - 2)
    dc = np.clip(k_col[None, :] - col[:, None] + (NA_WIN_W - 1), 0, 2 * NA_WIN_W - 2)
    valid = row_ok[:, :, None, :, None] & col_ok[None, None, :, None, :]
    dr_b = np.broadcast_to(dr[:, :, None, :, None], valid.shape)
    dc_b = np.broadcast_to(dc[None, None, :, None, :], valid.shape)
    bias = rel_bias.astype(F32)[:, dr_b, dc_b]
    bias = jnp.where(valid[None], bias, NEG)
    H = rel_bias.shape[0]
    return jnp.transpose(bias, (1, 0, 2, 3, 4, 5)).reshape(3, H, R * GRID_W, KR * GRID_W)


def _na_kernel(q_ref, k0_ref, k1_ref, k2_ref, v0_ref, v1_ref, v2_ref, bias_ref, o_ref):
    lane = lax.broadcasted_iota(jnp.int32, (NA_BLOCK, LANE), 1)
    lo = lane < HEAD_DIM
    scale = HEAD_DIM ** -0.5
    for u in range(N_HEADS // 2):
        cs = slice(u * LANE, (u + 1) * LANE)
        q = q_ref[:, cs] * scale
        ks = (k0_ref[:, cs], k1_ref[:, cs], k2_ref[:, cs])
        vs = (v0_ref[:, cs], v1_ref[:, cs], v2_ref[:, cs])
        outs = []
        for hh in range(2):
            qh = jnp.where(lo if hh == 0 else ~lo, q, jnp.zeros_like(q))
            s = jnp.concatenate(
                [lax.dot_general(qh, k, (((1,), (1,)), ((), ())), preferred_element_type=F32) for k in ks], axis=1)
            s = s + bias_ref[2 * u + hh]
            m = jnp.max(s, axis=1, keepdims=True)
            p = jnp.exp(s - m)
            l = jnp.sum(p, axis=1, keepdims=True)
            pb = p.astype(BF16)
            o = None
            for j in range(NA_KEY_BLOCKS):
                t = jnp.dot(pb[:, j * NA_BLOCK:(j + 1) * NA_BLOCK], vs[j], preferred_element_type=F32)
                o = t if o is None else o + t
            outs.append(o / l)
        o_ref[:, cs] = jnp.where(lo, outs[0], outs[1]).astype(o_ref.dtype)


def neighborhood_attention(mix, bias_table, batch, seq):
    N = mix.shape[0]
    W = N_HEADS * HEAD_DIM
    nb = seq // NA_BLOCK
    assert seq % NA_BLOCK == 0 and nb >= NA_KEY_BLOCKS

    def kv_spec(colblk, j):
        return pl.BlockSpec(
            (NA_BLOCK, W),
            lambda b, i: (b * nb + jnp.clip(i - 1, 0, nb - NA_KEY_BLOCKS) + j, colblk))

    def bias_idx(b, i):
        return (jnp.where(i == 0, 0, jnp.where(i == nb - 1, 2, 1)), 0, 0, 0)

    return pl.pallas_call(
        _na_kernel,
        grid=(batch, nb),
        in_specs=[pl.BlockSpec((NA_BLOCK, W), lambda b, i: (b * nb + i, 0))]
        + [kv_spec(1, j) for j in range(NA_KEY_BLOCKS)]
        + [kv_spec(2, j) for j in range(NA_KEY_BLOCKS)]
        + [pl.BlockSpec((None, N_HEADS, NA_BLOCK, NA_KEY_BLOCKS * NA_BLOCK), bias_idx)],
        out_specs=pl.BlockSpec((NA_BLOCK, W), lambda b, i: (b * nb + i, 0)),
        out_shape=jax.ShapeDtypeStruct((N, W), BF16),
        compiler_params=_params(("parallel", "arbitrary")),
        name="neighborhood_attention",
    )(mix, mix, mix, mix, mix, mix, mix, bias_table)


def _rope_angles(pos, dim):
    inv_freq = ROPE_THETA ** (-jnp.arange(0, dim, 2, dtype=F32) / dim)
    ang = pos.astype(F32)[:, None] * inv_freq[None, :]
    return jnp.concatenate([ang, ang], axis=-1)


def _rot_tables(cos, sin, scale):
    n = cos.shape[1]
    first = (np.arange(n) % 32) < 16
    sa = jnp.where(first[None, :], -sin, 0.0)
    sb = jnp.where(first[None, :], 0.0, sin)
    return cos * scale, sa * scale, sb * scale


def _mla_tables(seq):
    pos = jnp.arange(seq)
    ang = _rope_angles(pos, MLA_ROPE)
    cos, sin = jnp.cos(ang), jnp.sin(ang)
    pad_hi = LANE - MLA_NOPE - MLA_ROPE

    def place(t, nope_val):
        return jnp.concatenate(
            [jnp.full((seq, MLA_NOPE), nope_val, F32), t, jnp.zeros((seq, pad_hi), F32)], axis=1)

    qs = (MLA_NOPE + MLA_ROPE) ** -0.5 * LOG2E
    qc, qa, qb = _rot_tables(cos, sin, qs)
    kc, ka, kb = _rot_tables(cos, sin, 1.0)
    q_tab = jnp.stack([place(qc, qs), place(qa, 0.0), place(qb, 0.0)])
    k_tab = jnp.stack([place(kc, 0.0), place(ka, 0.0), place(kb, 0.0)])
    return q_tab, k_tab


def _gqa_tables(seq):
    pos = jnp.arange(seq)
    half = HEAD_DIM // 2
    ar = _rope_angles(pos // GRID_W, half)
    ac = _rope_angles(pos % GRID_W, half)
    cos = jnp.concatenate([jnp.cos(ar), jnp.cos(ac)] * 2, axis=1)
    sin = jnp.concatenate([jnp.sin(ar), jnp.sin(ac)] * 2, axis=1)
    qs = HEAD_DIM ** -0.5 * LOG2E
    return jnp.stack(_rot_tables(cos, sin, qs)), jnp.stack(_rot_tables(cos, sin, 1.0))


def _apply_rot(x, tab_ref):
    n = x.shape[1]
    return (x * tab_ref[0] + pltpu.roll(x, n - 16, 1) * tab_ref[1] + pltpu.roll(x, 16, 1) * tab_ref[2])


def _rms(x, g):
    ms = jnp.mean(x * x, axis=-1, keepdims=True)
    return x * lax.rsqrt(ms + EPS) * g


def _mla_prep_kernel(cq_ref, ckv_ref, kr_ref, gq_ref, gkv_ref, wq_ref, wkn_ref, wv_ref, qtab_ref, ktab_ref,
                     q_out, k_out, v_out):
    cqn = _rms(cq_ref[...].astype(F32), gq_ref[...]).astype(BF16)
    ckvn = _rms(ckv_ref[...].astype(F32), gkv_ref[...]).astype(BF16)
    qf = jnp.dot(cqn, wq_ref[...], preferred_element_type=F32)
    kn = jnp.dot(ckvn, wkn_ref[...], preferred_element_type=F32)
    tm = cq_ref.shape[0]
    hi_lane = lax.broadcasted_iota(jnp.int32, (tm, LANE), 1) >= MLA_V
    vf = jnp.dot(ckvn, wv_ref[...], preferred_element_type=F32)
    k_rope = _apply_rot(kr_ref[...].astype(F32), ktab_ref)
    for h in range(N_HEADS):
        cs = slice(h * LANE, (h + 1) * LANE)
        q_out[:, cs] = _apply_rot(qf[:, cs], qtab_ref).astype(q_out.dtype)
        k_out[:, cs] = (kn[:, cs] + k_rope).astype(k_out.dtype)
        v_out[:, cs] = jnp.where(hi_lane, 1.0, vf[:, cs]).astype(v_out.dtype)


def mla_prepare(mix, g_q, g_kv, wq, wkn, wv, q_tab, k_tab, seq):
    N = mix.shape[0]
    tm = _tile(seq, 512)
    per_seq = seq // tm
    qr, kvr = wq.shape[0], wkn.shape[0]
    cq_blk = 1536 // qr
    ckv_blk = 2560 // kvr
    kr_blk = 2816 // LANE
    const = lambda i: (0, 0)
    tab = pl.BlockSpec((3, tm, LANE), lambda i: (0, i % per_seq, 0))
    return pl.pallas_call(
        _mla_prep_kernel,
        grid=(N // tm,),
        in_specs=[
            pl.BlockSpec((tm, qr), lambda i: (i, cq_blk)),
            pl.BlockSpec((tm, kvr), lambda i: (i, ckv_blk)),
            pl.BlockSpec((tm, LANE), lambda i: (i, kr_blk)),
            pl.BlockSpec((1, qr), const),
            pl.BlockSpec((1, kvr), const),
            pl.BlockSpec(wq.shape, const),
            pl.BlockSpec(wkn.shape, const),
            pl.BlockSpec(wv.shape, const),
            tab, tab,
        ],
        out_specs=[
            pl.BlockSpec((tm, N_HEADS * LANE), lambda i: (i, 0)),
            pl.BlockSpec((tm, N_HEADS * LANE), lambda i: (i, 0)),
            pl.BlockSpec((tm, N_HEADS * LANE), lambda i: (i, 0)),
        ],
        out_shape=[jax.ShapeDtypeStruct((N, N_HEADS * LANE), BF16)] * 3,
        compiler_params=_params(("parallel",)),
        name="mla_prepare",
    )(mix, mix, mix, g_q.reshape(1, qr), g_kv.reshape(1, kvr), wq, wkn, wv, q_tab, k_tab)


def _head_rms(x, g, seg_ref):
    sq = x * x
    hi = sq.astype(BF16)
    lo = (sq - hi.astype(F32)).astype(BF16)
    ms = (jnp.dot(hi, seg_ref[...], preferred_element_type=F32)
          + jnp.dot(lo, seg_ref[...], preferred_element_type=F32)) * (1.0 / HEAD_DIM)
    return x * lax.rsqrt(ms + EPS) * g


def _gqa_prep_kernel(q_ref, k_ref, v_ref, gq_ref, gk_ref, segq_ref, segk_ref, qtab_ref, ktab_ref,
                     q_out, k_out, v_out):
    tm = q_ref.shape[0]
    lane = lax.broadcasted_iota(jnp.int32, (tm, LANE), 1)
    lo = lane < HEAD_DIM
    for g in range(GQA_KV_HEADS):
        cs = slice(g * LANE, (g + 1) * LANE)
        v_out[:, cs] = jnp.where(lo, v_ref[:, cs], jnp.ones((tm, LANE), v_ref.dtype))
    qn = _head_rms(q_ref[...].astype(F32), gq_ref[...], segq_ref)
    for u in range(N_HEADS // 2):
        y = _apply_rot(qn[:, u * LANE:(u + 1) * LANE], qtab_ref)
        q_out[:, (2 * u) * LANE:(2 * u + 1) * LANE] = jnp.where(lo, y, 0.0).astype(q_out.dtype)
        q_out[:, (2 * u + 1) * LANE:(2 * u + 2) * LANE] = jnp.where(lo, 0.0, y).astype(q_out.dtype)
    kn = _head_rms(k_ref[...].astype(F32), gk_ref[...], segk_ref)
    k_out[...] = _apply_rot(kn, ktab_ref).astype(k_out.dtype)


def gqa_prepare(mix, g_q, g_k, q_tab, k_tab, seq):
    N = mix.shape[0]
    tm = _tile(seq, 512)
    per_seq = seq // tm
    W = N_HEADS * HEAD_DIM
    KW = GQA_KV_HEADS * HEAD_DIM
    VW = GQA_KV_HEADS * LANE
    const = lambda i: (0, 0)
    seg = lambda n: jnp.asarray(np.kron(np.eye(n // HEAD_DIM), np.ones((HEAD_DIM, HEAD_DIM))), BF16)
    tab = pl.BlockSpec((3, tm, LANE), lambda i: (0, i % per_seq, 0))
    return pl.pallas_call(
        _gqa_prep_kernel,
        grid=(N // tm,),
        in_specs=[
            pl.BlockSpec((tm, W), lambda i: (i, 2048 // W)),
            pl.BlockSpec((tm, KW), lambda i: (i, 2944 // KW)),
            pl.BlockSpec((tm, VW), lambda i: (i, 3072 // VW)),
            pl.BlockSpec((1, W), const),
            pl.BlockSpec((1, KW), const),
            pl.BlockSpec((W, W), const),
            pl.BlockSpec((KW, KW), const),
            tab, tab,
        ],
        out_specs=[
            pl.BlockSpec((tm, N_HEADS * LANE), lambda i: (i, 0)),
            pl.BlockSpec((tm, KW), lambda i: (i, 0)),
            pl.BlockSpec((tm, VW), lambda i: (i, 0)),
        ],
        out_shape=[
            jax.ShapeDtypeStruct((N, N_HEADS * LANE), BF16),
            jax.ShapeDtypeStruct((N, KW), BF16),
            jax.ShapeDtypeStruct((N, VW), BF16),
        ],
        compiler_params=_params(("parallel",)),
        name="gqa_prepare",
    )(mix, mix, mix, jnp.tile(g_q, N_HEADS).reshape(1, W), jnp.tile(g_k, GQA_KV_HEADS).reshape(1, KW),
      seg(W), seg(KW), q_tab, k_tab)


def _flash_kernel(q_ref, k_ref, v_ref, o_ref, m_sc, acc_sc, *, shared_k, v_groups):
    kv = pl.program_id(2)
    tq = q_ref.shape[0]
    tk = k_ref.shape[0]

    @pl.when(kv == 0)
    def _():
        m_sc[...] = jnp.full(m_sc.shape, -jnp.inf, F32)
        acc_sc[...] = jnp.zeros(acc_sc.shape, F32)

    for h in range(N_HEADS):
        q = q_ref[:, h * LANE:(h + 1) * LANE]
        k = k_ref[...] if shared_k else k_ref[:, h * LANE:(h + 1) * LANE]
        g = h % v_groups
        v = v_ref[:, g * LANE:(g + 1) * LANE]
        s = lax.dot_general(q, k, (((1,), (1,)), ((), ())), preferred_element_type=F32)
        m_prev = m_sc[h]
        m_new = jnp.maximum(m_prev, jnp.max(s, axis=1, keepdims=True))
        alpha = jnp.exp2(m_prev - m_new)
        p = jnp.exp2(s - jnp.concatenate([m_new] * (tk // LANE), axis=1))
        acc_sc[h] = alpha * acc_sc[h] + jnp.dot(p.astype(BF16), v, preferred_element_type=F32)
        m_sc[h] = m_new

    @pl.when(kv == pl.num_programs(2) - 1)
    def _():
        lane = lax.broadcasted_iota(jnp.int32, (tq, LANE), 1)
        lo = lane < HEAD_DIM
        for u in range(N_HEADS // 2):
            a = acc_sc[2 * u]
            b = acc_sc[2 * u + 1]
            a = a / pltpu.roll(a, HEAD_DIM, 1)
            b = pltpu.roll(b, HEAD_DIM, 1) / b
            o_ref[:, u * LANE:(u + 1) * LANE] = jnp.where(lo, a, b).astype(o_ref.dtype)


def flash_attention(q, k, v, batch, seq, *, shared_k, name):
    N = q.shape[0]
    tq = _tile(seq, 1024)
    tk = _tile(seq, 1024)
    nq, nk = seq // tq, seq // tk
    kw, vw = k.shape[1], v.shape[1]
    ow = N_HEADS * HEAD_DIM
    return pl.pallas_call(
        functools.partial(_flash_kernel, shared_k=shared_k, v_groups=vw // LANE),
        grid=(batch, nq, nk),
        in_specs=[
            pl.BlockSpec((tq, N_HEADS * LANE), lambda b, i, j: (b * nq + i, 0)),
            pl.BlockSpec((tk, kw), lambda b, i, j: (b * nk + j, 0)),
            pl.BlockSpec((tk, vw), lambda b, i, j: (b * nk + j, 0)),
        ],
        out_specs=pl.BlockSpec((tq, ow), lambda b, i, j: (b * nq + i, 0)),
        out_shape=jax.ShapeDtypeStruct((N, ow), BF16),
        scratch_shapes=[pltpu.VMEM((N_HEADS, tq, LANE), F32)] * 2,
        compiler_params=_params(("parallel", "parallel", "arbitrary")),
        name=name,
    )(q, k, v)


def _merge_kernel(oa_ref, ob_ref, oc_ref, sg_ref, x_ref, gt_ref, wb_ref, wo_ref, o_ref):
    D = x_ref.shape[1]
    merged = None
    for i, o_r in enumerate((oa_ref, ob_ref, oc_ref)):
        t = jnp.dot(o_r[...], wb_ref[i], preferred_element_type=F32)
        t = sg_ref[:, i * D:(i + 1) * D].astype(F32) * t
        merged = t if merged is None else merged + t
    mix = jnp.dot(merged.astype(BF16), wo_ref[...], preferred_element_type=F32)
    o_ref[...] = x_ref[...] + gt_ref[...] * mix


def merge_branches(o_a, o_b, o_c, sgate, x, gt, w_branch, w_out, seq):
    N, D = x.shape
    W = o_a.shape[1]
    tm = _tile(seq, 256)
    per_seq = seq // tm
    row = lambda i: (i, 0)
    return pl.pallas_call(
        _merge_kernel,
        grid=(N // tm,),
        in_specs=[
            pl.BlockSpec((tm, W), row), pl.BlockSpec((tm, W), row), pl.BlockSpec((tm, W), row),
            pl.BlockSpec((tm, N_BRANCH * D), row),
            pl.BlockSpec((tm, D), row),
            pl.BlockSpec((None, 1, D), lambda i: (i // per_seq, 0, 0)),
            pl.BlockSpec((N_BRANCH, W, D), lambda i: (0, 0, 0)),
            pl.BlockSpec((D, D), lambda i: (0, 0)),
        ],
        out_specs=pl.BlockSpec((tm, D), row),
        out_shape=jax.ShapeDtypeStruct((N, D), F32),
        compiler_params=_params(("parallel",)),
        name="merge_branches",
    )(o_a, o_b, o_c, sgate, x, gt, w_branch, w_out)


def _router_kernel(x_ref, g_ref, sc_ref, sh_ref, wr_ref, h_out, aff_out):
    h = _norm_mod(x_ref[...], g_ref[...], sc_ref[...], sh_ref[...]).astype(BF16)
    h_out[...] = h
    logits = lax.dot_general(wr_ref[...], h, (((1,), (1,)), ((), ())), preferred_element_type=F32)
    m = jnp.max(logits, axis=0, keepdims=True)
    p = jnp.exp(logits - m)
    aff_out[...] = p / jnp.sum(p, axis=0, keepdims=True)


def router(x, g, sc, sh, w_router_t, seq):
    N, D = x.shape
    E = w_router_t.shape[0]
    tm = _tile(seq, 512)
    per_seq = seq // tm
    return pl.pallas_call(
        _router_kernel,
        grid=(N // tm,),
        in_specs=[
            pl.BlockSpec((tm, D), lambda i: (i, 0)),
            pl.BlockSpec((1, D), lambda i: (0, 0)),
            pl.BlockSpec((None, 1, D), lambda i: (i // per_seq, 0, 0)),
            pl.BlockSpec((None, 1, D), lambda i: (i // per_seq, 0, 0)),
            pl.BlockSpec((E, D), lambda i: (0, 0)),
        ],
        out_specs=[pl.BlockSpec((tm, D), lambda i: (i, 0)), pl.BlockSpec((E, tm), lambda i: (0, i))],
        out_shape=[jax.ShapeDtypeStruct((N, D), BF16), jax.ShapeDtypeStruct((E, N), F32)],
        compiler_params=_params(("parallel",)),
        name="router",
    )(x, g.reshape(1, D), sc, sh, w_router_t)


MOE_WIN = 512
MOE_ALIGN = 16
MOE_CHUNK = 128
MOE_TILE = 512
_SEG_PIECES = (512, 256, 128, 64, 32, 16)


def _round_up(x, m):
    return (x + (m - 1)) // m * m


def _select_kernel(aff_ref, slot_ref, wbase_ref, wcnt_ref, *, cap, chunks_per_win):
    a = aff_ref[...]
    nc = a.shape[0]
    bits = pltpu.bitcast(a, jnp.int32)

    def search(i, t):
        cand = t | jnp.left_shift(jnp.int32(1), 30 - i)
        n_ge = jnp.sum((bits >= cand).astype(jnp.int32))
        return jnp.where(n_ge >= cap, cand, t)

    thr = lax.fori_loop(0, 31, search, jnp.int32(0))
    gt = bits > thr
    eq = bits == thr
    need = (cap - jnp.sum(gt.astype(jnp.int32))).astype(F32)

    li = lax.broadcasted_iota(jnp.int32, (LANE, LANE), 0)
    lj = lax.broadcasted_iota(jnp.int32, (LANE, LANE), 1)
    tri = jnp.where(li <= lj, 1.0, 0.0).astype(BF16)
    ones = jnp.ones((LANE, LANE), BF16)
    ci = lax.broadcasted_iota(jnp.int32, (nc, nc), 0)
    cj = lax.broadcasted_iota(jnp.int32, (nc, nc), 1)
    shift = chunks_per_win.bit_length() - 1
    wi, wj = jnp.right_shift(ci, shift), jnp.right_shift(cj, shift)
    as_bf = lambda m: jnp.where(m, 1.0, 0.0).astype(BF16)
    dot = lambda x, y: jnp.dot(x, y, preferred_element_type=F32)

    eq_b = as_bf(eq)
    eq_rank = dot(as_bf(cj < ci), dot(eq_b, ones).astype(BF16)) + dot(eq_b, tri) - eq_b.astype(F32)
    sel = gt | (eq & (eq_rank < need))
    sel_b = as_bf(sel)
    incl = dot(sel_b, tri)
    tot = dot(sel_b, ones).astype(BF16)
    in_win = dot(as_bf((cj < ci) & (wj == wi)), tot)
    wcnt = dot(as_bf(wj == wi), tot)
    wpad = jnp.ceil(wcnt * (1.0 / MOE_ALIGN)) * MOE_ALIGN
    first = (cj & (chunks_per_win - 1)) == 0
    wbase = dot(as_bf((wj < wi) & first), wpad.astype(BF16))
    slot = wbase + in_win + incl - sel_b.astype(F32)
    slot_ref[...] = jnp.where(sel, slot, -1.0).astype(jnp.int32)
    wbase_ref[...] = wbase.astype(jnp.int32)
    wcnt_ref[...] = wcnt.astype(jnp.int32)


def select_tokens(aff, cap):
    E, N = aff.shape
    nc = N // LANE
    cpw = MOE_WIN // LANE
    blk = pl.BlockSpec((None, nc, LANE), lambda e: (e, 0, 0))
    out = jax.ShapeDtypeStruct((E, nc, LANE), jnp.int32)
    slot, wbase, wcnt = pl.pallas_call(
        functools.partial(_select_kernel, cap=cap, chunks_per_win=cpw),
        grid=(E,),
        in_specs=[blk],
        out_specs=[blk, blk, blk],
        out_shape=[out, out, out],
        compiler_params=_params(("parallel",)),
        name="select_tokens",
    )(aff.reshape(E, nc, LANE))
    return slot.reshape(E, N), wbase[:, ::cpw, 0], wcnt[:, ::cpw, 0]


def _for_segment_pieces(n_rows, make_copy, act):
    off = jnp.int32(0)
    for piece in _SEG_PIECES:
        has = (n_rows & piece) != 0

        @pl.when(has)
        def _(off=off, piece=piece):
            act(make_copy(pl.multiple_of(off, MOE_ALIGN), piece))

        off = off + jnp.where(has, piece, 0)


def _dispatch_kernel(base_ref, cnt_ref, h_ref, slot_ref, xe_hbm, stage, zero_sc, sems, zsem, *, nw, ne):
    w, e = pl.program_id(0), pl.program_id(1)
    step = w * ne + e
    cur = step % 2
    win = h_ref.shape[0]

    def seg_rows(ww, ee):
        return _round_up(cnt_ref[ee * nw + ww], MOE_ALIGN)

    def seg_copy(buf, ww, ee):
        b = base_ref[ee * nw + ww]

        def make(off, piece):
            return pltpu.make_async_copy(stage.at[buf, pl.ds(off, piece)],
                                         xe_hbm.at[ee, pl.ds(pl.multiple_of(b + off, MOE_ALIGN), piece)],
                                         sems.at[buf])
        return make

    @pl.when(step == 0)
    def _():
        zero_sc[...] = jnp.zeros(zero_sc.shape, zero_sc.dtype)

    @pl.when(step >= 2)
    def _():
        ps = step - 2
        _for_segment_pieces(seg_rows(ps // ne, ps % ne), seg_copy(cur, ps // ne, ps % ne), lambda c: c.wait())

    b = base_ref[e * nw + w]
    rows = seg_rows(w, e)
    srow = slot_ref[...]
    h = h_ref[...]

    def chunk(c, carry):
        r0 = pl.multiple_of(c * MOE_CHUNK, MOE_CHUNK)
        want = b + r0 + lax.broadcasted_iota(jnp.int32, (MOE_CHUNK, win), 0)
        onehot = jnp.where(srow == want, 1.0, 0.0).astype(BF16)
        stage[cur, pl.ds(r0, MOE_CHUNK), :] = jnp.dot(onehot, h, preferred_element_type=F32).astype(stage.dtype)
        return carry

    lax.fori_loop(0, (rows + MOE_CHUNK - 1) // MOE_CHUNK, chunk, 0)
    _for_segment_pieces(rows, seg_copy(cur, w, e), lambda c: c.start())

    @pl.when(w == nw - 1)
    def _():
        used = b + rows
        fill = _round_up(used, MOE_TILE) - used

        def zmake(off, piece):
            return pltpu.make_async_copy(zero_sc.at[pl.ds(0, piece)],
                                         xe_hbm.at[e, pl.ds(pl.multiple_of(used + off, MOE_ALIGN), piece)], zsem)
        _for_segment_pieces(fill, zmake, lambda c: c.start())
        _for_segment_pieces(fill, zmake, lambda c: c.wait())

    @pl.when(step == nw * ne - 1)
    def _():
        _for_segment_pieces(rows, seg_copy(cur, w, e), lambda c: c.wait())
        ps = step - 1
        _for_segment_pieces(seg_rows(ps // ne, ps % ne), seg_copy(1 - cur, ps // ne, ps % ne), lambda c: c.wait())


def dispatch_tokens(h, slot, base, cnt, rows_per_expert):
    N, D = h.shape
    E = slot.shape[0]
    nw = N // MOE_WIN
    assert nw * E >= 2 and MOE_WIN <= _SEG_PIECES[0]
    return pl.pallas_call(
        functools.partial(_dispatch_kernel, nw=nw, ne=E),
        grid_spec=pltpu.PrefetchScalarGridSpec(
            num_scalar_prefetch=2,
            grid=(nw, E),
            in_specs=[
                pl.BlockSpec((MOE_WIN, D), lambda w, e, *_: (w, 0)),
                pl.BlockSpec((None, 1, MOE_WIN), lambda w, e, *_: (e, 0, w)),
            ],
            out_specs=pl.BlockSpec(memory_space=pl.ANY),
            scratch_shapes=[
                pltpu.VMEM((2, MOE_WIN, D), BF16),
                pltpu.VMEM((MOE_TILE, D), BF16),
                pltpu.SemaphoreType.DMA((2,)),
                pltpu.SemaphoreType.DMA,
            ],
        ),
        out_shape=jax.ShapeDtypeStruct((E, rows_per_expert, D), BF16),
        compiler_params=_params(("arbitrary", "arbitrary")),
        name="dispatch_tokens",
    )(base.reshape(-1), cnt.reshape(-1), h, slot.reshape(E, 1, N))


def _expert_kernel(used_ref, x_ref, wg_ref, wu_ref, wd_ref, o_ref):
    @pl.when(pl.program_id(1) * x_ref.shape[0] < used_ref[pl.program_id(0)])
    def _():
        x = x_ref[...]
        a = jnp.dot(x, wg_ref[...], preferred_element_type=F32)
        u = jnp.dot(x, wu_ref[...], preferred_element_type=F32)
        hmid = (a * jax.nn.sigmoid(a) * u).astype(BF16)
        o_ref[...] = jnp.dot(hmid, wd_ref[...], preferred_element_type=F32).astype(o_ref.dtype)


def expert_ffn(xe, used, w_gate, w_up, w_down):
    E, R, D = xe.shape
    F = w_gate.shape[2]
    return pl.pallas_call(
        _expert_kernel,
        grid_spec=pltpu.PrefetchScalarGridSpec(
            num_scalar_prefetch=1,
            grid=(E, R // MOE_TILE),
            in_specs=[
                pl.BlockSpec((None, MOE_TILE, D), lambda e, i, *_: (e, i, 0)),
                pl.BlockSpec((None, D, F), lambda e, i, *_: (e, 0, 0)),
                pl.BlockSpec((None, D, F), lambda e, i, *_: (e, 0, 0)),
                pl.BlockSpec((None, F, D), lambda e, i, *_: (e, 0, 0)),
            ],
            out_specs=pl.BlockSpec((None, MOE_TILE, D), lambda e, i, *_: (e, i, 0)),
        ),
        out_shape=jax.ShapeDtypeStruct((E, R, D), BF16),
        compiler_params=_params(("parallel", "arbitrary")),
        name="expert_ffn",
    )(used, xe, w_gate, w_up, w_down)


def _combine_kernel(base_ref, cnt_ref, lim_ref, x_ref, gt_ref, slot_ref, aff_ref, g_ref, ye_hbm, o_ref,
                    acc, buf, xbuf, sems, xsem, *, nw, ne, final_norm):
    w, e = pl.program_id(0), pl.program_id(1)
    step = w * ne + e
    cur = step % 2
    win = x_ref.shape[0]

    def chunk_start(ww, ee, k):
        b = base_ref[ee * nw + ww]
        return pl.multiple_of(jnp.minimum(b + k * MOE_CHUNK, lim_ref[ee] - MOE_CHUNK), MOE_ALIGN)

    def first_copy(slot_id, ww, ee):
        return pltpu.make_async_copy(ye_hbm.at[ee, pl.ds(chunk_start(ww, ee, 0), MOE_CHUNK)], buf.at[slot_id],
                                     sems.at[slot_id])

    @pl.when(e == 0)
    def _():
        acc[...] = jnp.zeros(acc.shape, F32)

    c = cnt_ref[e * nw + w]

    @pl.when((step == 0) & (c > 0))
    def _():
        first_copy(cur, w, e).start()

    @pl.when(step + 1 < nw * ne)
    def _():
        ns = step + 1

        @pl.when(cnt_ref[(ns % ne) * nw + ns // ne] > 0)
        def _():
            first_copy(1 - cur, ns // ne, ns % ne).start()

    @pl.when(c > 0)
    def _():
        b = base_ref[e * nw + w]
        lane_e = lax.broadcasted_iota(jnp.int32, (win, ne), 1)
        scol = jnp.sum(jnp.where(lane_e == e, slot_ref[...], 0).astype(F32), axis=1, keepdims=True)
        gcol = jnp.sum(jnp.where(lane_e == e, aff_ref[...], 0.0), axis=1, keepdims=True)
        col = lax.broadcasted_iota(jnp.int32, (win, MOE_CHUNK), 1).astype(F32)

        def apply(k, rows):
            start = chunk_start(w, e, k)
            onehot = jnp.where(scol == start.astype(F32) + col, 1.0, 0.0).astype(BF16)
            gate = jnp.where(scol >= (b + k * MOE_CHUNK).astype(F32), gcol, 0.0)
            acc[...] += gate * jnp.dot(onehot, rows, preferred_element_type=F32)

        first_copy(cur, w, e).wait()
        apply(0, buf[cur])

        def extra(k, carry):
            cp = pltpu.make_async_copy(ye_hbm.at[e, pl.ds(chunk_start(w, e, k), MOE_CHUNK)], xbuf, xsem)
            cp.start()
            cp.wait()
            apply(k, xbuf[...])
            return carry

        lax.fori_loop(1, (c + MOE_CHUNK - 1) // MOE_CHUNK, extra, 0)

    @pl.when(e == ne - 1)
    def _():
        x = x_ref[...] + gt_ref[...] * acc[...]
        if final_norm:
            ms = jnp.mean(x * x, axis=-1, keepdims=True)
            x = x * lax.rsqrt(ms + EPS) * g_ref[...]
        o_ref[...] = x


def combine_experts(x, gt, slot_t, aff_t, ye, base, cnt, lim, g_final, seq, *, final_norm):
    N, D = x.shape
    E = slot_t.shape[1]
    nw = N // MOE_WIN
    per_seq = seq // MOE_WIN
    return pl.pallas_call(
        functools.partial(_combine_kernel, nw=nw, ne=E, final_norm=final_norm),
        grid_spec=pltpu.PrefetchScalarGridSpec(
            num_scalar_prefetch=3,
            grid=(nw, E),
            in_specs=[
                pl.BlockSpec((MOE_WIN, D), lambda w, e, *_: (w, 0)),
                pl.BlockSpec((None, 1, D), lambda w, e, *_: (w // per_seq, 0, 0)),
                pl.BlockSpec((MOE_WIN, E), lambda w, e, *_: (w, 0)),
                pl.BlockSpec((MOE_WIN, E), lambda w, e, *_: (w, 0)),
                pl.BlockSpec((1, D), lambda w, e, *_: (0, 0)),
                pl.BlockSpec(memory_space=pl.ANY),
            ],
            out_specs=pl.BlockSpec((MOE_WIN, D), lambda w, e, *_: (w, 0)),
            scratch_shapes=[
                pltpu.VMEM((MOE_WIN, D), F32),
                pltpu.VMEM((2, MOE_CHUNK, D), BF16),
                pltpu.VMEM((MOE_CHUNK, D), BF16),
                pltpu.SemaphoreType.DMA((2,)),
                pltpu.SemaphoreType.DMA,
            ],
        ),
        out_shape=jax.ShapeDtypeStruct((N, D), F32),
        compiler_params=_params(("arbitrary", "arbitrary")),
        name="combine_experts",
    )(base.reshape(-1), cnt.reshape(-1), lim, x, gt, slot_t, aff_t, g_final.reshape(1, D), ye)


def _prepare_layer(l, p, D):
    w_in = p['w_in'][l]
    W = N_HEADS * HEAD_DIM
    KW = GQA_KV_HEADS * HEAD_DIM
    qr = p['mla_w_uq'].shape[1]
    kvr = p['mla_w_ukv'].shape[1]
    o = 0
    seg = {}
    for name, width in (('a', 3 * W), ('cq', qr), ('ckv', kvr), ('kr', MLA_ROPE), ('gq', W), ('gk', KW), ('gv', KW),
                        ('gate', N_BRANCH * D)):
        seg[name] = w_in[:, o:o + width]
        o += width
    gq_cols = np.concatenate([np.arange(h * HEAD_DIM, (h + 1) * HEAD_DIM) for h in GQA_HEAD_ORDER])
    zeros = lambda n: jnp.zeros((D, n), w_in.dtype)
    gv = [p_ for g in range(GQA_KV_HEADS)
          for p_ in (seg['gv'][:, g * HEAD_DIM:(g + 1) * HEAD_DIM], zeros(LANE - HEAD_DIM))]
    w_mix = jnp.concatenate(
        [seg['a'], seg['cq'], seg['gq'][:, gq_cols], seg['ckv'],
         zeros(MLA_NOPE), seg['kr'], zeros(LANE - MLA_NOPE - MLA_ROPE), seg['gk']] + gv, axis=1).astype(BF16)

    w_uq = p['mla_w_uq'][l].reshape(qr, N_HEADS, MLA_NOPE + MLA_ROPE)
    wq = jnp.pad(w_uq, ((0, 0), (0, 0), (0, LANE - MLA_NOPE - MLA_ROPE))).reshape(qr, N_HEADS * LANE).astype(BF16)
    w_ukv = p['mla_w_ukv'][l].reshape(kvr, N_HEADS, MLA_NOPE + MLA_V)
    wkn = jnp.pad(w_ukv[:, :, :MLA_NOPE], ((0, 0), (0, 0), (0, LANE - MLA_NOPE))).reshape(kvr, N_HEADS * LANE)
    wv = jnp.pad(w_ukv[:, :, MLA_NOPE:], ((0, 0), (0, 0), (0, LANE - MLA_V))).reshape(kvr, N_HEADS * LANE)

    w_branch = p['w_branch'][l]
    w_branch = jnp.stack([w_branch[0], w_branch[1], w_branch[2][gq_cols]]).astype(BF16)
    return dict(
        w_mix=w_mix, w_gate_logits=seg['gate'].astype(BF16),
        wq=wq, wkn=wkn.astype(BF16), wv=wv.astype(BF16),
        w_branch=w_branch, w_out=p['w_out'][l].astype(BF16),
        w_router_t=p['w_router'][l].T.astype(BF16),
        w_exp_gate=p['w_exp_gate'][l].astype(BF16), w_exp_up=p['w_exp_up'][l].astype(BF16),
        w_exp_down=p['w_exp_down'][l].astype(BF16),
    )


def _trunk(x3, mod, p, layers, tables):
    B, T, D = x3.shape
    N = B * T
    x = x3.reshape(N, D)
    depth = len(layers)
    for l in range(depth):
        w = layers[l]
        sh1, sc1, gt1, sh2, sc2, gt2 = [mod[l][:, None, i * D:(i + 1) * D] for i in range(6)]
        mix = in_projection(x, p['g_mix'][l], sc1, sh1, w['w_mix'], T, tn=w['w_mix'].shape[1] // 2,
                            sigmoid_out=False, name="mix_projection")
        sgate = in_projection(x, p['g_mix'][l], sc1, sh1, w['w_gate_logits'], T, tn=_tile(N_BRANCH * D, 768),
                              sigmoid_out=True, name="gate_projection")
        o_a = neighborhood_attention(mix, tables['na_bias'][l], B, T)
        qm, km, vm = mla_prepare(mix, p['mla_g_q'][l], p['mla_g_kv'][l], w['wq'], w['wkn'], w['wv'],
                                 tables['mla_q'][T], tables['mla_k'][T], T)
        o_b = flash_attention(qm, km, vm, B, T, shared_k=False, name="mla_attention")
        qg, kg, vg = gqa_prepare(mix, p['gqa_g_q'][l], p['gqa_g_k'][l], tables['gqa_q'][T], tables['gqa_k'][T], T)
        o_c = flash_attention(qg, kg, vg, B, T, shared_k=True, name="gqa_attention")
        x = merge_branches(o_a, o_b, o_c, sgate, x, gt1, w['w_branch'], w['w_out'], T)

        h, aff = router(x, p['g_ffn'][l], sc2, sh2, w['w_router_t'], T)
        cap = EC_CAPACITY * N // N_EXPERTS
        slot, base, cnt = select_tokens(aff, cap)
        used = base[:, -1] + _round_up(cnt[:, -1], MOE_ALIGN)
        rows = _round_up(cap + MOE_ALIGN * (N // MOE_WIN), MOE_TILE)
        xe = dispatch_tokens(h, slot, base, cnt, rows)
        ye = expert_ffn(xe, used, w['w_exp_gate'], w['w_exp_up'], w['w_exp_down'])
        x = combine_experts(x, gt2, slot.T, aff.T, ye, base, cnt, _round_up(used, MOE_TILE), p['g_final'], T,
                            final_norm=l == depth - 1)
    return x.reshape(B, T, D)


def kernel(x_prompt, x_sample, c_prompt, c_sample, w_ada, b_ada, g_mix, g_ffn, w_in, na_rel_bias, mla_g_q, mla_w_uq,
           mla_g_kv, mla_w_ukv, gqa_g_q, gqa_g_k, w_branch, w_out, w_router, w_exp_gate, w_exp_up, w_exp_down,
           g_final):
    p = dict(g_mix=g_mix, g_ffn=g_ffn, w_in=w_in, mla_g_q=mla_g_q, mla_w_uq=mla_w_uq, mla_g_kv=mla_g_kv,
             mla_w_ukv=mla_w_ukv, gqa_g_q=gqa_g_q, gqa_g_k=gqa_g_k, w_branch=w_branch, w_out=w_out,
             w_router=w_router, w_exp_gate=w_exp_gate, w_exp_up=w_exp_up, w_exp_down=w_exp_down, g_final=g_final)
    depth, D = g_mix.shape
    bp, bs = c_prompt.shape[0], c_sample.shape[0]
    rows = -(-(bp + bs) // 8) * 8
    c_all = jnp.concatenate([c_prompt, c_sample, jnp.zeros((rows - bp - bs, D), F32)], axis=0)
    mod = ada_modulation(c_all, w_ada, b_ada)
    layers = [_prepare_layer(l, p, D) for l in range(depth)]
    tables = dict(na_bias=[_na_bias_table(na_rel_bias[l]) for l in range(depth)],
                  mla_q={}, mla_k={}, gqa_q={}, gqa_k={})
    for T in {x_prompt.shape[1], x_sample.shape[1]}:
        tables['mla_q'][T], tables['mla_k'][T] = _mla_tables(T)
        tables['gqa_q'][T], tables['gqa_k'][T] = _gqa_tables(T)
    y_prompt = _trunk(x_prompt, mod[:, :bp], p, layers, tables)
    y_sample = _trunk(x_sample, mod[:, bp:bp + bs], p, layers, tables)
    return (y_prompt, y_sample)
```

```python
import functools
import math

import numpy as np
import jax
import jax.numpy as jnp
from jax import lax
from jax.experimental import pallas as pl
from jax.experimental.pallas import tpu as pltpu

F32 = jnp.float32
BF16 = jnp.bfloat16

GRID_W = 64
HEAD_DIM = 64
N_HEADS = 8
NA_WIN_H = 8
NA_WIN_W = 16
MLA_NOPE = 64
MLA_ROPE = 32
MLA_V = 64
GQA_KV_HEADS = 2
N_BRANCH = 3
N_EXPERTS = 16
EC_CAPACITY = 2
ROPE_THETA = 10000.0
EPS = 1e-6
LOG2E = 1.4426950408889634

LANE = 128
NA_ROWS_PER_BLOCK = 4
NA_BLOCK = NA_ROWS_PER_BLOCK * GRID_W
NA_KEY_BLOCKS = 3
NEG = -1e30
VMEM_LIMIT = 56 * 1024 * 1024

GQA_HEAD_ORDER = (0, 4, 1, 5, 2, 6, 3, 7)


def _params(sem):
    return pltpu.CompilerParams(dimension_semantics=sem, vmem_limit_bytes=VMEM_LIMIT)


def _tile(n, pref):
    t = min(n, pref)
    while n % t:
        t //= 2
    return t


def _ada_kernel(c_ref, w_ref, b_ref, o_ref):
    c = c_ref[...]
    a = (c * jax.nn.sigmoid(c)).astype(BF16)
    o_ref[...] = jnp.dot(a, w_ref[...].astype(BF16), preferred_element_type=F32) + b_ref[...]


def ada_modulation(c, w_ada, b_ada):
    L, D, D6 = w_ada.shape
    R = c.shape[0]
    tn = _tile(D6, 1024)
    return pl.pallas_call(
        _ada_kernel,
        grid=(L, D6 // tn),
        in_specs=[
            pl.BlockSpec((R, D), lambda l, j: (0, 0)),
            pl.BlockSpec((None, D, tn), lambda l, j: (l, 0, j)),
            pl.BlockSpec((None, 1, tn), lambda l, j: (l, 0, j)),
        ],
        out_specs=pl.BlockSpec((None, R, tn), lambda l, j: (l, 0, j)),
        out_shape=jax.ShapeDtypeStruct((L, R, D6), F32),
        compiler_params=_params(("parallel", "parallel")),
        name="ada_modulation",
    )(c, w_ada, b_ada.reshape(L, 1, D6))


def _norm_mod(x, g, sc, sh):
    ms = jnp.mean(x * x, axis=-1, keepdims=True)
    y = x * lax.rsqrt(ms + EPS)
    return (y * g) * (1.0 + sc) + sh


def _in_proj_kernel(x_ref, g_ref, sc_ref, sh_ref, w_ref, o_ref, h_sc, *, sigmoid_out):
    @pl.when(pl.program_id(1) == 0)
    def _():
        h_sc[...] = _norm_mod(x_ref[...], g_ref[...], sc_ref[...], sh_ref[...]).astype(BF16)

    acc = jnp.dot(h_sc[...], w_ref[...], preferred_element_type=F32)
    if sigmoid_out:
        acc = jax.nn.sigmoid(acc)
    o_ref[...] = acc.astype(o_ref.dtype)


def in_projection(x, g, sc, sh, w, seq, *, tn, sigmoid_out, name):
    N, D = x.shape
    C = w.shape[1]
    tm = _tile(seq, 1024)
    per_seq = seq // tm
    return pl.pallas_call(
        functools.partial(_in_proj_kernel, sigmoid_out=sigmoid_out),
        grid=(N // tm, C // tn),
        in_specs=[
            pl.BlockSpec((tm, D), lambda i, j: (i, 0)),
            pl.BlockSpec((1, D), lambda i, j: (0, 0)),
            pl.BlockSpec((None, 1, D), lambda i, j: (i // per_seq, 0, 0)),
            pl.BlockSpec((None, 1, D), lambda i, j: (i // per_seq, 0, 0)),
            pl.BlockSpec((D, tn), lambda i, j: (0, j)),
        ],
        out_specs=pl.BlockSpec((tm, tn), lambda i, j: (i, j)),
        out_shape=jax.ShapeDtypeStruct((N, C), BF16),
        scratch_shapes=[pltpu.VMEM((tm, D), BF16)],
        compiler_params=_params(("parallel", "arbitrary")),
        name=name,
    )(x, g.reshape(1, D), sc, sh, w)


def _na_bias_table(rel_bias):
    R = NA_ROWS_PER_BLOCK
    KR = NA_KEY_BLOCKS * R
    q_rows = np.stack([np.arange(R), R + np.arange(R), 2 * R + np.arange(R)])
    w_start = np.stack([np.zeros(R, np.int64), np.arange(R), np.full(R, KR - NA_WIN_H)])
    col = np.arange(GRID_W)
    col_start = np.clip(col - NA_WIN_W // 2, 0, GRID_W - NA_WIN_W)
    k_row = np.arange(KR)
    k_col = np.arange(GRID_W)
    row_ok = (k_row[None, None, :] >= w_start[:, :, None]) & (k_row[None, None, :] < w_start[:, :, None] + NA_WIN_H)
    col_ok = (k_col[None, :] >= col_start[:, None]) & (k_col[None, :] < col_start[:, None] + NA_WIN_W)
    dr = k_row[None, None, :] - q_rows[:, :, None] + (NA_WIN_H - 1)
    dc = k_col[None, :] - col[:, None] + (NA_WIN_W - 1)
    H = rel_bias.shape[0]
    pick_c = ((dc[None] == np.arange(2 * NA_WIN_W - 1)[:, None, None]) & col_ok[None]).astype(np.float32)
    by_col = jnp.einsum('hab,bqk->haqk', rel_bias.astype(F32), pick_c, precision=lax.Precision.HIGHEST)
    by_col = jnp.where(col_ok, by_col, NEG)
    masked = jnp.full((H, GRID_W, GRID_W), NEG, F32)
    types = []
    for t in range(3):
        q_blocks = []
        for qr in range(R):
            q_blocks.append(jnp.concatenate(
                [by_col[:, int(dr[t, qr, kr])] if row_ok[t, qr, kr] else masked for kr in range(KR)], axis=2))
        types.append(jnp.concatenate(q_blocks, axis=1))
    return jnp.stack(types)


def _na_kernel(q_ref, k0_ref, k1_ref, k2_ref, v0_ref, v1_ref, v2_ref, bias_ref, o_ref):
    lane = lax.broadcasted_iota(jnp.int32, (NA_BLOCK, LANE), 1)
    lo = lane < HEAD_DIM
    scale = HEAD_DIM ** -0.5
    for u in range(N_HEADS // 2):
        cs = slice(u * LANE, (u + 1) * LANE)
        q = q_ref[:, cs] * scale
        ks = (k0_ref[:, cs], k1_ref[:, cs], k2_ref[:, cs])
        vs = (v0_ref[:, cs], v1_ref[:, cs], v2_ref[:, cs])
        outs = []
        for hh in range(2):
            qh = jnp.where(lo if hh == 0 else ~lo, q, jnp.zeros_like(q))
            s = jnp.concatenate(
                [lax.dot_general(qh, k, (((1,), (1,)), ((), ())), preferred_element_type=F32) for k in ks], axis=1)
            s = s + bias_ref[2 * u + hh]
            m = jnp.max(s, axis=1, keepdims=True)
            p = jnp.exp(s - m)
            l = jnp.sum(p, axis=1, keepdims=True)
            pb = p.astype(BF16)
            o = None
            for j in range(NA_KEY_BLOCKS):
                t = jnp.dot(pb[:, j * NA_BLOCK:(j + 1) * NA_BLOCK], vs[j], preferred_element_type=F32)
                o = t if o is None else o + t
            outs.append(o / l)
        o_ref[:, cs] = jnp.where(lo, outs[0], outs[1]).astype(o_ref.dtype)


def neighborhood_attention(mix, bias_table, batch, seq):
    N = mix.shape[0]
    W = N_HEADS * HEAD_DIM
    nb = seq // NA_BLOCK
    assert seq % NA_BLOCK == 0 and nb >= NA_KEY_BLOCKS

    def kv_spec(colblk, j):
        return pl.BlockSpec(
            (NA_BLOCK, W),
            lambda b, i: (b * nb + jnp.clip(i - 1, 0, nb - NA_KEY_BLOCKS) + j, colblk))

    def bias_idx(b, i):
        return (jnp.where(i == 0, 0, jnp.where(i == nb - 1, 2, 1)), 0, 0, 0)

    return pl.pallas_call(
        _na_kernel,
        grid=(batch, nb),
        in_specs=[pl.BlockSpec((NA_BLOCK, W), lambda b, i: (b * nb + i, 0))]
        + [kv_spec(1, j) for j in range(NA_KEY_BLOCKS)]
        + [kv_spec(2, j) for j in range(NA_KEY_BLOCKS)]
        + [pl.BlockSpec((None, N_HEADS, NA_BLOCK, NA_KEY_BLOCKS * NA_BLOCK), bias_idx)],
        out_specs=pl.BlockSpec((NA_BLOCK, W), lambda b, i: (b * nb + i, 0)),
        out_shape=jax.ShapeDtypeStruct((N, W), BF16),
        compiler_params=_params(("parallel", "arbitrary")),
        name="neighborhood_attention",
    )(mix, mix, mix, mix, mix, mix, mix, bias_table)


def _rope_angles(pos, dim):
    inv_freq = ROPE_THETA ** (-jnp.arange(0, dim, 2, dtype=F32) / dim)
    ang = pos.astype(F32)[:, None] * inv_freq[None, :]
    return jnp.concatenate([ang, ang], axis=-1)


def _rot_tables(cos, sin, scale):
    n = cos.shape[1]
    first = (np.arange(n) % 32) < 16
    sa = jnp.where(first[None, :], -sin, 0.0)
    sb = jnp.where(first[None, :], 0.0, sin)
    return cos * scale, sa * scale, sb * scale


def _mla_tables(seq):
    pos = jnp.arange(seq)
    ang = _rope_angles(pos, MLA_ROPE)
    cos, sin = jnp.cos(ang), jnp.sin(ang)
    pad_hi = LANE - MLA_NOPE - MLA_ROPE

    def place(t, nope_val):
        return jnp.concatenate(
            [jnp.full((seq, MLA_NOPE), nope_val, F32), t, jnp.zeros((seq, pad_hi), F32)], axis=1)

    qs = (MLA_NOPE + MLA_ROPE) ** -0.5 * LOG2E
    qc, qa, qb = _rot_tables(cos, sin, qs)
    kc, ka, kb = _rot_tables(cos, sin, 1.0)
    q_tab = jnp.stack([place(qc, qs), place(qa, 0.0), place(qb, 0.0)])
    k_tab = jnp.stack([place(kc, 0.0), place(ka, 0.0), place(kb, 0.0)])
    return q_tab, k_tab


def _gqa_tables(seq):
    pos = jnp.arange(seq)
    half = HEAD_DIM // 2
    ar = _rope_angles(pos // GRID_W, half)
    ac = _rope_angles(pos % GRID_W, half)
    cos = jnp.concatenate([jnp.cos(ar), jnp.cos(ac)] * 2, axis=1)
    sin = jnp.concatenate([jnp.sin(ar), jnp.sin(ac)] * 2, axis=1)
    qs = HEAD_DIM ** -0.5 * LOG2E
    return jnp.stack(_rot_tables(cos, sin, qs)), jnp.stack(_rot_tables(cos, sin, 1.0))


def _apply_rot(x, tab_ref):
    n = x.shape[1]
    return (x * tab_ref[0] + pltpu.roll(x, n - 16, 1) * tab_ref[1] + pltpu.roll(x, 16, 1) * tab_ref[2])


def _rms(x, g):
    ms = jnp.mean(x * x, axis=-1, keepdims=True)
    return x * lax.rsqrt(ms + EPS) * g


def _mla_prep_kernel(cq_ref, ckv_ref, kr_ref, gq_ref, gkv_ref, wq_ref, wkn_ref, wv_ref, qtab_ref, ktab_ref,
                     q_out, k_out, v_out):
    cqn = _rms(cq_ref[...].astype(F32), gq_ref[...]).astype(BF16)
    ckvn = _rms(ckv_ref[...].astype(F32), gkv_ref[...]).astype(BF16)
    qf = jnp.dot(cqn, wq_ref[...], preferred_element_type=F32)
    kn = jnp.dot(ckvn, wkn_ref[...], preferred_element_type=F32)
    tm = cq_ref.shape[0]
    hi_lane = lax.broadcasted_iota(jnp.int32, (tm, LANE), 1) >= MLA_V
    vf = jnp.dot(ckvn, wv_ref[...], preferred_element_type=F32)
    k_rope = _apply_rot(kr_ref[...].astype(F32), ktab_ref)
    for h in range(N_HEADS):
        cs = slice(h * LANE, (h + 1) * LANE)
        q_out[:, cs] = _apply_rot(qf[:, cs], qtab_ref).astype(q_out.dtype)
        k_out[:, cs] = (kn[:, cs] + k_rope).astype(k_out.dtype)
        v_out[:, cs] = jnp.where(hi_lane, 1.0, vf[:, cs]).astype(v_out.dtype)


def mla_prepare(mix, g_q, g_kv, wq, wkn, wv, q_tab, k_tab, seq):
    N = mix.shape[0]
    tm = _tile(seq, 512)
    per_seq = seq // tm
    qr, kvr = wq.shape[0], wkn.shape[0]
    cq_blk = 1536 // qr
    ckv_blk = 2560 // kvr
    kr_blk = 2816 // LANE
    const = lambda i: (0, 0)
    tab = pl.BlockSpec((3, tm, LANE), lambda i: (0, i % per_seq, 0))
    return pl.pallas_call(
        _mla_prep_kernel,
        grid=(N // tm,),
        in_specs=[
            pl.BlockSpec((tm, qr), lambda i: (i, cq_blk)),
            pl.BlockSpec((tm, kvr), lambda i: (i, ckv_blk)),
            pl.BlockSpec((tm, LANE), lambda i: (i, kr_blk)),
            pl.BlockSpec((1, qr), const),
            pl.BlockSpec((1, kvr), const),
            pl.BlockSpec(wq.shape, const),
            pl.BlockSpec(wkn.shape, const),
            pl.BlockSpec(wv.shape, const),
            tab, tab,
        ],
        out_specs=[
            pl.BlockSpec((tm, N_HEADS * LANE), lambda i: (i, 0)),
            pl.BlockSpec((tm, N_HEADS * LANE), lambda i: (i, 0)),
            pl.BlockSpec((tm, N_HEADS * LANE), lambda i: (i, 0)),
        ],
        out_shape=[jax.ShapeDtypeStruct((N, N_HEADS * LANE), BF16)] * 3,
        compiler_params=_params(("parallel",)),
        name="mla_prepare",
    )(mix, mix, mix, g_q.reshape(1, qr), g_kv.reshape(1, kvr), wq, wkn, wv, q_tab, k_tab)


def _head_rms(x, g, seg_ref):
    sq = x * x
    hi = sq.astype(BF16)
    lo = (sq - hi.astype(F32)).astype(BF16)
    ms = (jnp.dot(hi, seg_ref[...], preferred_element_type=F32)
          + jnp.dot(lo, seg_ref[...], preferred_element_type=F32)) * (1.0 / HEAD_DIM)
    return x * lax.rsqrt(ms + EPS) * g


def _gqa_prep_kernel(q_ref, k_ref, v_ref, gq_ref, gk_ref, segq_ref, segk_ref, qtab_ref, ktab_ref,
                     q_out, k_out, v_out):
    tm = q_ref.shape[0]
    lane = lax.broadcasted_iota(jnp.int32, (tm, LANE), 1)
    lo = lane < HEAD_DIM
    for g in range(GQA_KV_HEADS):
        cs = slice(g * LANE, (g + 1) * LANE)
        v_out[:, cs] = jnp.where(lo, v_ref[:, cs], jnp.ones((tm, LANE), v_ref.dtype))
    qn = _head_rms(q_ref[...].astype(F32), gq_ref[...], segq_ref)
    for u in range(N_HEADS // 2):
        y = _apply_rot(qn[:, u * LANE:(u + 1) * LANE], qtab_ref)
        q_out[:, (2 * u) * LANE:(2 * u + 1) * LANE] = jnp.where(lo, y, 0.0).astype(q_out.dtype)
        q_out[:, (2 * u + 1) * LANE:(2 * u + 2) * LANE] = jnp.where(lo, 0.0, y).astype(q_out.dtype)
    kn = _head_rms(k_ref[...].astype(F32), gk_ref[...], segk_ref)
    k_out[...] = _apply_rot(kn, ktab_ref).astype(k_out.dtype)


def gqa_prepare(mix, g_q, g_k, q_tab, k_tab, seq):
    N = mix.shape[0]
    tm = _tile(seq, 512)
    per_seq = seq // tm
    W = N_HEADS * HEAD_DIM
    KW = GQA_KV_HEADS * HEAD_DIM
    VW = GQA_KV_HEADS * LANE
    const = lambda i: (0, 0)
    seg = lambda n: jnp.asarray(np.kron(np.eye(n // HEAD_DIM), np.ones((HEAD_DIM, HEAD_DIM))), BF16)
    tab = pl.BlockSpec((3, tm, LANE), lambda i: (0, i % per_seq, 0))
    return pl.pallas_call(
        _gqa_prep_kernel,
        grid=(N // tm,),
        in_specs=[
            pl.BlockSpec((tm, W), lambda i: (i, 2048 // W)),
            pl.BlockSpec((tm, KW), lambda i: (i, 2944 // KW)),
            pl.BlockSpec((tm, VW), lambda i: (i, 3072 // VW)),
            pl.BlockSpec((1, W), const),
            pl.BlockSpec((1, KW), const),
            pl.BlockSpec((W, W), const),
            pl.BlockSpec((KW, KW), const),
            tab, tab,
        ],
        out_specs=[
            pl.BlockSpec((tm, N_HEADS * LANE), lambda i: (i, 0)),
            pl.BlockSpec((tm, KW), lambda i: (i, 0)),
            pl.BlockSpec((tm, VW), lambda i: (i, 0)),
        ],
        out_shape=[
            jax.ShapeDtypeStruct((N, N_HEADS * LANE), BF16),
            jax.ShapeDtypeStruct((N, KW), BF16),
            jax.ShapeDtypeStruct((N, VW), BF16),
        ],
        compiler_params=_params(("parallel",)),
        name="gqa_prepare",
    )(mix, mix, mix, jnp.tile(g_q, N_HEADS).reshape(1, W), jnp.tile(g_k, GQA_KV_HEADS).reshape(1, KW),
      seg(W), seg(KW), q_tab, k_tab)


def _flash_kernel(q_ref, k_ref, v_ref, o_ref, m_sc, acc_sc, *, shared_k, v_groups):
    kv = pl.program_id(2)
    tq = q_ref.shape[0]
    tk = k_ref.shape[0]

    @pl.when(kv == 0)
    def _():
        m_sc[...] = jnp.full(m_sc.shape, -jnp.inf, F32)
        acc_sc[...] = jnp.zeros(acc_sc.shape, F32)

    for h in range(N_HEADS):
        q = q_ref[:, h * LANE:(h + 1) * LANE]
        k = k_ref[...] if shared_k else k_ref[:, h * LANE:(h + 1) * LANE]
        g = h % v_groups
        v = v_ref[:, g * LANE:(g + 1) * LANE]
        s = lax.dot_general(q, k, (((1,), (1,)), ((), ())), preferred_element_type=F32)
        m_prev = m_sc[h]
        m_new = jnp.maximum(m_prev, jnp.max(s, axis=1, keepdims=True))
        alpha = jnp.exp2(m_prev - m_new)
        p = jnp.exp2(s - jnp.concatenate([m_new] * (tk // LANE), axis=1))
        acc_sc[h] = alpha * acc_sc[h] + jnp.dot(p.astype(BF16), v, preferred_element_type=F32)
        m_sc[h] = m_new

    @pl.when(kv == pl.num_programs(2) - 1)
    def _():
        lane = lax.broadcasted_iota(jnp.int32, (tq, LANE), 1)
        lo = lane < HEAD_DIM
        for u in range(N_HEADS // 2):
            a = acc_sc[2 * u]
            b = acc_sc[2 * u + 1]
            a = a / pltpu.roll(a, HEAD_DIM, 1)
            b = pltpu.roll(b, HEAD_DIM, 1) / b
            o_ref[:, u * LANE:(u + 1) * LANE] = jnp.where(lo, a, b).astype(o_ref.dtype)


def flash_attention(q, k, v, batch, seq, *, shared_k, name):
    N = q.shape[0]
    tq = _tile(seq, 1024)
    tk = _tile(seq, 1024)
    nq, nk = seq // tq, seq // tk
    kw, vw = k.shape[1], v.shape[1]
    ow = N_HEADS * HEAD_DIM
    return pl.pallas_call(
        functools.partial(_flash_kernel, shared_k=shared_k, v_groups=vw // LANE),
        grid=(batch, nq, nk),
        in_specs=[
            pl.BlockSpec((tq, N_HEADS * LANE), lambda b, i, j: (b * nq + i, 0)),
            pl.BlockSpec((tk, kw), lambda b, i, j: (b * nk + j, 0)),
            pl.BlockSpec((tk, vw), lambda b, i, j: (b * nk + j, 0)),
        ],
        out_specs=pl.BlockSpec((tq, ow), lambda b, i, j: (b * nq + i, 0)),
        out_shape=jax.ShapeDtypeStruct((N, ow), BF16),
        scratch_shapes=[pltpu.VMEM((N_HEADS, tq, LANE), F32)] * 2,
        compiler_params=_params(("parallel", "parallel", "arbitrary")),
        name=name,
    )(q, k, v)


def _merge_kernel(oa_ref, ob_ref, oc_ref, sg_ref, x_ref, gt_ref, wb_ref, wo_ref, o_ref):
    D = x_ref.shape[1]
    merged = None
    for i, o_r in enumerate((oa_ref, ob_ref, oc_ref)):
        t = jnp.dot(o_r[...], wb_ref[i], preferred_element_type=F32)
        t = sg_ref[:, i * D:(i + 1) * D].astype(F32) * t
        merged = t if merged is None else merged + t
    mix = jnp.dot(merged.astype(BF16), wo_ref[...], preferred_element_type=F32)
    o_ref[...] = x_ref[...] + gt_ref[...] * mix


def merge_branches(o_a, o_b, o_c, sgate, x, gt, w_branch, w_out, seq):
    N, D = x.shape
    W = o_a.shape[1]
    tm = _tile(seq, 256)
    per_seq = seq // tm
    row = lambda i: (i, 0)
    return pl.pallas_call(
        _merge_kernel,
        grid=(N // tm,),
        in_specs=[
            pl.BlockSpec((tm, W), row), pl.BlockSpec((tm, W), row), pl.BlockSpec((tm, W), row),
            pl.BlockSpec((tm, N_BRANCH * D), row),
            pl.BlockSpec((tm, D), row),
            pl.BlockSpec((None, 1, D), lambda i: (i // per_seq, 0, 0)),
            pl.BlockSpec((N_BRANCH, W, D), lambda i: (0, 0, 0)),
            pl.BlockSpec((D, D), lambda i: (0, 0)),
        ],
        out_specs=pl.BlockSpec((tm, D), row),
        out_shape=jax.ShapeDtypeStruct((N, D), F32),
        compiler_params=_params(("parallel",)),
        name="merge_branches",
    )(o_a, o_b, o_c, sgate, x, gt, w_branch, w_out)


def _router_kernel(x_ref, g_ref, sc_ref, sh_ref, wr_ref, h_out, aff_out):
    h = _norm_mod(x_ref[...], g_ref[...], sc_ref[...], sh_ref[...]).astype(BF16)
    h_out[...] = h
    logits = lax.dot_general(wr_ref[...], h, (((1,), (1,)), ((), ())), preferred_element_type=F32)
    m = jnp.max(logits, axis=0, keepdims=True)
    p = jnp.exp(logits - m)
    aff_out[...] = p / jnp.sum(p, axis=0, keepdims=True)


def router(x, g, sc, sh, w_router_t, seq):
    N, D = x.shape
    E = w_router_t.shape[0]
    tm = _tile(seq, 512)
    per_seq = seq // tm
    return pl.pallas_call(
        _router_kernel,
        grid=(N // tm,),
        in_specs=[
            pl.BlockSpec((tm, D), lambda i: (i, 0)),
            pl.BlockSpec((1, D), lambda i: (0, 0)),
            pl.BlockSpec((None, 1, D), lambda i: (i // per_seq, 0, 0)),
            pl.BlockSpec((None, 1, D), lambda i: (i // per_seq, 0, 0)),
            pl.BlockSpec((E, D), lambda i: (0, 0)),
        ],
        out_specs=[pl.BlockSpec((tm, D), lambda i: (i, 0)), pl.BlockSpec((E, tm), lambda i: (0, i))],
        out_shape=[jax.ShapeDtypeStruct((N, D), BF16), jax.ShapeDtypeStruct((E, N), F32)],
        compiler_params=_params(("parallel",)),
        name="router",
    )(x, g.reshape(1, D), sc, sh, w_router_t)


MOE_WIN = 512
MOE_ALIGN = 16
MOE_CHUNK = 128
MOE_TILE = 512
_SEG_PIECES = (512, 256, 128, 64, 32, 16)


def _round_up(x, m):
    return (x + (m - 1)) // m * m


def _select_kernel(aff_ref, slot_ref, wbase_ref, wcnt_ref, *, cap, chunks_per_win):
    a = aff_ref[...]
    nc = a.shape[0]
    bits = pltpu.bitcast(a, jnp.int32)

    def search(i, t):
        cand = t | jnp.left_shift(jnp.int32(1), 30 - i)
        n_ge = jnp.sum((bits >= cand).astype(jnp.int32))
        return jnp.where(n_ge >= cap, cand, t)

    thr = lax.fori_loop(0, 31, search, jnp.int32(0))
    gt = bits > thr
    eq = bits == thr
    need = (cap - jnp.sum(gt.astype(jnp.int32))).astype(F32)

    li = lax.broadcasted_iota(jnp.int32, (LANE, LANE), 0)
    lj = lax.broadcasted_iota(jnp.int32, (LANE, LANE), 1)
    tri = jnp.where(li <= lj, 1.0, 0.0).astype(BF16)
    ones = jnp.ones((LANE, LANE), BF16)
    ci = lax.broadcasted_iota(jnp.int32, (nc, nc), 0)
    cj = lax.broadcasted_iota(jnp.int32, (nc, nc), 1)
    shift = chunks_per_win.bit_length() - 1
    wi, wj = jnp.right_shift(ci, shift), jnp.right_shift(cj, shift)
    as_bf = lambda m: jnp.where(m, 1.0, 0.0).astype(BF16)
    dot = lambda x, y: jnp.dot(x, y, preferred_element_type=F32)

    eq_b = as_bf(eq)
    eq_rank = dot(as_bf(cj < ci), dot(eq_b, ones).astype(BF16)) + dot(eq_b, tri) - eq_b.astype(F32)
    sel = gt | (eq & (eq_rank < need))
    sel_b = as_bf(sel)
    incl = dot(sel_b, tri)
    tot = dot(sel_b, ones).astype(BF16)
    in_win = dot(as_bf((cj < ci) & (wj == wi)), tot)
    wcnt = dot(as_bf(wj == wi), tot)
    wpad = jnp.ceil(wcnt * (1.0 / MOE_ALIGN)) * MOE_ALIGN
    first = (cj & (chunks_per_win - 1)) == 0
    wbase = dot(as_bf((wj < wi) & first), wpad.astype(BF16))
    slot = wbase + in_win + incl - sel_b.astype(F32)
    slot_ref[...] = jnp.where(sel, slot, -1.0).astype(jnp.int32)
    wbase_ref[...] = wbase.astype(jnp.int32)
    wcnt_ref[...] = wcnt.astype(jnp.int32)


def select_tokens(aff, cap):
    E, N = aff.shape
    nc = N // LANE
    cpw = MOE_WIN // LANE
    blk = pl.BlockSpec((None, nc, LANE), lambda e: (e, 0, 0))
    out = jax.ShapeDtypeStruct((E, nc, LANE), jnp.int32)
    slot, wbase, wcnt = pl.pallas_call(
        functools.partial(_select_kernel, cap=cap, chunks_per_win=cpw),
        grid=(E,),
        in_specs=[blk],
        out_specs=[blk, blk, blk],
        out_shape=[out, out, out],
        compiler_params=_params(("parallel",)),
        name="select_tokens",
    )(aff.reshape(E, nc, LANE))
    return slot.reshape(E, N), wbase[:, ::cpw, 0], wcnt[:, ::cpw, 0]


def _for_segment_pieces(n_rows, make_copy, act):
    off = jnp.int32(0)
    for piece in _SEG_PIECES:
        has = (n_rows & piece) != 0

        @pl.when(has)
        def _(off=off, piece=piece):
            act(make_copy(pl.multiple_of(off, MOE_ALIGN), piece))

        off = off + jnp.where(has, piece, 0)


MOE_GROUP = 4


def _dispatch_kernel(base_ref, cnt_ref, h_ref, slot_ref, xe_hbm, stage, xstage, zero_sc, sems, xsem, *, nw, ne):
    w = pl.program_id(0)
    cur = w % 2
    win = h_ref.shape[0]
    row_id = lax.broadcasted_iota(jnp.int32, (MOE_CHUNK, win), 0)

    def seg_rows(ww, ee):
        return _round_up(cnt_ref[ee * nw + ww], MOE_ALIGN)

    def head_copy(buf, ww, ee):
        b = base_ref[ee * nw + ww]

        def make(off, piece):
            return pltpu.make_async_copy(stage.at[buf, pl.ds(pl.multiple_of(ee * MOE_CHUNK + off, MOE_ALIGN), piece)],
                                         xe_hbm.at[ee, pl.ds(pl.multiple_of(b + off, MOE_ALIGN), piece)],
                                         sems.at[buf])
        return make

    def for_heads(buf, ww, act):
        for ee in range(ne):
            _for_segment_pieces(jnp.minimum(seg_rows(ww, ee), MOE_CHUNK), head_copy(buf, ww, ee), act)

    @pl.when(w == 0)
    def _():
        zero_sc[...] = jnp.zeros(zero_sc.shape, zero_sc.dtype)

    @pl.when(w >= 2)
    def _():
        for_heads(cur, w - 2, lambda c: c.wait())

    h = h_ref[...]

    def onehot(ee, k):
        want = base_ref[ee * nw + w] + k * MOE_CHUNK + row_id
        return jnp.where(slot_ref[ee:ee + 1, :] == want, 1.0, 0.0).astype(BF16)

    for g in range(ne // MOE_GROUP):
        sel = jnp.concatenate([onehot(g * MOE_GROUP + i, 0) for i in range(MOE_GROUP)], axis=0)
        stage[cur, g * MOE_GROUP * MOE_CHUNK:(g + 1) * MOE_GROUP * MOE_CHUNK, :] = jnp.dot(
            sel, h, preferred_element_type=F32).astype(stage.dtype)
    for_heads(cur, w, lambda c: c.start())

    for ee in range(ne):
        rows = seg_rows(w, ee)
        b = base_ref[ee * nw + w]

        def tail(k, carry, ee=ee, rows=rows, b=b):
            xstage[...] = jnp.dot(onehot(ee, k), h, preferred_element_type=F32).astype(xstage.dtype)

            def make(off, piece):
                return pltpu.make_async_copy(
                    xstage.at[pl.ds(off, piece)],
                    xe_hbm.at[ee, pl.ds(pl.multiple_of(b + k * MOE_CHUNK + off, MOE_ALIGN), piece)], xsem)
            n = jnp.minimum(rows - k * MOE_CHUNK, MOE_CHUNK)
            _for_segment_pieces(n, make, lambda c: c.start())
            _for_segment_pieces(n, make, lambda c: c.wait())
            return carry

        lax.fori_loop(1, (rows + MOE_CHUNK - 1) // MOE_CHUNK, tail, 0)

    @pl.when(w == nw - 1)
    def _():
        for ee in range(ne):
            used = base_ref[ee * nw + w] + seg_rows(w, ee)
            fill = _round_up(used, MOE_TILE) - used

            def zmake(off, piece, ee=ee, used=used):
                return pltpu.make_async_copy(zero_sc.at[pl.ds(0, piece)],
                                             xe_hbm.at[ee, pl.ds(pl.multiple_of(used + off, MOE_ALIGN), piece)], xsem)
            _for_segment_pieces(fill, zmake, lambda c: c.start())
            _for_segment_pieces(fill, zmake, lambda c: c.wait())
        for_heads(cur, w, lambda c: c.wait())
        for_heads(1 - cur, w - 1, lambda c: c.wait())


def dispatch_tokens(h, slot, base, cnt, rows_per_expert):
    N, D = h.shape
    E = slot.shape[0]
    nw = N // MOE_WIN
    assert nw >= 2 and E % MOE_GROUP == 0 and MOE_TILE <= _SEG_PIECES[0]
    return pl.pallas_call(
        functools.partial(_dispatch_kernel, nw=nw, ne=E),
        grid_spec=pltpu.PrefetchScalarGridSpec(
            num_scalar_prefetch=2,
            grid=(nw,),
            in_specs=[
                pl.BlockSpec((MOE_WIN, D), lambda w, *_: (w, 0)),
                pl.BlockSpec((E, MOE_WIN), lambda w, *_: (0, w)),
            ],
            out_specs=pl.BlockSpec(memory_space=pl.ANY),
            scratch_shapes=[
                pltpu.VMEM((2, E * MOE_CHUNK, D), BF16),
                pltpu.VMEM((MOE_CHUNK, D), BF16),
                pltpu.VMEM((MOE_TILE, D), BF16),
                pltpu.SemaphoreType.DMA((2,)),
                pltpu.SemaphoreType.DMA,
            ],
        ),
        out_shape=jax.ShapeDtypeStruct((E, rows_per_expert, D), BF16),
        compiler_params=_params(("arbitrary",)),
        name="dispatch_tokens",
    )(base.reshape(-1), cnt.reshape(-1), h, slot)


def _expert_kernel(used_ref, x_ref, wg_ref, wu_ref, wd_ref, o_ref):
    @pl.when(pl.program_id(1) * x_ref.shape[0] < used_ref[pl.program_id(0)])
    def _():
        x = x_ref[...]
        a = jnp.dot(x, wg_ref[...], preferred_element_type=F32)
        u = jnp.dot(x, wu_ref[...], preferred_element_type=F32)
        hmid = (a * jax.nn.sigmoid(a) * u).astype(BF16)
        o_ref[...] = jnp.dot(hmid, wd_ref[...], preferred_element_type=F32).astype(o_ref.dtype)


def expert_ffn(xe, used, w_gate, w_up, w_down):
    E, R, D = xe.shape
    F = w_gate.shape[2]
    return pl.pallas_call(
        _expert_kernel,
        grid_spec=pltpu.PrefetchScalarGridSpec(
            num_scalar_prefetch=1,
            grid=(E, R // MOE_TILE),
            in_specs=[
                pl.BlockSpec((None, MOE_TILE, D), lambda e, i, *_: (e, i, 0)),
                pl.BlockSpec((None, D, F), lambda e, i, *_: (e, 0, 0)),
                pl.BlockSpec((None, D, F), lambda e, i, *_: (e, 0, 0)),
                pl.BlockSpec((None, F, D), lambda e, i, *_: (e, 0, 0)),
            ],
            out_specs=pl.BlockSpec((None, MOE_TILE, D), lambda e, i, *_: (e, i, 0)),
        ),
        out_shape=jax.ShapeDtypeStruct((E, R, D), BF16),
        compiler_params=_params(("parallel", "arbitrary")),
        name="expert_ffn",
    )(used, xe, w_gate, w_up, w_down)


def _combine_kernel(base_ref, cnt_ref, lim_ref, x_ref, gt_ref, slot_ref, aff_ref, g_ref, ye_hbm, o_ref,
                    buf, xbuf, sems, xsem, *, nw, ne, final_norm):
    w = pl.program_id(0)
    cur = w % 2
    win = x_ref.shape[0]
    col = lax.broadcasted_iota(jnp.int32, (win, MOE_CHUNK), 1)

    def chunk_start(ww, ee, k):
        b = base_ref[ee * nw + ww]
        return pl.multiple_of(jnp.minimum(b + k * MOE_CHUNK, lim_ref[ee] - MOE_CHUNK), MOE_ALIGN)

    def head_copy(slot_id, ww, ee):
        return pltpu.make_async_copy(ye_hbm.at[ee, pl.ds(chunk_start(ww, ee, 0), MOE_CHUNK)],
                                     buf.at[slot_id, ee * MOE_CHUNK:(ee + 1) * MOE_CHUNK], sems.at[slot_id])

    @pl.when(w == 0)
    def _():
        for ee in range(ne):
            head_copy(cur, w, ee).start()

    @pl.when(w + 1 < nw)
    def _():
        for ee in range(ne):
            head_copy(1 - cur, w + 1, ee).start()

    def weights(ee, k):
        scol = slot_ref[:, ee:ee + 1]
        gate = aff_ref[:, ee:ee + 1]
        if not (isinstance(k, int) and k == 0):
            gate = jnp.where(scol >= base_ref[ee * nw + w] + k * MOE_CHUNK, gate, 0.0)
        return jnp.where(scol == chunk_start(w, ee, k) + col, gate, 0.0).astype(BF16)

    for ee in range(ne):
        head_copy(cur, w, ee).wait()
    o_ref[...] = jnp.dot(jnp.concatenate([weights(ee, 0) for ee in range(ne)], axis=1), buf[cur],
                         preferred_element_type=F32)

    for ee in range(ne):
        def extra(k, carry, ee=ee):
            cp = pltpu.make_async_copy(ye_hbm.at[ee, pl.ds(chunk_start(w, ee, k), MOE_CHUNK)], xbuf, xsem)
            cp.start()
            cp.wait()
            o_ref[...] += jnp.dot(weights(ee, k), xbuf[...], preferred_element_type=F32)
            return carry

        lax.fori_loop(1, (cnt_ref[ee * nw + w] + MOE_CHUNK - 1) // MOE_CHUNK, extra, 0)

    x = x_ref[...] + gt_ref[...] * o_ref[...]
    if final_norm:
        ms = jnp.mean(x * x, axis=-1, keepdims=True)
        x = x * lax.rsqrt(ms + EPS) * g_ref[...]
    o_ref[...] = x


def combine_experts(x, gt, slot_t, aff_t, ye, base, cnt, lim, g_final, seq, *, final_norm):
    N, D = x.shape
    E = slot_t.shape[1]
    nw = N // MOE_WIN
    per_seq = seq // MOE_WIN
    return pl.pallas_call(
        functools.partial(_combine_kernel, nw=nw, ne=E, final_norm=final_norm),
        grid_spec=pltpu.PrefetchScalarGridSpec(
            num_scalar_prefetch=3,
            grid=(nw,),
            in_specs=[
                pl.BlockSpec((MOE_WIN, D), lambda w, *_: (w, 0)),
                pl.BlockSpec((None, 1, D), lambda w, *_: (w // per_seq, 0, 0)),
                pl.BlockSpec((MOE_WIN, E), lambda w, *_: (w, 0)),
                pl.BlockSpec((MOE_WIN, E), lambda w, *_: (w, 0)),
                pl.BlockSpec((1, D), lambda w, *_: (0, 0)),
                pl.BlockSpec(memory_space=pl.ANY),
            ],
            out_specs=pl.BlockSpec((MOE_WIN, D), lambda w, *_: (w, 0)),
            scratch_shapes=[
                pltpu.VMEM((2, E * MOE_CHUNK, D), BF16),
                pltpu.VMEM((MOE_CHUNK, D), BF16),
                pltpu.SemaphoreType.DMA((2,)),
                pltpu.SemaphoreType.DMA,
            ],
        ),
        out_shape=jax.ShapeDtypeStruct((N, D), F32),
        compiler_params=_params(("arbitrary",)),
        name="combine_experts",
    )(base.reshape(-1), cnt.reshape(-1), lim, x, gt, slot_t, aff_t, g_final.reshape(1, D), ye)


def _prepare_layer(l, p, D):
    w_in = p['w_in'][l]
    W = N_HEADS * HEAD_DIM
    KW = GQA_KV_HEADS * HEAD_DIM
    qr = p['mla_w_uq'].shape[1]
    kvr = p['mla_w_ukv'].shape[1]
    o = 0
    seg = {}
    for name, width in (('a', 3 * W), ('cq', qr), ('ckv', kvr), ('kr', MLA_ROPE), ('gq', W), ('gk', KW), ('gv', KW),
                        ('gate', N_BRANCH * D)):
        seg[name] = w_in[:, o:o + width]
        o += width
    gq_cols = np.concatenate([np.arange(h * HEAD_DIM, (h + 1) * HEAD_DIM) for h in GQA_HEAD_ORDER])
    zeros = lambda n: jnp.zeros((D, n), w_in.dtype)
    gv = [p_ for g in range(GQA_KV_HEADS)
          for p_ in (seg['gv'][:, g * HEAD_DIM:(g + 1) * HEAD_DIM], zeros(LANE - HEAD_DIM))]
    w_mix = jnp.concatenate(
        [seg['a'], seg['cq'], seg['gq'][:, gq_cols], seg['ckv'],
         zeros(MLA_NOPE), seg['kr'], zeros(LANE - MLA_NOPE - MLA_ROPE), seg['gk']] + gv, axis=1).astype(BF16)

    w_uq = p['mla_w_uq'][l].reshape(qr, N_HEADS, MLA_NOPE + MLA_ROPE)
    wq = jnp.pad(w_uq, ((0, 0), (0, 0), (0, LANE - MLA_NOPE - MLA_ROPE))).reshape(qr, N_HEADS * LANE).astype(BF16)
    w_ukv = p['mla_w_ukv'][l].reshape(kvr, N_HEADS, MLA_NOPE + MLA_V)
    wkn = jnp.pad(w_ukv[:, :, :MLA_NOPE], ((0, 0), (0, 0), (0, LANE - MLA_NOPE))).reshape(kvr, N_HEADS * LANE)
    wv = jnp.pad(w_ukv[:, :, MLA_NOPE:], ((0, 0), (0, 0), (0, LANE - MLA_V))).reshape(kvr, N_HEADS * LANE)

    w_branch = p['w_branch'][l]
    w_branch = jnp.stack([w_branch[0], w_branch[1], w_branch[2][gq_cols]]).astype(BF16)
    return dict(
        w_mix=w_mix, w_gate_logits=seg['gate'].astype(BF16),
        wq=wq, wkn=wkn.astype(BF16), wv=wv.astype(BF16),
        w_branch=w_branch, w_out=p['w_out'][l].astype(BF16),
        w_router_t=p['w_router'][l].T.astype(BF16),
        w_exp_gate=p['w_exp_gate'][l].astype(BF16), w_exp_up=p['w_exp_up'][l].astype(BF16),
        w_exp_down=p['w_exp_down'][l].astype(BF16),
    )


def _trunk(x3, mod, p, layers, tables):
    B, T, D = x3.shape
    N = B * T
    x = x3.reshape(N, D)
    depth = len(layers)
    for l in range(depth):
        w = layers[l]
        sh1, sc1, gt1, sh2, sc2, gt2 = [mod[l][:, None, i * D:(i + 1) * D] for i in range(6)]
        mix = in_projection(x, p['g_mix'][l], sc1, sh1, w['w_mix'], T, tn=w['w_mix'].shape[1] // 2,
                            sigmoid_out=False, name="mix_projection")
        sgate = in_projection(x, p['g_mix'][l], sc1, sh1, w['w_gate_logits'], T, tn=_tile(N_BRANCH * D, 768),
                              sigmoid_out=True, name="gate_projection")
        o_a = neighborhood_attention(mix, tables['na_bias'][l], B, T)
        qm, km, vm = mla_prepare(mix, p['mla_g_q'][l], p['mla_g_kv'][l], w['wq'], w['wkn'], w['wv'],
                                 tables['mla_q'][T], tables['mla_k'][T], T)
        o_b = flash_attention(qm, km, vm, B, T, shared_k=False, name="mla_attention")
        qg, kg, vg = gqa_prepare(mix, p['gqa_g_q'][l], p['gqa_g_k'][l], tables['gqa_q'][T], tables['gqa_k'][T], T)
        o_c = flash_attention(qg, kg, vg, B, T, shared_k=True, name="gqa_attention")
        x = merge_branches(o_a, o_b, o_c, sgate, x, gt1, w['w_branch'], w['w_out'], T)

        h, aff = router(x, p['g_ffn'][l], sc2, sh2, w['w_router_t'], T)
        cap = EC_CAPACITY * N // N_EXPERTS
        slot, base, cnt = select_tokens(aff, cap)
        used = base[:, -1] + _round_up(cnt[:, -1], MOE_ALIGN)
        rows = _round_up(cap + MOE_ALIGN * (N // MOE_WIN), MOE_TILE)
        xe = dispatch_tokens(h, slot, base, cnt, rows)
        ye = expert_ffn(xe, used, w['w_exp_gate'], w['w_exp_up'], w['w_exp_down'])
        x = combine_experts(x, gt2, slot.T, aff.T, ye, base, cnt, _round_up(used, MOE_TILE), p['g_final'], T,
                            final_norm=l == depth - 1)
    return x.reshape(B, T, D)


def kernel(x_prompt, x_sample, c_prompt, c_sample, w_ada, b_ada, g_mix, g_ffn, w_in, na_rel_bias, mla_g_q, mla_w_uq,
           mla_g_kv, mla_w_ukv, gqa_g_q, gqa_g_k, w_branch, w_out, w_router, w_exp_gate, w_exp_up, w_exp_down,
           g_final):
    p = dict(g_mix=g_mix, g_ffn=g_ffn, w_in=w_in, mla_g_q=mla_g_q, mla_w_uq=mla_w_uq, mla_g_kv=mla_g_kv,
             mla_w_ukv=mla_w_ukv, gqa_g_q=gqa_g_q, gqa_g_k=gqa_g_k, w_branch=w_branch, w_out=w_out,
             w_router=w_router, w_exp_gate=w_exp_gate, w_exp_up=w_exp_up, w_exp_down=w_exp_down, g_final=g_final)
    depth, D = g_mix.shape
    bp, bs = c_prompt.shape[0], c_sample.shape[0]
    rows = -(-(bp + bs) // 8) * 8
    c_all = jnp.concatenate([c_prompt, c_sample, jnp.zeros((rows - bp - bs, D), F32)], axis=0)
    mod = ada_modulation(c_all, w_ada, b_ada)
    layers = [_prepare_layer(l, p, D) for l in range(depth)]
    tables = dict(na_bias=[_na_bias_table(na_rel_bias[l]) for l in range(depth)],
                  mla_q={}, mla_k={}, gqa_q={}, gqa_k={})
    for T in {x_prompt.shape[1], x_sample.shape[1]}:
        tables['mla_q'][T], tables['mla_k'][T] = _mla_tables(T)
        tables['gqa_q'][T], tables['gqa_k'][T] = _gqa_tables(T)
    y_prompt = _trunk(x_prompt, mod[:, :bp], p, layers, tables)
    y_sample = _trunk(x_sample, mod[:, bp:bp + bs], p, layers, tables)
    return (y_prompt, y_sample)
```

```python
import functools
import math

import numpy as np
import jax
import jax.numpy as jnp
from jax import lax
from jax.experimental import pallas as pl
from jax.experimental.pallas import tpu as pltpu

F32 = jnp.float32
BF16 = jnp.bfloat16

GRID_W = 64
HEAD_DIM = 64
N_HEADS = 8
NA_WIN_H = 8
NA_WIN_W = 16
MLA_NOPE = 64
MLA_ROPE = 32
MLA_V = 64
GQA_KV_HEADS = 2
N_BRANCH = 3
N_EXPERTS = 16
EC_CAPACITY = 2
ROPE_THETA = 10000.0
EPS = 1e-6
LOG2E = 1.4426950408889634

LANE = 128
NA_ROWS_PER_BLOCK = 4
NA_BLOCK = NA_ROWS_PER_BLOCK * GRID_W
NA_KEY_BLOCKS = 3
NEG = -1e30
VMEM_LIMIT = 56 * 1024 * 1024

GQA_HEAD_ORDER = (0, 4, 1, 5, 2, 6, 3, 7)


def _params(sem):
    return pltpu.CompilerParams(dimension_semantics=sem, vmem_limit_bytes=VMEM_LIMIT)


def _tile(n, pref):
    t = min(n, pref)
    while n % t:
        t //= 2
    return t


def _ada_kernel(c_ref, w_ref, b_ref, o_ref):
    c = c_ref[...]
    a = (c * jax.nn.sigmoid(c)).astype(BF16)
    o_ref[...] = jnp.dot(a, w_ref[...].astype(BF16), preferred_element_type=F32) + b_ref[...]


def ada_modulation(c, w_ada, b_ada):
    L, D, D6 = w_ada.shape
    R = c.shape[0]
    tn = _tile(D6, 1024)
    return pl.pallas_call(
        _ada_kernel,
        grid=(L, D6 // tn),
        in_specs=[
            pl.BlockSpec((R, D), lambda l, j: (0, 0)),
            pl.BlockSpec((None, D, tn), lambda l, j: (l, 0, j)),
            pl.BlockSpec((None, 1, tn), lambda l, j: (l, 0, j)),
        ],
        out_specs=pl.BlockSpec((None, R, tn), lambda l, j: (l, 0, j)),
        out_shape=jax.ShapeDtypeStruct((L, R, D6), F32),
        compiler_params=_params(("parallel", "parallel")),
        name="ada_modulation",
    )(c, w_ada, b_ada.reshape(L, 1, D6))


def _norm_mod(x, g, sc, sh):
    ms = jnp.mean(x * x, axis=-1, keepdims=True)
    y = x * lax.rsqrt(ms + EPS)
    return (y * g) * (1.0 + sc) + sh


def _in_proj_kernel(x_ref, g_ref, sc_ref, sh_ref, w_ref, o_ref, h_sc, *, sigmoid_out):
    @pl.when(pl.program_id(1) == 0)
    def _():
        h_sc[...] = _norm_mod(x_ref[...], g_ref[...], sc_ref[...], sh_ref[...]).astype(BF16)

    acc = jnp.dot(h_sc[...], w_ref[...], preferred_element_type=F32)
    if sigmoid_out:
        acc = jax.nn.sigmoid(acc)
    o_ref[...] = acc.astype(o_ref.dtype)


def in_projection(x, g, sc, sh, w, seq, *, tn, sigmoid_out, name):
    N, D = x.shape
    C = w.shape[1]
    tm = _tile(seq, 1024)
    per_seq = seq // tm
    return pl.pallas_call(
        functools.partial(_in_proj_kernel, sigmoid_out=sigmoid_out),
        grid=(N // tm, C // tn),
        in_specs=[
            pl.BlockSpec((tm, D), lambda i, j: (i, 0)),
            pl.BlockSpec((1, D), lambda i, j: (0, 0)),
            pl.BlockSpec((None, 1, D), lambda i, j: (i // per_seq, 0, 0)),
            pl.BlockSpec((None, 1, D), lambda i, j: (i // per_seq, 0, 0)),
            pl.BlockSpec((D, tn), lambda i, j: (0, j)),
        ],
        out_specs=pl.BlockSpec((tm, tn), lambda i, j: (i, j)),
        out_shape=jax.ShapeDtypeStruct((N, C), BF16),
        scratch_shapes=[pltpu.VMEM((tm, D), BF16)],
        compiler_params=_params(("parallel", "arbitrary")),
        name=name,
    )(x, g.reshape(1, D), sc, sh, w)


def _na_bias_table(rel_bias):
    R = NA_ROWS_PER_BLOCK
    KR = NA_KEY_BLOCKS * R
    q_rows = np.stack([np.arange(R), R + np.arange(R), 2 * R + np.arange(R)])
    w_start = np.stack([np.zeros(R, np.int64), np.arange(R), np.full(R, KR - NA_WIN_H)])
    col = np.arange(GRID_W)
    col_start = np.clip(col - NA_WIN_W // 2, 0, GRID_W - NA_WIN_W)
    k_row = np.arange(KR)
    k_col = np.arange(GRID_W)
    row_ok = (k_row[None, None, :] >= w_start[:, :, None]) & (k_row[None, None, :] < w_start[:, :, None] + NA_WIN_H)
    col_ok = (k_col[None, :] >= col_start[:, None]) & (k_col[None, :] < col_start[:, None] + NA_WIN_W)
    dr = k_row[None, None, :] - q_rows[:, :, None] + (NA_WIN_H - 1)
    dc = k_col[None, :] - col[:, None] + (NA_WIN_W - 1)
    H = rel_bias.shape[0]
    pick_c = ((dc[None] == np.arange(2 * NA_WIN_W - 1)[:, None, None]) & col_ok[None]).astype(np.float32)
    by_col = jnp.einsum('hab,bqk->haqk', rel_bias.astype(F32), pick_c, precision=lax.Precision.HIGHEST)
    by_col = jnp.where(col_ok, by_col, NEG)
    masked = jnp.full((H, GRID_W, GRID_W), NEG, F32)
    types = []
    for t in range(3):
        q_blocks = []
        for qr in range(R):
            q_blocks.append(jnp.concatenate(
                [by_col[:, int(dr[t, qr, kr])] if row_ok[t, qr, kr] else masked for kr in range(KR)], axis=2))
        types.append(jnp.concatenate(q_blocks, axis=1))
    return jnp.stack(types)


def _na_kernel(q_ref, k0_ref, k1_ref, k2_ref, v0_ref, v1_ref, v2_ref, bias_ref, o_ref):
    lane = lax.broadcasted_iota(jnp.int32, (NA_BLOCK, LANE), 1)
    lo = lane < HEAD_DIM
    scale = HEAD_DIM ** -0.5
    for u in range(N_HEADS // 2):
        cs = slice(u * LANE, (u + 1) * LANE)
        q = q_ref[:, cs] * scale
        ks = (k0_ref[:, cs], k1_ref[:, cs], k2_ref[:, cs])
        vs = (v0_ref[:, cs], v1_ref[:, cs], v2_ref[:, cs])
        outs = []
        for hh in range(2):
            qh = jnp.where(lo if hh == 0 else ~lo, q, jnp.zeros_like(q))
            s = jnp.concatenate(
                [lax.dot_general(qh, k, (((1,), (1,)), ((), ())), preferred_element_type=F32) for k in ks], axis=1)
            s = s + bias_ref[2 * u + hh]
            m = jnp.max(s, axis=1, keepdims=True)
            p = jnp.exp(s - m)
            l = jnp.sum(p, axis=1, keepdims=True)
            pb = p.astype(BF16)
            o = None
            for j in range(NA_KEY_BLOCKS):
                t = jnp.dot(pb[:, j * NA_BLOCK:(j + 1) * NA_BLOCK], vs[j], preferred_element_type=F32)
                o = t if o is None else o + t
            outs.append(o / l)
        o_ref[:, cs] = jnp.where(lo, outs[0], outs[1]).astype(o_ref.dtype)


def neighborhood_attention(mix, bias_table, batch, seq):
    N = mix.shape[0]
    W = N_HEADS * HEAD_DIM
    nb = seq // NA_BLOCK
    assert seq % NA_BLOCK == 0 and nb >= NA_KEY_BLOCKS

    def kv_spec(colblk, j):
        return pl.BlockSpec(
            (NA_BLOCK, W),
            lambda b, i: (b * nb + jnp.clip(i - 1, 0, nb - NA_KEY_BLOCKS) + j, colblk))

    def bias_idx(b, i):
        return (jnp.where(i == 0, 0, jnp.where(i == nb - 1, 2, 1)), 0, 0, 0)

    return pl.pallas_call(
        _na_kernel,
        grid=(batch, nb),
        in_specs=[pl.BlockSpec((NA_BLOCK, W), lambda b, i: (b * nb + i, 0))]
        + [kv_spec(1, j) for j in range(NA_KEY_BLOCKS)]
        + [kv_spec(2, j) for j in range(NA_KEY_BLOCKS)]
        + [pl.BlockSpec((None, N_HEADS, NA_BLOCK, NA_KEY_BLOCKS * NA_BLOCK), bias_idx)],
        out_specs=pl.BlockSpec((NA_BLOCK, W), lambda b, i: (b * nb + i, 0)),
        out_shape=jax.ShapeDtypeStruct((N, W), BF16),
        compiler_params=_params(("parallel", "arbitrary")),
        name="neighborhood_attention",
    )(mix, mix, mix, mix, mix, mix, mix, bias_table)


def _rope_angles(pos, dim):
    inv_freq = ROPE_THETA ** (-jnp.arange(0, dim, 2, dtype=F32) / dim)
    ang = pos.astype(F32)[:, None] * inv_freq[None, :]
    return jnp.concatenate([ang, ang], axis=-1)


def _rot_tables(cos, sin, scale):
    n = cos.shape[1]
    first = (np.arange(n) % 32) < 16
    sa = jnp.where(first[None, :], -sin, 0.0)
    sb = jnp.where(first[None, :], 0.0, sin)
    return cos * scale, sa * scale, sb * scale


def _mla_tables(seq):
    pos = jnp.arange(seq)
    ang = _rope_angles(pos, MLA_ROPE)
    cos, sin = jnp.cos(ang), jnp.sin(ang)
    pad_hi = LANE - MLA_NOPE - MLA_ROPE

    def place(t, nope_val):
        return jnp.concatenate(
            [jnp.full((seq, MLA_NOPE), nope_val, F32), t, jnp.zeros((seq, pad_hi), F32)], axis=1)

    qs = (MLA_NOPE + MLA_ROPE) ** -0.5 * LOG2E
    qc, qa, qb = _rot_tables(cos, sin, qs)
    kc, ka, kb = _rot_tables(cos, sin, 1.0)
    q_tab = jnp.stack([place(qc, qs), place(qa, 0.0), place(qb, 0.0)])
    k_tab = jnp.stack([place(kc, 0.0), place(ka, 0.0), place(kb, 0.0)])
    return q_tab, k_tab


def _gqa_tables(seq):
    pos = jnp.arange(seq)
    half = HEAD_DIM // 2
    ar = _rope_angles(pos // GRID_W, half)
    ac = _rope_angles(pos % GRID_W, half)
    cos = jnp.concatenate([jnp.cos(ar), jnp.cos(ac)] * 2, axis=1)
    sin = jnp.concatenate([jnp.sin(ar), jnp.sin(ac)] * 2, axis=1)
    qs = HEAD_DIM ** -0.5 * LOG2E
    return jnp.stack(_rot_tables(cos, sin, qs)), jnp.stack(_rot_tables(cos, sin, 1.0))


def _apply_rot(x, tab_ref):
    n = x.shape[1]
    return (x * tab_ref[0] + pltpu.roll(x, n - 16, 1) * tab_ref[1] + pltpu.roll(x, 16, 1) * tab_ref[2])


def _rms(x, g):
    ms = jnp.mean(x * x, axis=-1, keepdims=True)
    return x * lax.rsqrt(ms + EPS) * g


def _mla_prep_kernel(cq_ref, ckv_ref, kr_ref, gq_ref, gkv_ref, wq_ref, wkn_ref, wv_ref, qtab_ref, ktab_ref,
                     q_out, k_out, v_out):
    cqn = _rms(cq_ref[...].astype(F32), gq_ref[...]).astype(BF16)
    ckvn = _rms(ckv_ref[...].astype(F32), gkv_ref[...]).astype(BF16)
    qf = jnp.dot(cqn, wq_ref[...], preferred_element_type=F32)
    kn = jnp.dot(ckvn, wkn_ref[...], preferred_element_type=F32)
    tm = cq_ref.shape[0]
    hi_lane = lax.broadcasted_iota(jnp.int32, (tm, LANE), 1) >= MLA_V
    vf = jnp.dot(ckvn, wv_ref[...], preferred_element_type=F32)
    k_rope = _apply_rot(kr_ref[...].astype(F32), ktab_ref)
    for h in range(N_HEADS):
        cs = slice(h * LANE, (h + 1) * LANE)
        q_out[:, cs] = _apply_rot(qf[:, cs], qtab_ref).astype(q_out.dtype)
        k_out[:, cs] = (kn[:, cs] + k_rope).astype(k_out.dtype)
        v_out[:, cs] = jnp.where(hi_lane, 1.0, vf[:, cs]).astype(v_out.dtype)


def mla_prepare(mix, g_q, g_kv, wq, wkn, wv, q_tab, k_tab, seq):
    N = mix.shape[0]
    tm = _tile(seq, 512)
    per_seq = seq // tm
    qr, kvr = wq.shape[0], wkn.shape[0]
    cq_blk = 1536 // qr
    ckv_blk = 2560 // kvr
    kr_blk = 2816 // LANE
    const = lambda i: (0, 0)
    tab = pl.BlockSpec((3, tm, LANE), lambda i: (0, i % per_seq, 0))
    return pl.pallas_call(
        _mla_prep_kernel,
        grid=(N // tm,),
        in_specs=[
            pl.BlockSpec((tm, qr), lambda i: (i, cq_blk)),
            pl.BlockSpec((tm, kvr), lambda i: (i, ckv_blk)),
            pl.BlockSpec((tm, LANE), lambda i: (i, kr_blk)),
            pl.BlockSpec((1, qr), const),
            pl.BlockSpec((1, kvr), const),
            pl.BlockSpec(wq.shape, const),
            pl.BlockSpec(wkn.shape, const),
            pl.BlockSpec(wv.shape, const),
            tab, tab,
        ],
        out_specs=[
            pl.BlockSpec((tm, N_HEADS * LANE), lambda i: (i, 0)),
            pl.BlockSpec((tm, N_HEADS * LANE), lambda i: (i, 0)),
            pl.BlockSpec((tm, N_HEADS * LANE), lambda i: (i, 0)),
        ],
        out_shape=[jax.ShapeDtypeStruct((N, N_HEADS * LANE), BF16)] * 3,
        compiler_params=_params(("parallel",)),
        name="mla_prepare",
    )(mix, mix, mix, g_q.reshape(1, qr), g_kv.reshape(1, kvr), wq, wkn, wv, q_tab, k_tab)


def _head_rms(x, g, seg_ref):
    sq = x * x
    hi = sq.astype(BF16)
    lo = (sq - hi.astype(F32)).astype(BF16)
    ms = (jnp.dot(hi, seg_ref[...], preferred_element_type=F32)
          + jnp.dot(lo, seg_ref[...], preferred_element_type=F32)) * (1.0 / HEAD_DIM)
    return x * lax.rsqrt(ms + EPS) * g


def _gqa_prep_kernel(q_ref, k_ref, v_ref, gq_ref, gk_ref, segq_ref, segk_ref, qtab_ref, ktab_ref,
                     q_out, k_out, v_out):
    tm = q_ref.shape[0]
    lane = lax.broadcasted_iota(jnp.int32, (tm, LANE), 1)
    lo = lane < HEAD_DIM
    for g in range(GQA_KV_HEADS):
        cs = slice(g * LANE, (g + 1) * LANE)
        v_out[:, cs] = jnp.where(lo, v_ref[:, cs], jnp.ones((tm, LANE), v_ref.dtype))
    qn = _head_rms(q_ref[...].astype(F32), gq_ref[...], segq_ref)
    for u in range(N_HEADS // 2):
        y = _apply_rot(qn[:, u * LANE:(u + 1) * LANE], qtab_ref)
        q_out[:, (2 * u) * LANE:(2 * u + 1) * LANE] = jnp.where(lo, y, 0.0).astype(q_out.dtype)
        q_out[:, (2 * u + 1) * LANE:(2 * u + 2) * LANE] = jnp.where(lo, 0.0, y).astype(q_out.dtype)
    kn = _head_rms(k_ref[...].astype(F32), gk_ref[...], segk_ref)
    k_out[...] = _apply_rot(kn, ktab_ref).astype(k_out.dtype)


def gqa_prepare(mix, g_q, g_k, q_tab, k_tab, seq):
    N = mix.shape[0]
    tm = _tile(seq, 512)
    per_seq = seq // tm
    W = N_HEADS * HEAD_DIM
    KW = GQA_KV_HEADS * HEAD_DIM
    VW = GQA_KV_HEADS * LANE
    const = lambda i: (0, 0)
    seg = lambda n: jnp.asarray(np.kron(np.eye(n // HEAD_DIM), np.ones((HEAD_DIM, HEAD_DIM))), BF16)
    tab = pl.BlockSpec((3, tm, LANE), lambda i: (0, i % per_seq, 0))
    return pl.pallas_call(
        _gqa_prep_kernel,
        grid=(N // tm,),
        in_specs=[
            pl.BlockSpec((tm, W), lambda i: (i, 2048 // W)),
            pl.BlockSpec((tm, KW), lambda i: (i, 2944 // KW)),
            pl.BlockSpec((tm, VW), lambda i: (i, 3072 // VW)),
            pl.BlockSpec((1, W), const),
            pl.BlockSpec((1, KW), const),
            pl.BlockSpec((W, W), const),
            pl.BlockSpec((KW, KW), const),
            tab, tab,
        ],
        out_specs=[
            pl.BlockSpec((tm, N_HEADS * LANE), lambda i: (i, 0)),
            pl.BlockSpec((tm, KW), lambda i: (i, 0)),
            pl.BlockSpec((tm, VW), lambda i: (i, 0)),
        ],
        out_shape=[
            jax.ShapeDtypeStruct((N, N_HEADS * LANE), BF16),
            jax.ShapeDtypeStruct((N, KW), BF16),
            jax.ShapeDtypeStruct((N, VW), BF16),
        ],
        compiler_params=_params(("parallel",)),
        name="gqa_prepare",
    )(mix, mix, mix, jnp.tile(g_q, N_HEADS).reshape(1, W), jnp.tile(g_k, GQA_KV_HEADS).reshape(1, KW),
      seg(W), seg(KW), q_tab, k_tab)


def _flash_kernel(q_ref, k_ref, v_ref, o_ref, m_sc, acc_sc, *, shared_k, v_groups):
    kv = pl.program_id(2)
    tq = q_ref.shape[0]
    tk = k_ref.shape[0]

    @pl.when(kv == 0)
    def _():
        m_sc[...] = jnp.full(m_sc.shape, -jnp.inf, F32)
        acc_sc[...] = jnp.zeros(acc_sc.shape, F32)

    for h in range(N_HEADS):
        q = q_ref[:, h * LANE:(h + 1) * LANE]
        k = k_ref[...] if shared_k else k_ref[:, h * LANE:(h + 1) * LANE]
        g = h % v_groups
        v = v_ref[:, g * LANE:(g + 1) * LANE]
        s = lax.dot_general(q, k, (((1,), (1,)), ((), ())), preferred_element_type=F32)
        m_prev = m_sc[h]
        m_new = jnp.maximum(m_prev, jnp.max(s, axis=1, keepdims=True))
        alpha = jnp.exp2(m_prev - m_new)
        p = jnp.exp2(s - jnp.concatenate([m_new] * (tk // LANE), axis=1))
        acc_sc[h] = alpha * acc_sc[h] + jnp.dot(p.astype(BF16), v, preferred_element_type=F32)
        m_sc[h] = m_new

    @pl.when(kv == pl.num_programs(2) - 1)
    def _():
        lane = lax.broadcasted_iota(jnp.int32, (tq, LANE), 1)
        lo = lane < HEAD_DIM
        for u in range(N_HEADS // 2):
            a = acc_sc[2 * u]
            b = acc_sc[2 * u + 1]
            a = a / pltpu.roll(a, HEAD_DIM, 1)
            b = pltpu.roll(b, HEAD_DIM, 1) / b
            o_ref[:, u * LANE:(u + 1) * LANE] = jnp.where(lo, a, b).astype(o_ref.dtype)


def flash_attention(q, k, v, batch, seq, *, shared_k, name):
    N = q.shape[0]
    tq = _tile(seq, 1024)
    tk = _tile(seq, 2048)
    nq, nk = seq // tq, seq // tk
    kw, vw = k.shape[1], v.shape[1]
    ow = N_HEADS * HEAD_DIM
    return pl.pallas_call(
        functools.partial(_flash_kernel, shared_k=shared_k, v_groups=vw // LANE),
        grid=(batch, nq, nk),
        in_specs=[
            pl.BlockSpec((tq, N_HEADS * LANE), lambda b, i, j: (b * nq + i, 0)),
            pl.BlockSpec((tk, kw), lambda b, i, j: (b * nk + j, 0)),
            pl.BlockSpec((tk, vw), lambda b, i, j: (b * nk + j, 0)),
        ],
        out_specs=pl.BlockSpec((tq, ow), lambda b, i, j: (b * nq + i, 0)),
        out_shape=jax.ShapeDtypeStruct((N, ow), BF16),
        scratch_shapes=[pltpu.VMEM((N_HEADS, tq, LANE), F32)] * 2,
        compiler_params=_params(("parallel", "parallel", "arbitrary")),
        name=name,
    )(q, k, v)


def _merge_kernel(oa_ref, ob_ref, oc_ref, sg_ref, x_ref, gt_ref, wb_ref, wo_ref, o_ref):
    D = x_ref.shape[1]
    merged = None
    for i, o_r in enumerate((oa_ref, ob_ref, oc_ref)):
        t = jnp.dot(o_r[...], wb_ref[i], preferred_element_type=F32)
        t = sg_ref[:, i * D:(i + 1) * D].astype(F32) * t
        merged = t if merged is None else merged + t
    mix = jnp.dot(merged.astype(BF16), wo_ref[...], preferred_element_type=F32)
    o_ref[...] = x_ref[...] + gt_ref[...] * mix


def merge_branches(o_a, o_b, o_c, sgate, x, gt, w_branch, w_out, seq):
    N, D = x.shape
    W = o_a.shape[1]
    tm = _tile(seq, 256)
    per_seq = seq // tm
    row = lambda i: (i, 0)
    return pl.pallas_call(
        _merge_kernel,
        grid=(N // tm,),
        in_specs=[
            pl.BlockSpec((tm, W), row), pl.BlockSpec((tm, W), row), pl.BlockSpec((tm, W), row),
            pl.BlockSpec((tm, N_BRANCH * D), row),
            pl.BlockSpec((tm, D), row),
            pl.BlockSpec((None, 1, D), lambda i: (i // per_seq, 0, 0)),
            pl.BlockSpec((N_BRANCH, W, D), lambda i: (0, 0, 0)),
            pl.BlockSpec((D, D), lambda i: (0, 0)),
        ],
        out_specs=pl.BlockSpec((tm, D), row),
        out_shape=jax.ShapeDtypeStruct((N, D), F32),
        compiler_params=_params(("parallel",)),
        name="merge_branches",
    )(o_a, o_b, o_c, sgate, x, gt, w_branch, w_out)


def _router_kernel(x_ref, g_ref, sc_ref, sh_ref, wr_ref, h_out, aff_out):
    h = _norm_mod(x_ref[...], g_ref[...], sc_ref[...], sh_ref[...]).astype(BF16)
    h_out[...] = h
    logits = lax.dot_general(wr_ref[...], h, (((1,), (1,)), ((), ())), preferred_element_type=F32)
    m = jnp.max(logits, axis=0, keepdims=True)
    p = jnp.exp(logits - m)
    aff_out[...] = p / jnp.sum(p, axis=0, keepdims=True)


def router(x, g, sc, sh, w_router_t, seq):
    N, D = x.shape
    E = w_router_t.shape[0]
    tm = _tile(seq, 512)
    per_seq = seq // tm
    return pl.pallas_call(
        _router_kernel,
        grid=(N // tm,),
        in_specs=[
            pl.BlockSpec((tm, D), lambda i: (i, 0)),
            pl.BlockSpec((1, D), lambda i: (0, 0)),
            pl.BlockSpec((None, 1, D), lambda i: (i // per_seq, 0, 0)),
            pl.BlockSpec((None, 1, D), lambda i: (i // per_seq, 0, 0)),
            pl.BlockSpec((E, D), lambda i: (0, 0)),
        ],
        out_specs=[pl.BlockSpec((tm, D), lambda i: (i, 0)), pl.BlockSpec((E, tm), lambda i: (0, i))],
        out_shape=[jax.ShapeDtypeStruct((N, D), BF16), jax.ShapeDtypeStruct((E, N), F32)],
        compiler_params=_params(("parallel",)),
        name="router",
    )(x, g.reshape(1, D), sc, sh, w_router_t)


MOE_WIN = 512
MOE_ALIGN = 16
MOE_CHUNK = 128
MOE_TILE = 512
_SEG_PIECES = (512, 256, 128, 64, 32, 16)


def _round_up(x, m):
    return (x + (m - 1)) // m * m


def _select_kernel(aff_ref, slot_ref, wbase_ref, wcnt_ref, *, cap, chunks_per_win):
    a = aff_ref[...]
    nc = a.shape[0]
    bits = pltpu.bitcast(a, jnp.int32)

    def search(i, t):
        cand = t | jnp.left_shift(jnp.int32(1), 30 - i)
        n_ge = jnp.sum((bits >= cand).astype(jnp.int32))
        return jnp.where(n_ge >= cap, cand, t)

    thr = lax.fori_loop(0, 31, search, jnp.int32(0))
    gt = bits > thr
    eq = bits == thr
    need = (cap - jnp.sum(gt.astype(jnp.int32))).astype(F32)

    li = lax.broadcasted_iota(jnp.int32, (LANE, LANE), 0)
    lj = lax.broadcasted_iota(jnp.int32, (LANE, LANE), 1)
    tri = jnp.where(li <= lj, 1.0, 0.0).astype(BF16)
    ones = jnp.ones((LANE, LANE), BF16)
    ci = lax.broadcasted_iota(jnp.int32, (nc, nc), 0)
    cj = lax.broadcasted_iota(jnp.int32, (nc, nc), 1)
    shift = chunks_per_win.bit_length() - 1
    wi, wj = jnp.right_shift(ci, shift), jnp.right_shift(cj, shift)
    as_bf = lambda m: jnp.where(m, 1.0, 0.0).astype(BF16)
    dot = lambda x, y: jnp.dot(x, y, preferred_element_type=F32)

    eq_b = as_bf(eq)
    eq_rank = dot(as_bf(cj < ci), dot(eq_b, ones).astype(BF16)) + dot(eq_b, tri) - eq_b.astype(F32)
    sel = gt | (eq & (eq_rank < need))
    sel_b = as_bf(sel)
    incl = dot(sel_b, tri)
    tot = dot(sel_b, ones).astype(BF16)
    in_win = dot(as_bf((cj < ci) & (wj == wi)), tot)
    wcnt = dot(as_bf(wj == wi), tot)
    wpad = jnp.ceil(wcnt * (1.0 / MOE_ALIGN)) * MOE_ALIGN
    first = (cj & (chunks_per_win - 1)) == 0
    wbase = dot(as_bf((wj < wi) & first), wpad.astype(BF16))
    slot = wbase + in_win + incl - sel_b.astype(F32)
    slot_ref[...] = jnp.where(sel, slot, -1.0).astype(jnp.int32)
    wbase_ref[...] = wbase.astype(jnp.int32)
    wcnt_ref[...] = wcnt.astype(jnp.int32)


def select_tokens(aff, cap):
    E, N = aff.shape
    nc = N // LANE
    cpw = MOE_WIN // LANE
    blk = pl.BlockSpec((None, nc, LANE), lambda e: (e, 0, 0))
    out = jax.ShapeDtypeStruct((E, nc, LANE), jnp.int32)
    slot, wbase, wcnt = pl.pallas_call(
        functools.partial(_select_kernel, cap=cap, chunks_per_win=cpw),
        grid=(E,),
        in_specs=[blk],
        out_specs=[blk, blk, blk],
        out_shape=[out, out, out],
        compiler_params=_params(("parallel",)),
        name="select_tokens",
    )(aff.reshape(E, nc, LANE))
    return slot.reshape(E, N), wbase[:, ::cpw, 0], wcnt[:, ::cpw, 0]


def _for_segment_pieces(n_rows, make_copy, act):
    off = jnp.int32(0)
    for piece in _SEG_PIECES:
        has = (n_rows & piece) != 0

        @pl.when(has)
        def _(off=off, piece=piece):
            act(make_copy(pl.multiple_of(off, MOE_ALIGN), piece))

        off = off + jnp.where(has, piece, 0)


MOE_GROUP = 4


def _dispatch_kernel(base_ref, cnt_ref, h_ref, slot_ref, xe_hbm, stage, xstage, zero_sc, sems, xsem, *, nw, ne):
    w = pl.program_id(0)
    cur = w % 2
    win = h_ref.shape[0]
    row_id = lax.broadcasted_iota(jnp.int32, (MOE_CHUNK, win), 0)

    def seg_rows(ww, ee):
        return _round_up(cnt_ref[ee * nw + ww], MOE_ALIGN)

    def head_copy(buf, ww, ee):
        b = base_ref[ee * nw + ww]

        def make(off, piece):
            return pltpu.make_async_copy(stage.at[buf, pl.ds(pl.multiple_of(ee * MOE_CHUNK + off, MOE_ALIGN), piece)],
                                         xe_hbm.at[ee, pl.ds(pl.multiple_of(b + off, MOE_ALIGN), piece)],
                                         sems.at[buf])
        return make

    def for_heads(buf, ww, act):
        for ee in range(ne):
            _for_segment_pieces(jnp.minimum(seg_rows(ww, ee), MOE_CHUNK), head_copy(buf, ww, ee), act)

    @pl.when(w == 0)
    def _():
        zero_sc[...] = jnp.zeros(zero_sc.shape, zero_sc.dtype)

    @pl.when(w >= 2)
    def _():
        for_heads(cur, w - 2, lambda c: c.wait())

    h = h_ref[...]

    def onehot(ee, k):
        want = base_ref[ee * nw + w] + k * MOE_CHUNK + row_id
        return jnp.where(slot_ref[ee:ee + 1, :] == want, 1.0, 0.0).astype(BF16)

    for g in range(ne // MOE_GROUP):
        sel = jnp.concatenate([onehot(g * MOE_GROUP + i, 0) for i in range(MOE_GROUP)], axis=0)
        stage[cur, g * MOE_GROUP * MOE_CHUNK:(g + 1) * MOE_GROUP * MOE_CHUNK, :] = jnp.dot(
            sel, h, preferred_element_type=F32).astype(stage.dtype)
    for_heads(cur, w, lambda c: c.start())

    for ee in range(ne):
        rows = seg_rows(w, ee)
        b = base_ref[ee * nw + w]

        def tail(k, carry, ee=ee, rows=rows, b=b):
            xstage[...] = jnp.dot(onehot(ee, k), h, preferred_element_type=F32).astype(xstage.dtype)

            def make(off, piece):
                return pltpu.make_async_copy(
                    xstage.at[pl.ds(off, piece)],
                    xe_hbm.at[ee, pl.ds(pl.multiple_of(b + k * MOE_CHUNK + off, MOE_ALIGN), piece)], xsem)
            n = jnp.minimum(rows - k * MOE_CHUNK, MOE_CHUNK)
            _for_segment_pieces(n, make, lambda c: c.start())
            _for_segment_pieces(n, make, lambda c: c.wait())
            return carry

        lax.fori_loop(1, (rows + MOE_CHUNK - 1) // MOE_CHUNK, tail, 0)

    @pl.when(w == nw - 1)
    def _():
        for ee in range(ne):
            used = base_ref[ee * nw + w] + seg_rows(w, ee)
            fill = _round_up(used, MOE_TILE) - used

            def zmake(off, piece, ee=ee, used=used):
                return pltpu.make_async_copy(zero_sc.at[pl.ds(0, piece)],
                                             xe_hbm.at[ee, pl.ds(pl.multiple_of(used + off, MOE_ALIGN), piece)], xsem)
            _for_segment_pieces(fill, zmake, lambda c: c.start())
            _for_segment_pieces(fill, zmake, lambda c: c.wait())
        for_heads(cur, w, lambda c: c.wait())
        for_heads(1 - cur, w - 1, lambda c: c.wait())


def dispatch_tokens(h, slot, base, cnt, rows_per_expert):
    N, D = h.shape
    E = slot.shape[0]
    nw = N // MOE_WIN
    assert nw >= 2 and E % MOE_GROUP == 0 and MOE_TILE <= _SEG_PIECES[0]
    return pl.pallas_call(
        functools.partial(_dispatch_kernel, nw=nw, ne=E),
        grid_spec=pltpu.PrefetchScalarGridSpec(
            num_scalar_prefetch=2,
            grid=(nw,),
            in_specs=[
                pl.BlockSpec((MOE_WIN, D), lambda w, *_: (w, 0)),
                pl.BlockSpec((E, MOE_WIN), lambda w, *_: (0, w)),
            ],
            out_specs=pl.BlockSpec(memory_space=pl.ANY),
            scratch_shapes=[
                pltpu.VMEM((2, E * MOE_CHUNK, D), BF16),
                pltpu.VMEM((MOE_CHUNK, D), BF16),
                pltpu.VMEM((MOE_TILE, D), BF16),
                pltpu.SemaphoreType.DMA((2,)),
                pltpu.SemaphoreType.DMA,
            ],
        ),
        out_shape=jax.ShapeDtypeStruct((E, rows_per_expert, D), BF16),
        compiler_params=_params(("arbitrary",)),
        name="dispatch_tokens",
    )(base.reshape(-1), cnt.reshape(-1), h, slot)


def _expert_kernel(used_ref, x_ref, wg_ref, wu_ref, wd_ref, o_ref):
    @pl.when(pl.program_id(1) * x_ref.shape[0] < used_ref[pl.program_id(0)])
    def _():
        x = x_ref[...]
        a = jnp.dot(x, wg_ref[...], preferred_element_type=F32)
        u = jnp.dot(x, wu_ref[...], preferred_element_type=F32)
        hmid = (a * jax.nn.sigmoid(a) * u).astype(BF16)
        o_ref[...] = jnp.dot(hmid, wd_ref[...], preferred_element_type=F32).astype(o_ref.dtype)


def expert_ffn(xe, used, w_gate, w_up, w_down):
    E, R, D = xe.shape
    F = w_gate.shape[2]
    return pl.pallas_call(
        _expert_kernel,
        grid_spec=pltpu.PrefetchScalarGridSpec(
            num_scalar_prefetch=1,
            grid=(E, R // MOE_TILE),
            in_specs=[
                pl.BlockSpec((None, MOE_TILE, D), lambda e, i, *_: (e, i, 0)),
                pl.BlockSpec((None, D, F), lambda e, i, *_: (e, 0, 0)),
                pl.BlockSpec((None, D, F), lambda e, i, *_: (e, 0, 0)),
                pl.BlockSpec((None, F, D), lambda e, i, *_: (e, 0, 0)),
            ],
            out_specs=pl.BlockSpec((None, MOE_TILE, D), lambda e, i, *_: (e, i, 0)),
        ),
        out_shape=jax.ShapeDtypeStruct((E, R, D), BF16),
        compiler_params=_params(("parallel", "arbitrary")),
        name="expert_ffn",
    )(used, xe, w_gate, w_up, w_down)


def _combine_kernel(base_ref, cnt_ref, lim_ref, x_ref, gt_ref, slot_ref, aff_ref, g_ref, ye_hbm, o_ref,
                    buf, xbuf, sems, xsem, *, nw, ne, final_norm):
    w = pl.program_id(0)
    cur = w % 2
    win = x_ref.shape[0]
    col = lax.broadcasted_iota(jnp.int32, (win, MOE_CHUNK), 1)

    def chunk_start(ww, ee, k):
        b = base_ref[ee * nw + ww]
        return pl.multiple_of(jnp.minimum(b + k * MOE_CHUNK, lim_ref[ee] - MOE_CHUNK), MOE_ALIGN)

    def head_copy(slot_id, ww, ee):
        return pltpu.make_async_copy(ye_hbm.at[ee, pl.ds(chunk_start(ww, ee, 0), MOE_CHUNK)],
                                     buf.at[slot_id, ee * MOE_CHUNK:(ee + 1) * MOE_CHUNK], sems.at[slot_id])

    @pl.when(w == 0)
    def _():
        for ee in range(ne):
            head_copy(cur, w, ee).start()

    @pl.when(w + 1 < nw)
    def _():
        for ee in range(ne):
            head_copy(1 - cur, w + 1, ee).start()

    def weights(ee, k):
        scol = slot_ref[:, ee:ee + 1]
        gate = aff_ref[:, ee:ee + 1]
        if not (isinstance(k, int) and k == 0):
            gate = jnp.where(scol >= base_ref[ee * nw + w] + k * MOE_CHUNK, gate, 0.0)
        return jnp.where(scol == chunk_start(w, ee, k) + col, gate, 0.0).astype(BF16)

    for ee in range(ne):
        head_copy(cur, w, ee).wait()
    o_ref[...] = jnp.dot(jnp.concatenate([weights(ee, 0) for ee in range(ne)], axis=1), buf[cur],
                         preferred_element_type=F32)

    for ee in range(ne):
        def extra(k, carry, ee=ee):
            cp = pltpu.make_async_copy(ye_hbm.at[ee, pl.ds(chunk_start(w, ee, k), MOE_CHUNK)], xbuf, xsem)
            cp.start()
            cp.wait()
            o_ref[...] += jnp.dot(weights(ee, k), xbuf[...], preferred_element_type=F32)
            return carry

        lax.fori_loop(1, (cnt_ref[ee * nw + w] + MOE_CHUNK - 1) // MOE_CHUNK, extra, 0)

    x = x_ref[...] + gt_ref[...] * o_ref[...]
    if final_norm:
        ms = jnp.mean(x * x, axis=-1, keepdims=True)
        x = x * lax.rsqrt(ms + EPS) * g_ref[...]
    o_ref[...] = x


def combine_experts(x, gt, slot_t, aff_t, ye, base, cnt, lim, g_final, seq, *, final_norm):
    N, D = x.shape
    E = slot_t.shape[1]
    nw = N // MOE_WIN
    per_seq = seq // MOE_WIN
    return pl.pallas_call(
        functools.partial(_combine_kernel, nw=nw, ne=E, final_norm=final_norm),
        grid_spec=pltpu.PrefetchScalarGridSpec(
            num_scalar_prefetch=3,
            grid=(nw,),
            in_specs=[
                pl.BlockSpec((MOE_WIN, D), lambda w, *_: (w, 0)),
                pl.BlockSpec((None, 1, D), lambda w, *_: (w // per_seq, 0, 0)),
                pl.BlockSpec((MOE_WIN, E), lambda w, *_: (w, 0)),
                pl.BlockSpec((MOE_WIN, E), lambda w, *_: (w, 0)),
                pl.BlockSpec((1, D), lambda w, *_: (0, 0)),
                pl.BlockSpec(memory_space=pl.ANY),
            ],
            out_specs=pl.BlockSpec((MOE_WIN, D), lambda w, *_: (w, 0)),
            scratch_shapes=[
                pltpu.VMEM((2, E * MOE_CHUNK, D), BF16),
                pltpu.VMEM((MOE_CHUNK, D), BF16),
                pltpu.SemaphoreType.DMA((2,)),
                pltpu.SemaphoreType.DMA,
            ],
        ),
        out_shape=jax.ShapeDtypeStruct((N, D), F32),
        compiler_params=_params(("arbitrary",)),
        name="combine_experts",
    )(base.reshape(-1), cnt.reshape(-1), lim, x, gt, slot_t, aff_t, g_final.reshape(1, D), ye)


def _prepare_layer(l, p, D):
    w_in = p['w_in'][l]
    W = N_HEADS * HEAD_DIM
    KW = GQA_KV_HEADS * HEAD_DIM
    qr = p['mla_w_uq'].shape[1]
    kvr = p['mla_w_ukv'].shape[1]
    o = 0
    seg = {}
    for name, width in (('a', 3 * W), ('cq', qr), ('ckv', kvr), ('kr', MLA_ROPE), ('gq', W), ('gk', KW), ('gv', KW),
                        ('gate', N_BRANCH * D)):
        seg[name] = w_in[:, o:o + width]
        o += width
    gq_cols = np.concatenate([np.arange(h * HEAD_DIM, (h + 1) * HEAD_DIM) for h in GQA_HEAD_ORDER])
    zeros = lambda n: jnp.zeros((D, n), w_in.dtype)
    gv = [p_ for g in range(GQA_KV_HEADS)
          for p_ in (seg['gv'][:, g * HEAD_DIM:(g + 1) * HEAD_DIM], zeros(LANE - HEAD_DIM))]
    w_mix = jnp.concatenate(
        [seg['a'], seg['cq'], seg['gq'][:, gq_cols], seg['ckv'],
         zeros(MLA_NOPE), seg['kr'], zeros(LANE - MLA_NOPE - MLA_ROPE), seg['gk']] + gv, axis=1).astype(BF16)

    w_uq = p['mla_w_uq'][l].reshape(qr, N_HEADS, MLA_NOPE + MLA_ROPE)
    wq = jnp.pad(w_uq, ((0, 0), (0, 0), (0, LANE - MLA_NOPE - MLA_ROPE))).reshape(qr, N_HEADS * LANE).astype(BF16)
    w_ukv = p['mla_w_ukv'][l].reshape(kvr, N_HEADS, MLA_NOPE + MLA_V)
    wkn = jnp.pad(w_ukv[:, :, :MLA_NOPE], ((0, 0), (0, 0), (0, LANE - MLA_NOPE))).reshape(kvr, N_HEADS * LANE)
    wv = jnp.pad(w_ukv[:, :, MLA_NOPE:], ((0, 0), (0, 0), (0, LANE - MLA_V))).reshape(kvr, N_HEADS * LANE)

    w_branch = p['w_branch'][l]
    w_branch = jnp.stack([w_branch[0], w_branch[1], w_branch[2][gq_cols]]).astype(BF16)
    return dict(
        w_mix=w_mix, w_gate_logits=seg['gate'].astype(BF16),
        wq=wq, wkn=wkn.astype(BF16), wv=wv.astype(BF16),
        w_branch=w_branch, w_out=p['w_out'][l].astype(BF16),
        w_router_t=p['w_router'][l].T.astype(BF16),
        w_exp_gate=p['w_exp_gate'][l].astype(BF16), w_exp_up=p['w_exp_up'][l].astype(BF16),
        w_exp_down=p['w_exp_down'][l].astype(BF16),
    )


def _trunk(x3, mod, p, layers, tables):
    B, T, D = x3.shape
    N = B * T
    x = x3.reshape(N, D)
    depth = len(layers)
    for l in range(depth):
        w = layers[l]
        sh1, sc1, gt1, sh2, sc2, gt2 = [mod[l][:, None, i * D:(i + 1) * D] for i in range(6)]
        mix = in_projection(x, p['g_mix'][l], sc1, sh1, w['w_mix'], T, tn=w['w_mix'].shape[1] // 2,
                            sigmoid_out=False, name="mix_projection")
        sgate = in_projection(x, p['g_mix'][l], sc1, sh1, w['w_gate_logits'], T, tn=_tile(N_BRANCH * D, 768),
                              sigmoid_out=True, name="gate_projection")
        o_a = neighborhood_attention(mix, tables['na_bias'][l], B, T)
        qm, km, vm = mla_prepare(mix, p['mla_g_q'][l], p['mla_g_kv'][l], w['wq'], w['wkn'], w['wv'],
                                 tables['mla_q'][T], tables['mla_k'][T], T)
        o_b = flash_attention(qm, km, vm, B, T, shared_k=False, name="mla_attention")
        qg, kg, vg = gqa_prepare(mix, p['gqa_g_q'][l], p['gqa_g_k'][l], tables['gqa_q'][T], tables['gqa_k'][T], T)
        o_c = flash_attention(qg, kg, vg, B, T, shared_k=True, name="gqa_attention")
        x = merge_branches(o_a, o_b, o_c, sgate, x, gt1, w['w_branch'], w['w_out'], T)

        h, aff = router(x, p['g_ffn'][l], sc2, sh2, w['w_router_t'], T)
        cap = EC_CAPACITY * N // N_EXPERTS
        slot, base, cnt = select_tokens(aff, cap)
        used = base[:, -1] + _round_up(cnt[:, -1], MOE_ALIGN)
        rows = _round_up(cap + MOE_ALIGN * (N // MOE_WIN), MOE_TILE)
        xe = dispatch_tokens(h, slot, base, cnt, rows)
        ye = expert_ffn(xe, used, w['w_exp_gate'], w['w_exp_up'], w['w_exp_down'])
        x = combine_experts(x, gt2, slot.T, aff.T, ye, base, cnt, _round_up(used, MOE_TILE), p['g_final'], T,
                            final_norm=l == depth - 1)
    return x.reshape(B, T, D)


def kernel(x_prompt, x_sample, c_prompt, c_sample, w_ada, b_ada, g_mix, g_ffn, w_in, na_rel_bias, mla_g_q, mla_w_uq,
           mla_g_kv, mla_w_ukv, gqa_g_q, gqa_g_k, w_branch, w_out, w_router, w_exp_gate, w_exp_up, w_exp_down,
           g_final):
    p = dict(g_mix=g_mix, g_ffn=g_ffn, w_in=w_in, mla_g_q=mla_g_q, mla_w_uq=mla_w_uq, mla_g_kv=mla_g_kv,
             mla_w_ukv=mla_w_ukv, gqa_g_q=gqa_g_q, gqa_g_k=gqa_g_k, w_branch=w_branch, w_out=w_out,
             w_router=w_router, w_exp_gate=w_exp_gate, w_exp_up=w_exp_up, w_exp_down=w_exp_down, g_final=g_final)
    depth, D = g_mix.shape
    bp, bs = c_prompt.shape[0], c_sample.shape[0]
    rows = -(-(bp + bs) // 8) * 8
    c_all = jnp.concatenate([c_prompt, c_sample, jnp.zeros((rows - bp - bs, D), F32)], axis=0)
    mod = ada_modulation(c_all, w_ada, b_ada)
    layers = [_prepare_layer(l, p, D) for l in range(depth)]
    tables = dict(na_bias=[_na_bias_table(na_rel_bias[l]) for l in range(depth)],
                  mla_q={}, mla_k={}, gqa_q={}, gqa_k={})
    for T in {x_prompt.shape[1], x_sample.shape[1]}:
        tables['mla_q'][T], tables['mla_k'][T] = _mla_tables(T)
        tables['gqa_q'][T], tables['gqa_k'][T] = _gqa_tables(T)
    y_prompt = _trunk(x_prompt, mod[:, :bp], p, layers, tables)
    y_sample = _trunk(x_sample, mod[:, bp:bp + bs], p, layers, tables)
    return (y_prompt, y_sample)
```

```python
import functools
import math

import numpy as np
import jax
import jax.numpy as jnp
from jax import lax
from jax.experimental import pallas as pl
from jax.experimental.pallas import tpu as pltpu

F32 = jnp.float32
BF16 = jnp.bfloat16

GRID_W = 64
HEAD_DIM = 64
N_HEADS = 8
NA_WIN_H = 8
NA_WIN_W = 16
MLA_NOPE = 64
MLA_ROPE = 32
MLA_V = 64
GQA_KV_HEADS = 2
N_BRANCH = 3
N_EXPERTS = 16
EC_CAPACITY = 2
ROPE_THETA = 10000.0
EPS = 1e-6
LOG2E = 1.4426950408889634

LANE = 128
NA_ROWS_PER_BLOCK = 4
NA_BLOCK = NA_ROWS_PER_BLOCK * GRID_W
NA_KEY_BLOCKS = 3
NEG = -1e30
VMEM_LIMIT = 56 * 1024 * 1024

GQA_HEAD_ORDER = (0, 4, 1, 5, 2, 6, 3, 7)


def _params(sem):
    return pltpu.CompilerParams(dimension_semantics=sem, vmem_limit_bytes=VMEM_LIMIT)


def _tile(n, pref):
    t = min(n, pref)
    while n % t:
        t //= 2
    return t


def _ada_kernel(c_ref, w_ref, b_ref, o_ref):
    c = c_ref[...]
    a = (c * jax.nn.sigmoid(c)).astype(BF16)
    o_ref[...] = jnp.dot(a, w_ref[...].astype(BF16), preferred_element_type=F32) + b_ref[...]


def ada_modulation(c, w_ada, b_ada):
    L, D, D6 = w_ada.shape
    R = c.shape[0]
    tn = _tile(D6, 1024)
    return pl.pallas_call(
        _ada_kernel,
        grid=(L, D6 // tn),
        in_specs=[
            pl.BlockSpec((R, D), lambda l, j: (0, 0)),
            pl.BlockSpec((None, D, tn), lambda l, j: (l, 0, j)),
            pl.BlockSpec((None, 1, tn), lambda l, j: (l, 0, j)),
        ],
        out_specs=pl.BlockSpec((None, R, tn), lambda l, j: (l, 0, j)),
        out_shape=jax.ShapeDtypeStruct((L, R, D6), F32),
        compiler_params=_params(("parallel", "parallel")),
        name="ada_modulation",
    )(c, w_ada, b_ada.reshape(L, 1, D6))


def _norm_mod(x, g, sc, sh):
    ms = jnp.mean(x * x, axis=-1, keepdims=True)
    y = x * lax.rsqrt(ms + EPS)
    return (y * g) * (1.0 + sc) + sh


def _norm_mod_kernel(x_ref, g_ref, sc_ref, sh_ref, h_ref):
    h_ref[...] = _norm_mod(x_ref[...], g_ref[...], sc_ref[...], sh_ref[...]).astype(h_ref.dtype)


def norm_modulate(x, g, sc, sh, seq):
    N, D = x.shape
    tm = _tile(seq, 512)
    per_seq = seq // tm
    return pl.pallas_call(
        _norm_mod_kernel,
        grid=(N // tm,),
        in_specs=[
            pl.BlockSpec((tm, D), lambda i: (i, 0)),
            pl.BlockSpec((1, D), lambda i: (0, 0)),
            pl.BlockSpec((None, 1, D), lambda i: (i // per_seq, 0, 0)),
            pl.BlockSpec((None, 1, D), lambda i: (i // per_seq, 0, 0)),
        ],
        out_specs=pl.BlockSpec((tm, D), lambda i: (i, 0)),
        out_shape=jax.ShapeDtypeStruct((N, D), BF16),
        compiler_params=_params(("parallel",)),
        name="norm_modulate",
    )(x, g.reshape(1, D), sc, sh)


def _proj_kernel(h_ref, w_ref, o_ref, *, sigmoid_out):
    acc = jnp.dot(h_ref[...], w_ref[...], preferred_element_type=F32)
    if sigmoid_out:
        acc = jax.nn.sigmoid(acc)
    o_ref[...] = acc.astype(o_ref.dtype)


def projection(h, w, *, tn, sigmoid_out, name):
    N, D = h.shape
    C = w.shape[1]
    tm = _tile(N, 1024)
    return pl.pallas_call(
        functools.partial(_proj_kernel, sigmoid_out=sigmoid_out),
        grid=(N // tm, C // tn),
        in_specs=[
            pl.BlockSpec((tm, D), lambda i, j: (i, 0)),
            pl.BlockSpec((D, tn), lambda i, j: (0, j)),
        ],
        out_specs=pl.BlockSpec((tm, tn), lambda i, j: (i, j)),
        out_shape=jax.ShapeDtypeStruct((N, C), BF16),
        compiler_params=_params(("parallel", "arbitrary")),
        name=name,
    )(h, w)


def _na_bias_table(rel_bias):
    R = NA_ROWS_PER_BLOCK
    KR = NA_KEY_BLOCKS * R
    q_rows = np.stack([np.arange(R), R + np.arange(R), 2 * R + np.arange(R)])
    w_start = np.stack([np.zeros(R, np.int64), np.arange(R), np.full(R, KR - NA_WIN_H)])
    col = np.arange(GRID_W)
    col_start = np.clip(col - NA_WIN_W // 2, 0, GRID_W - NA_WIN_W)
    k_row = np.arange(KR)
    k_col = np.arange(GRID_W)
    row_ok = (k_row[None, None, :] >= w_start[:, :, None]) & (k_row[None, None, :] < w_start[:, :, None] + NA_WIN_H)
    col_ok = (k_col[None, :] >= col_start[:, None]) & (k_col[None, :] < col_start[:, None] + NA_WIN_W)
    dr = k_row[None, None, :] - q_rows[:, :, None] + (NA_WIN_H - 1)
    dc = k_col[None, :] - col[:, None] + (NA_WIN_W - 1)
    H = rel_bias.shape[0]
    pick_c = ((dc[None] == np.arange(2 * NA_WIN_W - 1)[:, None, None]) & col_ok[None]).astype(np.float32)
    by_col = jnp.einsum('hab,bqk->haqk', rel_bias.astype(F32), pick_c, precision=lax.Precision.HIGHEST)
    by_col = jnp.where(col_ok, by_col, NEG)
    masked = jnp.full((H, GRID_W, GRID_W), NEG, F32)
    types = []
    for t in range(3):
        q_blocks = []
        for qr in range(R):
            q_blocks.append(jnp.concatenate(
                [by_col[:, int(dr[t, qr, kr])] if row_ok[t, qr, kr] else masked for kr in range(KR)], axis=2))
        types.append(jnp.concatenate(q_blocks, axis=1))
    return jnp.stack(types)


def _na_kernel(q_ref, k0_ref, k1_ref, k2_ref, v0_ref, v1_ref, v2_ref, bias_ref, o_ref):
    lane = lax.broadcasted_iota(jnp.int32, (NA_BLOCK, LANE), 1)
    lo = lane < HEAD_DIM
    scale = HEAD_DIM ** -0.5
    for u in range(N_HEADS // 2):
        cs = slice(u * LANE, (u + 1) * LANE)
        q = q_ref[:, cs] * scale
        ks = (k0_ref[:, cs], k1_ref[:, cs], k2_ref[:, cs])
        vs = (v0_ref[:, cs], v1_ref[:, cs], v2_ref[:, cs])
        outs = []
        for hh in range(2):
            qh = jnp.where(lo if hh == 0 else ~lo, q, jnp.zeros_like(q))
            s = jnp.concatenate(
                [lax.dot_general(qh, k, (((1,), (1,)), ((), ())), preferred_element_type=F32) for k in ks], axis=1)
            s = s + bias_ref[2 * u + hh]
            m = jnp.max(s, axis=1, keepdims=True)
            p = jnp.exp(s - m)
            l = jnp.sum(p, axis=1, keepdims=True)
            pb = p.astype(BF16)
            o = None
            for j in range(NA_KEY_BLOCKS):
                t = jnp.dot(pb[:, j * NA_BLOCK:(j + 1) * NA_BLOCK], vs[j], preferred_element_type=F32)
                o = t if o is None else o + t
            outs.append(o / l)
        o_ref[:, cs] = jnp.where(lo, outs[0], outs[1]).astype(o_ref.dtype)


def neighborhood_attention(mix, bias_table, batch, seq):
    N = mix.shape[0]
    W = N_HEADS * HEAD_DIM
    nb = seq // NA_BLOCK
    assert seq % NA_BLOCK == 0 and nb >= NA_KEY_BLOCKS

    def kv_spec(colblk, j):
        return pl.BlockSpec(
            (NA_BLOCK, W),
            lambda b, i: (b * nb + jnp.clip(i - 1, 0, nb - NA_KEY_BLOCKS) + j, colblk))

    def bias_idx(b, i):
        return (jnp.where(i == 0, 0, jnp.where(i == nb - 1, 2, 1)), 0, 0, 0)

    return pl.pallas_call(
        _na_kernel,
        grid=(batch, nb),
        in_specs=[pl.BlockSpec((NA_BLOCK, W), lambda b, i: (b * nb + i, 0))]
        + [kv_spec(1, j) for j in range(NA_KEY_BLOCKS)]
        + [kv_spec(2, j) for j in range(NA_KEY_BLOCKS)]
        + [pl.BlockSpec((None, N_HEADS, NA_BLOCK, NA_KEY_BLOCKS * NA_BLOCK), bias_idx)],
        out_specs=pl.BlockSpec((NA_BLOCK, W), lambda b, i: (b * nb + i, 0)),
        out_shape=jax.ShapeDtypeStruct((N, W), BF16),
        compiler_params=_params(("parallel", "arbitrary")),
        name="neighborhood_attention",
    )(mix, mix, mix, mix, mix, mix, mix, bias_table)


def _rope_angles(pos, dim):
    inv_freq = ROPE_THETA ** (-jnp.arange(0, dim, 2, dtype=F32) / dim)
    ang = pos.astype(F32)[:, None] * inv_freq[None, :]
    return jnp.concatenate([ang, ang], axis=-1)


def _rot_tables(cos, sin, scale):
    n = cos.shape[1]
    first = (np.arange(n) % 32) < 16
    sa = jnp.where(first[None, :], -sin, 0.0)
    sb = jnp.where(first[None, :], 0.0, sin)
    return cos * scale, sa * scale, sb * scale


def _mla_tables(seq):
    pos = jnp.arange(seq)
    ang = _rope_angles(pos, MLA_ROPE)
    cos, sin = jnp.cos(ang), jnp.sin(ang)
    pad_hi = LANE - MLA_NOPE - MLA_ROPE

    def place(t, nope_val):
        return jnp.concatenate(
            [jnp.full((seq, MLA_NOPE), nope_val, F32), t, jnp.zeros((seq, pad_hi), F32)], axis=1)

    qs = (MLA_NOPE + MLA_ROPE) ** -0.5 * LOG2E
    qc, qa, qb = _rot_tables(cos, sin, qs)
    kc, ka, kb = _rot_tables(cos, sin, 1.0)
    q_tab = jnp.stack([place(qc, qs), place(qa, 0.0), place(qb, 0.0)])
    k_tab = jnp.stack([place(kc, 0.0), place(ka, 0.0), place(kb, 0.0)])
    return q_tab, k_tab


def _gqa_tables(seq):
    pos = jnp.arange(seq)
    half = HEAD_DIM // 2
    ar = _rope_angles(pos // GRID_W, half)
    ac = _rope_angles(pos % GRID_W, half)
    cos = jnp.concatenate([jnp.cos(ar), jnp.cos(ac)] * 2, axis=1)
    sin = jnp.concatenate([jnp.sin(ar), jnp.sin(ac)] * 2, axis=1)
    qs = HEAD_DIM ** -0.5 * LOG2E
    return jnp.stack(_rot_tables(cos, sin, qs)), jnp.stack(_rot_tables(cos, sin, 1.0))


def _apply_rot(x, tab_ref):
    n = x.shape[1]
    return (x * tab_ref[0] + pltpu.roll(x, n - 16, 1) * tab_ref[1] + pltpu.roll(x, 16, 1) * tab_ref[2])


def _rms(x, g):
    ms = jnp.mean(x * x, axis=-1, keepdims=True)
    return x * lax.rsqrt(ms + EPS) * g


def _mla_prep_kernel(cq_ref, ckv_ref, kr_ref, gq_ref, gkv_ref, wq_ref, wkn_ref, wv_ref, qtab_ref, ktab_ref,
                     q_out, k_out, v_out):
    cqn = _rms(cq_ref[...].astype(F32), gq_ref[...]).astype(BF16)
    ckvn = _rms(ckv_ref[...].astype(F32), gkv_ref[...]).astype(BF16)
    qf = jnp.dot(cqn, wq_ref[...], preferred_element_type=F32)
    kn = jnp.dot(ckvn, wkn_ref[...], preferred_element_type=F32)
    tm = cq_ref.shape[0]
    hi_lane = lax.broadcasted_iota(jnp.int32, (tm, LANE), 1) >= MLA_V
    vf = jnp.dot(ckvn, wv_ref[...], preferred_element_type=F32)
    k_rope = _apply_rot(kr_ref[...].astype(F32), ktab_ref)
    for h in range(N_HEADS):
        cs = slice(h * LANE, (h + 1) * LANE)
        q_out[:, cs] = _apply_rot(qf[:, cs], qtab_ref).astype(q_out.dtype)
        k_out[:, cs] = (kn[:, cs] + k_rope).astype(k_out.dtype)
        v_out[:, cs] = jnp.where(hi_lane, 1.0, vf[:, cs]).astype(v_out.dtype)


def mla_prepare(mix, g_q, g_kv, wq, wkn, wv, q_tab, k_tab, seq):
    N = mix.shape[0]
    tm = _tile(seq, 512)
    per_seq = seq // tm
    qr, kvr = wq.shape[0], wkn.shape[0]
    cq_blk = 1536 // qr
    ckv_blk = 2560 // kvr
    kr_blk = 2816 // LANE
    const = lambda i: (0, 0)
    tab = pl.BlockSpec((3, tm, LANE), lambda i: (0, i % per_seq, 0))
    return pl.pallas_call(
        _mla_prep_kernel,
        grid=(N // tm,),
        in_specs=[
            pl.BlockSpec((tm, qr), lambda i: (i, cq_blk)),
            pl.BlockSpec((tm, kvr), lambda i: (i, ckv_blk)),
            pl.BlockSpec((tm, LANE), lambda i: (i, kr_blk)),
            pl.BlockSpec((1, qr), const),
            pl.BlockSpec((1, kvr), const),
            pl.BlockSpec(wq.shape, const),
            pl.BlockSpec(wkn.shape, const),
            pl.BlockSpec(wv.shape, const),
            tab, tab,
        ],
        out_specs=[
            pl.BlockSpec((tm, N_HEADS * LANE), lambda i: (i, 0)),
            pl.BlockSpec((tm, N_HEADS * LANE), lambda i: (i, 0)),
            pl.BlockSpec((tm, N_HEADS * LANE), lambda i: (i, 0)),
        ],
        out_shape=[jax.ShapeDtypeStruct((N, N_HEADS * LANE), BF16)] * 3,
        compiler_params=_params(("parallel",)),
        name="mla_prepare",
    )(mix, mix, mix, g_q.reshape(1, qr), g_kv.reshape(1, kvr), wq, wkn, wv, q_tab, k_tab)


def _head_rms(x, g, seg_ref):
    sq = x * x
    hi = sq.astype(BF16)
    lo = (sq - hi.astype(F32)).astype(BF16)
    ms = (jnp.dot(hi, seg_ref[...], preferred_element_type=F32)
          + jnp.dot(lo, seg_ref[...], preferred_element_type=F32)) * (1.0 / HEAD_DIM)
    return x * lax.rsqrt(ms + EPS) * g


def _gqa_prep_kernel(q_ref, k_ref, v_ref, gq_ref, gk_ref, segq_ref, segk_ref, qtab_ref, ktab_ref,
                     q_out, k_out, v_out):
    tm = q_ref.shape[0]
    lane = lax.broadcasted_iota(jnp.int32, (tm, LANE), 1)
    lo = lane < HEAD_DIM
    for g in range(GQA_KV_HEADS):
        cs = slice(g * LANE, (g + 1) * LANE)
        v_out[:, cs] = jnp.where(lo, v_ref[:, cs], jnp.ones((tm, LANE), v_ref.dtype))
    qn = _head_rms(q_ref[...].astype(F32), gq_ref[...], segq_ref)
    for u in range(N_HEADS // 2):
        y = _apply_rot(qn[:, u * LANE:(u + 1) * LANE], qtab_ref)
        q_out[:, (2 * u) * LANE:(2 * u + 1) * LANE] = jnp.where(lo, y, 0.0).astype(q_out.dtype)
        q_out[:, (2 * u + 1) * LANE:(2 * u + 2) * LANE] = jnp.where(lo, 0.0, y).astype(q_out.dtype)
    kn = _head_rms(k_ref[...].astype(F32), gk_ref[...], segk_ref)
    k_out[...] = _apply_rot(kn, ktab_ref).astype(k_out.dtype)


def gqa_prepare(mix, g_q, g_k, q_tab, k_tab, seq):
    N = mix.shape[0]
    tm = _tile(seq, 512)
    per_seq = seq // tm
    W = N_HEADS * HEAD_DIM
    KW = GQA_KV_HEADS * HEAD_DIM
    VW = GQA_KV_HEADS * LANE
    const = lambda i: (0, 0)
    seg = lambda n: jnp.asarray(np.kron(np.eye(n // HEAD_DIM), np.ones((HEAD_DIM, HEAD_DIM))), BF16)
    tab = pl.BlockSpec((3, tm, LANE), lambda i: (0, i % per_seq, 0))
    return pl.pallas_call(
        _gqa_prep_kernel,
        grid=(N // tm,),
        in_specs=[
            pl.BlockSpec((tm, W), lambda i: (i, 2048 // W)),
            pl.BlockSpec((tm, KW), lambda i: (i, 2944 // KW)),
            pl.BlockSpec((tm, VW), lambda i: (i, 3072 // VW)),
            pl.BlockSpec((1, W), const),
            pl.BlockSpec((1, KW), const),
            pl.BlockSpec((W, W), const),
            pl.BlockSpec((KW, KW), const),
            tab, tab,
        ],
        out_specs=[
            pl.BlockSpec((tm, N_HEADS * LANE), lambda i: (i, 0)),
            pl.BlockSpec((tm, KW), lambda i: (i, 0)),
            pl.BlockSpec((tm, VW), lambda i: (i, 0)),
        ],
        out_shape=[
            jax.ShapeDtypeStruct((N, N_HEADS * LANE), BF16),
            jax.ShapeDtypeStruct((N, KW), BF16),
            jax.ShapeDtypeStruct((N, VW), BF16),
        ],
        compiler_params=_params(("parallel",)),
        name="gqa_prepare",
    )(mix, mix, mix, jnp.tile(g_q, N_HEADS).reshape(1, W), jnp.tile(g_k, GQA_KV_HEADS).reshape(1, KW),
      seg(W), seg(KW), q_tab, k_tab)


def _flash_kernel(q_ref, k_ref, v_ref, o_ref, m_sc, acc_sc, *, shared_k, v_groups):
    kv = pl.program_id(2)
    tq = q_ref.shape[0]
    tk = k_ref.shape[0]

    @pl.when(kv == 0)
    def _():
        m_sc[...] = jnp.full(m_sc.shape, -jnp.inf, F32)
        acc_sc[...] = jnp.zeros(acc_sc.shape, F32)

    for h in range(N_HEADS):
        q = q_ref[:, h * LANE:(h + 1) * LANE]
        k = k_ref[...] if shared_k else k_ref[:, h * LANE:(h + 1) * LANE]
        g = h % v_groups
        v = v_ref[:, g * LANE:(g + 1) * LANE]
        s = lax.dot_general(q, k, (((1,), (1,)), ((), ())), preferred_element_type=F32)
        m_prev = m_sc[h]
        m_new = jnp.maximum(m_prev, jnp.max(s, axis=1, keepdims=True))
        alpha = jnp.exp2(m_prev - m_new)
        p = jnp.exp2(s - jnp.concatenate([m_new] * (tk // LANE), axis=1))
        acc_sc[h] = alpha * acc_sc[h] + jnp.dot(p.astype(BF16), v, preferred_element_type=F32)
        m_sc[h] = m_new

    @pl.when(kv == pl.num_programs(2) - 1)
    def _():
        lane = lax.broadcasted_iota(jnp.int32, (tq, LANE), 1)
        lo = lane < HEAD_DIM
        for u in range(N_HEADS // 2):
            a = acc_sc[2 * u]
            b = acc_sc[2 * u + 1]
            a = a / pltpu.roll(a, HEAD_DIM, 1)
            b = pltpu.roll(b, HEAD_DIM, 1) / b
            o_ref[:, u * LANE:(u + 1) * LANE] = jnp.where(lo, a, b).astype(o_ref.dtype)


def flash_attention(q, k, v, batch, seq, *, shared_k, name):
    N = q.shape[0]
    tq = _tile(seq, 1024)
    tk = _tile(seq, 2048)
    nq, nk = seq // tq, seq // tk
    kw, vw = k.shape[1], v.shape[1]
    ow = N_HEADS * HEAD_DIM
    return pl.pallas_call(
        functools.partial(_flash_kernel, shared_k=shared_k, v_groups=vw // LANE),
        grid=(batch, nq, nk),
        in_specs=[
            pl.BlockSpec((tq, N_HEADS * LANE), lambda b, i, j: (b * nq + i, 0)),
            pl.BlockSpec((tk, kw), lambda b, i, j: (b * nk + j, 0)),
            pl.BlockSpec((tk, vw), lambda b, i, j: (b * nk + j, 0)),
        ],
        out_specs=pl.BlockSpec((tq, ow), lambda b, i, j: (b * nq + i, 0)),
        out_shape=jax.ShapeDtypeStruct((N, ow), BF16),
        scratch_shapes=[pltpu.VMEM((N_HEADS, tq, LANE), F32)] * 2,
        compiler_params=_params(("parallel", "parallel", "arbitrary")),
        name=name,
    )(q, k, v)


def _merge_kernel(oa_ref, ob_ref, oc_ref, sg_ref, x_ref, gt_ref, wb_ref, wo_ref, o_ref):
    D = x_ref.shape[1]
    merged = None
    for i, o_r in enumerate((oa_ref, ob_ref, oc_ref)):
        t = jnp.dot(o_r[...], wb_ref[i], preferred_element_type=F32)
        t = sg_ref[:, i * D:(i + 1) * D].astype(F32) * t
        merged = t if merged is None else merged + t
    mix = jnp.dot(merged.astype(BF16), wo_ref[...], preferred_element_type=F32)
    o_ref[...] = x_ref[...] + gt_ref[...] * mix


def merge_branches(o_a, o_b, o_c, sgate, x, gt, w_branch, w_out, seq):
    N, D = x.shape
    W = o_a.shape[1]
    tm = _tile(seq, 256)
    per_seq = seq // tm
    row = lambda i: (i, 0)
    return pl.pallas_call(
        _merge_kernel,
        grid=(N // tm,),
        in_specs=[
            pl.BlockSpec((tm, W), row), pl.BlockSpec((tm, W), row), pl.BlockSpec((tm, W), row),
            pl.BlockSpec((tm, N_BRANCH * D), row),
            pl.BlockSpec((tm, D), row),
            pl.BlockSpec((None, 1, D), lambda i: (i // per_seq, 0, 0)),
            pl.BlockSpec((N_BRANCH, W, D), lambda i: (0, 0, 0)),
            pl.BlockSpec((D, D), lambda i: (0, 0)),
        ],
        out_specs=pl.BlockSpec((tm, D), row),
        out_shape=jax.ShapeDtypeStruct((N, D), F32),
        compiler_params=_params(("parallel",)),
        name="merge_branches",
    )(o_a, o_b, o_c, sgate, x, gt, w_branch, w_out)


def _router_kernel(x_ref, g_ref, sc_ref, sh_ref, wr_ref, h_out, aff_out):
    h = _norm_mod(x_ref[...], g_ref[...], sc_ref[...], sh_ref[...]).astype(BF16)
    h_out[...] = h
    logits = lax.dot_general(wr_ref[...], h, (((1,), (1,)), ((), ())), preferred_element_type=F32)
    m = jnp.max(logits, axis=0, keepdims=True)
    p = jnp.exp(logits - m)
    aff_out[...] = p / jnp.sum(p, axis=0, keepdims=True)


def router(x, g, sc, sh, w_router_t, seq):
    N, D = x.shape
    E = w_router_t.shape[0]
    tm = _tile(seq, 512)
    per_seq = seq // tm
    return pl.pallas_call(
        _router_kernel,
        grid=(N // tm,),
        in_specs=[
            pl.BlockSpec((tm, D), lambda i: (i, 0)),
            pl.BlockSpec((1, D), lambda i: (0, 0)),
            pl.BlockSpec((None, 1, D), lambda i: (i // per_seq, 0, 0)),
            pl.BlockSpec((None, 1, D), lambda i: (i // per_seq, 0, 0)),
            pl.BlockSpec((E, D), lambda i: (0, 0)),
        ],
        out_specs=[pl.BlockSpec((tm, D), lambda i: (i, 0)), pl.BlockSpec((E, tm), lambda i: (0, i))],
        out_shape=[jax.ShapeDtypeStruct((N, D), BF16), jax.ShapeDtypeStruct((E, N), F32)],
        compiler_params=_params(("parallel",)),
        name="router",
    )(x, g.reshape(1, D), sc, sh, w_router_t)


MOE_WIN = 512
MOE_ALIGN = 16
MOE_CHUNK = 128
MOE_TILE = 512
_SEG_PIECES = (512, 256, 128, 64, 32, 16)


def _round_up(x, m):
    return (x + (m - 1)) // m * m


def _select_kernel(aff_ref, slot_ref, wbase_ref, wcnt_ref, *, cap, chunks_per_win):
    a = aff_ref[...]
    nc = a.shape[0]
    bits = pltpu.bitcast(a, jnp.int32)

    def search(i, t):
        cand = t | jnp.left_shift(jnp.int32(1), 30 - i)
        n_ge = jnp.sum((bits >= cand).astype(jnp.int32))
        return jnp.where(n_ge >= cap, cand, t)

    thr = lax.fori_loop(0, 31, search, jnp.int32(0))
    gt = bits > thr
    eq = bits == thr
    need = (cap - jnp.sum(gt.astype(jnp.int32))).astype(F32)

    li = lax.broadcasted_iota(jnp.int32, (LANE, LANE), 0)
    lj = lax.broadcasted_iota(jnp.int32, (LANE, LANE), 1)
    tri = jnp.where(li <= lj, 1.0, 0.0).astype(BF16)
    ones = jnp.ones((LANE, LANE), BF16)
    ci = lax.broadcasted_iota(jnp.int32, (nc, nc), 0)
    cj = lax.broadcasted_iota(jnp.int32, (nc, nc), 1)
    shift = chunks_per_win.bit_length() - 1
    wi, wj = jnp.right_shift(ci, shift), jnp.right_shift(cj, shift)
    as_bf = lambda m: jnp.where(m, 1.0, 0.0).astype(BF16)
    dot = lambda x, y: jnp.dot(x, y, preferred_element_type=F32)

    eq_b = as_bf(eq)
    eq_rank = dot(as_bf(cj < ci), dot(eq_b, ones).astype(BF16)) + dot(eq_b, tri) - eq_b.astype(F32)
    sel = gt | (eq & (eq_rank < need))
    sel_b = as_bf(sel)
    incl = dot(sel_b, tri)
    tot = dot(sel_b, ones).astype(BF16)
    in_win = dot(as_bf((cj < ci) & (wj == wi)), tot)
    wcnt = dot(as_bf(wj == wi), tot)
    wpad = jnp.ceil(wcnt * (1.0 / MOE_ALIGN)) * MOE_ALIGN
    first = (cj & (chunks_per_win - 1)) == 0
    wbase = dot(as_bf((wj < wi) & first), wpad.astype(BF16))
    slot = wbase + in_win + incl - sel_b.astype(F32)
    slot_ref[...] = jnp.where(sel, slot, -1.0).astype(jnp.int32)
    wbase_ref[...] = wbase.astype(jnp.int32)
    wcnt_ref[...] = wcnt.astype(jnp.int32)


def select_tokens(aff, cap):
    E, N = aff.shape
    nc = N // LANE
    cpw = MOE_WIN // LANE
    blk = pl.BlockSpec((None, nc, LANE), lambda e: (e, 0, 0))
    out = jax.ShapeDtypeStruct((E, nc, LANE), jnp.int32)
    slot, wbase, wcnt = pl.pallas_call(
        functools.partial(_select_kernel, cap=cap, chunks_per_win=cpw),
        grid=(E,),
        in_specs=[blk],
        out_specs=[blk, blk, blk],
        out_shape=[out, out, out],
        compiler_params=_params(("parallel",)),
        name="select_tokens",
    )(aff.reshape(E, nc, LANE))
    return slot.reshape(E, N), wbase[:, ::cpw, 0], wcnt[:, ::cpw, 0]


def _for_segment_pieces(n_rows, make_copy, act):
    off = jnp.int32(0)
    for piece in _SEG_PIECES:
        has = (n_rows & piece) != 0

        @pl.when(has)
        def _(off=off, piece=piece):
            act(make_copy(pl.multiple_of(off, MOE_ALIGN), piece))

        off = off + jnp.where(has, piece, 0)


MOE_GROUP = 4


def _dispatch_kernel(base_ref, cnt_ref, h_ref, slot_ref, xe_hbm, stage, xstage, zero_sc, sems, xsem, *, nw, ne):
    w = pl.program_id(0)
    cur = w % 2
    win = h_ref.shape[0]
    row_id = lax.broadcasted_iota(jnp.int32, (MOE_CHUNK, win), 0)

    def seg_rows(ww, ee):
        return _round_up(cnt_ref[ee * nw + ww], MOE_ALIGN)

    def head_copy(buf, ww, ee):
        b = base_ref[ee * nw + ww]

        def make(off, piece):
            return pltpu.make_async_copy(stage.at[buf, pl.ds(pl.multiple_of(ee * MOE_CHUNK + off, MOE_ALIGN), piece)],
                                         xe_hbm.at[ee, pl.ds(pl.multiple_of(b + off, MOE_ALIGN), piece)],
                                         sems.at[buf])
        return make

    def for_heads(buf, ww, act):
        for ee in range(ne):
            _for_segment_pieces(jnp.minimum(seg_rows(ww, ee), MOE_CHUNK), head_copy(buf, ww, ee), act)

    @pl.when(w == 0)
    def _():
        zero_sc[...] = jnp.zeros(zero_sc.shape, zero_sc.dtype)

    @pl.when(w >= 2)
    def _():
        for_heads(cur, w - 2, lambda c: c.wait())

    h = h_ref[...]

    def onehot(ee, k):
        want = base_ref[ee * nw + w] + k * MOE_CHUNK + row_id
        return jnp.where(slot_ref[ee:ee + 1, :] == want, 1.0, 0.0).astype(BF16)

    for g in range(ne // MOE_GROUP):
        sel = jnp.concatenate([onehot(g * MOE_GROUP + i, 0) for i in range(MOE_GROUP)], axis=0)
        stage[cur, g * MOE_GROUP * MOE_CHUNK:(g + 1) * MOE_GROUP * MOE_CHUNK, :] = jnp.dot(
            sel, h, preferred_element_type=F32).astype(stage.dtype)
    for_heads(cur, w, lambda c: c.start())

    for ee in range(ne):
        rows = seg_rows(w, ee)
        b = base_ref[ee * nw + w]

        def tail(k, carry, ee=ee, rows=rows, b=b):
            xstage[...] = jnp.dot(onehot(ee, k), h, preferred_element_type=F32).astype(xstage.dtype)

            def make(off, piece):
                return pltpu.make_async_copy(
                    xstage.at[pl.ds(off, piece)],
                    xe_hbm.at[ee, pl.ds(pl.multiple_of(b + k * MOE_CHUNK + off, MOE_ALIGN), piece)], xsem)
            n = jnp.minimum(rows - k * MOE_CHUNK, MOE_CHUNK)
            _for_segment_pieces(n, make, lambda c: c.start())
            _for_segment_pieces(n, make, lambda c: c.wait())
            return carry

        lax.fori_loop(1, (rows + MOE_CHUNK - 1) // MOE_CHUNK, tail, 0)

    @pl.when(w == nw - 1)
    def _():
        for ee in range(ne):
            used = base_ref[ee * nw + w] + seg_rows(w, ee)
            fill = _round_up(used, MOE_TILE) - used

            def zmake(off, piece, ee=ee, used=used):
                return pltpu.make_async_copy(zero_sc.at[pl.ds(0, piece)],
                                             xe_hbm.at[ee, pl.ds(pl.multiple_of(used + off, MOE_ALIGN), piece)], xsem)
            _for_segment_pieces(fill, zmake, lambda c: c.start())
            _for_segment_pieces(fill, zmake, lambda c: c.wait())
        for_heads(cur, w, lambda c: c.wait())
        for_heads(1 - cur, w - 1, lambda c: c.wait())


def dispatch_tokens(h, slot, base, cnt, rows_per_expert):
    N, D = h.shape
    E = slot.shape[0]
    nw = N // MOE_WIN
    assert nw >= 2 and E % MOE_GROUP == 0 and MOE_TILE <= _SEG_PIECES[0]
    return pl.pallas_call(
        functools.partial(_dispatch_kernel, nw=nw, ne=E),
        grid_spec=pltpu.PrefetchScalarGridSpec(
            num_scalar_prefetch=2,
            grid=(nw,),
            in_specs=[
                pl.BlockSpec((MOE_WIN, D), lambda w, *_: (w, 0)),
                pl.BlockSpec((E, MOE_WIN), lambda w, *_: (0, w)),
            ],
            out_specs=pl.BlockSpec(memory_space=pl.ANY),
            scratch_shapes=[
                pltpu.VMEM((2, E * MOE_CHUNK, D), BF16),
                pltpu.VMEM((MOE_CHUNK, D), BF16),
                pltpu.VMEM((MOE_TILE, D), BF16),
                pltpu.SemaphoreType.DMA((2,)),
                pltpu.SemaphoreType.DMA,
            ],
        ),
        out_shape=jax.ShapeDtypeStruct((E, rows_per_expert, D), BF16),
        compiler_params=_params(("arbitrary",)),
        name="dispatch_tokens",
    )(base.reshape(-1), cnt.reshape(-1), h, slot)


def _expert_kernel(used_ref, x_ref, wg_ref, wu_ref, wd_ref, o_ref):
    @pl.when(pl.program_id(1) * x_ref.shape[0] < used_ref[pl.program_id(0)])
    def _():
        x = x_ref[...]
        a = jnp.dot(x, wg_ref[...], preferred_element_type=F32)
        u = jnp.dot(x, wu_ref[...], preferred_element_type=F32)
        hmid = (a * jax.nn.sigmoid(a) * u).astype(BF16)
        o_ref[...] = jnp.dot(hmid, wd_ref[...], preferred_element_type=F32).astype(o_ref.dtype)


def expert_ffn(xe, used, w_gate, w_up, w_down):
    E, R, D = xe.shape
    F = w_gate.shape[2]
    return pl.pallas_call(
        _expert_kernel,
        grid_spec=pltpu.PrefetchScalarGridSpec(
            num_scalar_prefetch=1,
            grid=(E, R // MOE_TILE),
            in_specs=[
                pl.BlockSpec((None, MOE_TILE, D), lambda e, i, *_: (e, i, 0)),
                pl.BlockSpec((None, D, F), lambda e, i, *_: (e, 0, 0)),
                pl.BlockSpec((None, D, F), lambda e, i, *_: (e, 0, 0)),
                pl.BlockSpec((None, F, D), lambda e, i, *_: (e, 0, 0)),
            ],
            out_specs=pl.BlockSpec((None, MOE_TILE, D), lambda e, i, *_: (e, i, 0)),
        ),
        out_shape=jax.ShapeDtypeStruct((E, R, D), BF16),
        compiler_params=_params(("parallel", "arbitrary")),
        name="expert_ffn",
    )(used, xe, w_gate, w_up, w_down)


def _combine_kernel(base_ref, cnt_ref, lim_ref, x_ref, gt_ref, slot_ref, aff_ref, g_ref, ye_hbm, o_ref,
                    buf, xbuf, sems, xsem, *, nw, ne, final_norm):
    w = pl.program_id(0)
    cur = w % 2
    win = x_ref.shape[0]
    col = lax.broadcasted_iota(jnp.int32, (win, MOE_CHUNK), 1)

    def chunk_start(ww, ee, k):
        b = base_ref[ee * nw + ww]
        return pl.multiple_of(jnp.minimum(b + k * MOE_CHUNK, lim_ref[ee] - MOE_CHUNK), MOE_ALIGN)

    def head_copy(slot_id, ww, ee):
        return pltpu.make_async_copy(ye_hbm.at[ee, pl.ds(chunk_start(ww, ee, 0), MOE_CHUNK)],
                                     buf.at[slot_id, ee * MOE_CHUNK:(ee + 1) * MOE_CHUNK], sems.at[slot_id])

    @pl.when(w == 0)
    def _():
        for ee in range(ne):
            head_copy(cur, w, ee).start()

    @pl.when(w + 1 < nw)
    def _():
        for ee in range(ne):
            head_copy(1 - cur, w + 1, ee).start()

    def weights(ee, k):
        scol = slot_ref[:, ee:ee + 1]
        gate = aff_ref[:, ee:ee + 1]
        if not (isinstance(k, int) and k == 0):
            gate = jnp.where(scol >= base_ref[ee * nw + w] + k * MOE_CHUNK, gate, 0.0)
        return jnp.where(scol == chunk_start(w, ee, k) + col, gate, 0.0).astype(BF16)

    for ee in range(ne):
        head_copy(cur, w, ee).wait()
    o_ref[...] = jnp.dot(jnp.concatenate([weights(ee, 0) for ee in range(ne)], axis=1), buf[cur],
                         preferred_element_type=F32)

    for ee in range(ne):
        def extra(k, carry, ee=ee):
            cp = pltpu.make_async_copy(ye_hbm.at[ee, pl.ds(chunk_start(w, ee, k), MOE_CHUNK)], xbuf, xsem)
            cp.start()
            cp.wait()
            o_ref[...] += jnp.dot(weights(ee, k), xbuf[...], preferred_element_type=F32)
            return carry

        lax.fori_loop(1, (cnt_ref[ee * nw + w] + MOE_CHUNK - 1) // MOE_CHUNK, extra, 0)

    x = x_ref[...] + gt_ref[...] * o_ref[...]
    if final_norm:
        ms = jnp.mean(x * x, axis=-1, keepdims=True)
        x = x * lax.rsqrt(ms + EPS) * g_ref[...]
    o_ref[...] = x


def combine_experts(x, gt, slot_t, aff_t, ye, base, cnt, lim, g_final, seq, *, final_norm):
    N, D = x.shape
    E = slot_t.shape[1]
    nw = N // MOE_WIN
    per_seq = seq // MOE_WIN
    return pl.pallas_call(
        functools.partial(_combine_kernel, nw=nw, ne=E, final_norm=final_norm),
        grid_spec=pltpu.PrefetchScalarGridSpec(
            num_scalar_prefetch=3,
            grid=(nw,),
            in_specs=[
                pl.BlockSpec((MOE_WIN, D), lambda w, *_: (w, 0)),
                pl.BlockSpec((None, 1, D), lambda w, *_: (w // per_seq, 0, 0)),
                pl.BlockSpec((MOE_WIN, E), lambda w, *_: (w, 0)),
                pl.BlockSpec((MOE_WIN, E), lambda w, *_: (w, 0)),
                pl.BlockSpec((1, D), lambda w, *_: (0, 0)),
                pl.BlockSpec(memory_space=pl.ANY),
            ],
            out_specs=pl.BlockSpec((MOE_WIN, D), lambda w, *_: (w, 0)),
            scratch_shapes=[
                pltpu.VMEM((2, E * MOE_CHUNK, D), BF16),
                pltpu.VMEM((MOE_CHUNK, D), BF16),
                pltpu.SemaphoreType.DMA((2,)),
                pltpu.SemaphoreType.DMA,
            ],
        ),
        out_shape=jax.ShapeDtypeStruct((N, D), F32),
        compiler_params=_params(("arbitrary",)),
        name="combine_experts",
    )(base.reshape(-1), cnt.reshape(-1), lim, x, gt, slot_t, aff_t, g_final.reshape(1, D), ye)


def _prepare_layer(l, p, D):
    w_in = p['w_in'][l]
    W = N_HEADS * HEAD_DIM
    KW = GQA_KV_HEADS * HEAD_DIM
    qr = p['mla_w_uq'].shape[1]
    kvr = p['mla_w_ukv'].shape[1]
    o = 0
    seg = {}
    for name, width in (('a', 3 * W), ('cq', qr), ('ckv', kvr), ('kr', MLA_ROPE), ('gq', W), ('gk', KW), ('gv', KW),
                        ('gate', N_BRANCH * D)):
        seg[name] = w_in[:, o:o + width]
        o += width
    gq_cols = np.concatenate([np.arange(h * HEAD_DIM, (h + 1) * HEAD_DIM) for h in GQA_HEAD_ORDER])
    zeros = lambda n: jnp.zeros((D, n), w_in.dtype)
    gv = [p_ for g in range(GQA_KV_HEADS)
          for p_ in (seg['gv'][:, g * HEAD_DIM:(g + 1) * HEAD_DIM], zeros(LANE - HEAD_DIM))]
    w_mix = jnp.concatenate(
        [seg['a'], seg['cq'], seg['gq'][:, gq_cols], seg['ckv'],
         zeros(MLA_NOPE), seg['kr'], zeros(LANE - MLA_NOPE - MLA_ROPE), seg['gk']] + gv, axis=1).astype(BF16)

    w_uq = p['mla_w_uq'][l].reshape(qr, N_HEADS, MLA_NOPE + MLA_ROPE)
    wq = jnp.pad(w_uq, ((0, 0), (0, 0), (0, LANE - MLA_NOPE - MLA_ROPE))).reshape(qr, N_HEADS * LANE).astype(BF16)
    w_ukv = p['mla_w_ukv'][l].reshape(kvr, N_HEADS, MLA_NOPE + MLA_V)
    wkn = jnp.pad(w_ukv[:, :, :MLA_NOPE], ((0, 0), (0, 0), (0, LANE - MLA_NOPE))).reshape(kvr, N_HEADS * LANE)
    wv = jnp.pad(w_ukv[:, :, MLA_NOPE:], ((0, 0), (0, 0), (0, LANE - MLA_V))).reshape(kvr, N_HEADS * LANE)

    w_branch = p['w_branch'][l]
    w_branch = jnp.stack([w_branch[0], w_branch[1], w_branch[2][gq_cols]]).astype(BF16)
    return dict(
        w_mix=w_mix, w_gate_logits=seg['gate'].astype(BF16),
        wq=wq, wkn=wkn.astype(BF16), wv=wv.astype(BF16),
        w_branch=w_branch, w_out=p['w_out'][l].astype(BF16),
        w_router_t=p['w_router'][l].T.astype(BF16),
        w_exp_gate=p['w_exp_gate'][l].astype(BF16), w_exp_up=p['w_exp_up'][l].astype(BF16),
        w_exp_down=p['w_exp_down'][l].astype(BF16),
    )


def _trunk(x3, mod, p, layers, tables):
    B, T, D = x3.shape
    N = B * T
    x = x3.reshape(N, D)
    depth = len(layers)
    for l in range(depth):
        w = layers[l]
        sh1, sc1, gt1, sh2, sc2, gt2 = [mod[l][:, None, i * D:(i + 1) * D] for i in range(6)]
        h1 = norm_modulate(x, p['g_mix'][l], sc1, sh1, T)
        mix = projection(h1, w['w_mix'], tn=w['w_mix'].shape[1] // 2, sigmoid_out=False, name="mix_projection")
        sgate = projection(h1, w['w_gate_logits'], tn=_tile(N_BRANCH * D, 1536), sigmoid_out=True,
                           name="gate_projection")
        o_a = neighborhood_attention(mix, tables['na_bias'][l], B, T)
        qm, km, vm = mla_prepare(mix, p['mla_g_q'][l], p['mla_g_kv'][l], w['wq'], w['wkn'], w['wv'],
                                 tables['mla_q'][T], tables['mla_k'][T], T)
        o_b = flash_attention(qm, km, vm, B, T, shared_k=False, name="mla_attention")
        qg, kg, vg = gqa_prepare(mix, p['gqa_g_q'][l], p['gqa_g_k'][l], tables['gqa_q'][T], tables['gqa_k'][T], T)
        o_c = flash_attention(qg, kg, vg, B, T, shared_k=True, name="gqa_attention")
        x = merge_branches(o_a, o_b, o_c, sgate, x, gt1, w['w_branch'], w['w_out'], T)

        h, aff = router(x, p['g_ffn'][l], sc2, sh2, w['w_router_t'], T)
        cap = EC_CAPACITY * N // N_EXPERTS
        slot, base, cnt = select_tokens(aff, cap)
        used = base[:, -1] + _round_up(cnt[:, -1], MOE_ALIGN)
        rows = _round_up(cap + MOE_ALIGN * (N // MOE_WIN), MOE_TILE)
        xe = dispatch_tokens(h, slot, base, cnt, rows)
        ye = expert_ffn(xe, used, w['w_exp_gate'], w['w_exp_up'], w['w_exp_down'])
        x = combine_experts(x, gt2, slot.T, aff.T, ye, base, cnt, _round_up(used, MOE_TILE), p['g_final'], T,
                            final_norm=l == depth - 1)
    return x.reshape(B, T, D)


def kernel(x_prompt, x_sample, c_prompt, c_sample, w_ada, b_ada, g_mix, g_ffn, w_in, na_rel_bias, mla_g_q, mla_w_uq,
           mla_g_kv, mla_w_ukv, gqa_g_q, gqa_g_k, w_branch, w_out, w_router, w_exp_gate, w_exp_up, w_exp_down,
           g_final):
    p = dict(g_mix=g_mix, g_ffn=g_ffn, w_in=w_in, mla_g_q=mla_g_q, mla_w_uq=mla_w_uq, mla_g_kv=mla_g_kv,
             mla_w_ukv=mla_w_ukv, gqa_g_q=gqa_g_q, gqa_g_k=gqa_g_k, w_branch=w_branch, w_out=w_out,
             w_router=w_router, w_exp_gate=w_exp_gate, w_exp_up=w_exp_up, w_exp_down=w_exp_down, g_final=g_final)
    depth, D = g_mix.shape
    bp, bs = c_prompt.shape[0], c_sample.shape[0]
    rows = -(-(bp + bs) // 8) * 8
    c_all = jnp.concatenate([c_prompt, c_sample, jnp.zeros((rows - bp - bs, D), F32)], axis=0)
    mod = ada_modulation(c_all, w_ada, b_ada)
    layers = [_prepare_layer(l, p, D) for l in range(depth)]
    tables = dict(na_bias=[_na_bias_table(na_rel_bias[l]) for l in range(depth)],
                  mla_q={}, mla_k={}, gqa_q={}, gqa_k={})
    for T in {x_prompt.shape[1], x_sample.shape[1]}:
        tables['mla_q'][T], tables['mla_k'][T] = _mla_tables(T)
        tables['gqa_q'][T], tables['gqa_k'][T] = _gqa_tables(T)
    y_prompt = _trunk(x_prompt, mod[:, :bp], p, layers, tables)
    y_sample = _trunk(x_sample, mod[:, bp:bp + bs], p, layers, tables)
    return (y_prompt, y_sample)
```

```python
import functools
import math

import numpy as np
import jax
import jax.numpy as jnp
from jax import lax
from jax.experimental import pallas as pl
from jax.experimental.pallas import tpu as pltpu

F32 = jnp.float32
BF16 = jnp.bfloat16

GRID_W = 64
HEAD_DIM = 64
N_HEADS = 8
NA_WIN_H = 8
NA_WIN_W = 16
MLA_NOPE = 64
MLA_ROPE = 32
MLA_V = 64
GQA_KV_HEADS = 2
N_BRANCH = 3
N_EXPERTS = 16
EC_CAPACITY = 2
ROPE_THETA = 10000.0
EPS = 1e-6
LOG2E = 1.4426950408889634

LANE = 128
NA_ROWS_PER_BLOCK = 4
NA_BLOCK = NA_ROWS_PER_BLOCK * GRID_W
NA_KEY_BLOCKS = 3
NEG = -1e30
VMEM_LIMIT = 56 * 1024 * 1024

GQA_HEAD_ORDER = (0, 4, 1, 5, 2, 6, 3, 7)


def _params(sem):
    return pltpu.CompilerParams(dimension_semantics=sem, vmem_limit_bytes=VMEM_LIMIT)


def _tile(n, pref):
    t = min(n, pref)
    while n % t:
        t //= 2
    return t


def _ada_kernel(c_ref, w_ref, b_ref, o_ref):
    c = c_ref[...]
    a = (c * jax.nn.sigmoid(c)).astype(BF16)
    o_ref[...] = jnp.dot(a, w_ref[...].astype(BF16), preferred_element_type=F32) + b_ref[...]


def ada_modulation(c, w_ada, b_ada):
    L, D, D6 = w_ada.shape
    R = c.shape[0]
    tn = _tile(D6, 1024)
    return pl.pallas_call(
        _ada_kernel,
        grid=(L, D6 // tn),
        in_specs=[
            pl.BlockSpec((R, D), lambda l, j: (0, 0)),
            pl.BlockSpec((None, D, tn), lambda l, j: (l, 0, j)),
            pl.BlockSpec((None, 1, tn), lambda l, j: (l, 0, j)),
        ],
        out_specs=pl.BlockSpec((None, R, tn), lambda l, j: (l, 0, j)),
        out_shape=jax.ShapeDtypeStruct((L, R, D6), F32),
        compiler_params=_params(("parallel", "parallel")),
        name="ada_modulation",
    )(c, w_ada, b_ada.reshape(L, 1, D6))


def _norm_mod(x, g, sc, sh):
    ms = jnp.mean(x * x, axis=-1, keepdims=True)
    y = x * lax.rsqrt(ms + EPS)
    return (y * g) * (1.0 + sc) + sh


def _norm_mod_kernel(x_ref, g_ref, sc_ref, sh_ref, h_ref):
    h_ref[...] = _norm_mod(x_ref[...], g_ref[...], sc_ref[...], sh_ref[...]).astype(h_ref.dtype)


def norm_modulate(x, g, sc, sh, seq):
    N, D = x.shape
    tm = _tile(seq, 512)
    per_seq = seq // tm
    return pl.pallas_call(
        _norm_mod_kernel,
        grid=(N // tm,),
        in_specs=[
            pl.BlockSpec((tm, D), lambda i: (i, 0)),
            pl.BlockSpec((1, D), lambda i: (0, 0)),
            pl.BlockSpec((None, 1, D), lambda i: (i // per_seq, 0, 0)),
            pl.BlockSpec((None, 1, D), lambda i: (i // per_seq, 0, 0)),
        ],
        out_specs=pl.BlockSpec((tm, D), lambda i: (i, 0)),
        out_shape=jax.ShapeDtypeStruct((N, D), BF16),
        compiler_params=_params(("parallel",)),
        name="norm_modulate",
    )(x, g.reshape(1, D), sc, sh)


def _proj_kernel(h_ref, w_ref, o_ref, *, sigmoid_out):
    acc = jnp.dot(h_ref[...], w_ref[...], preferred_element_type=F32)
    if sigmoid_out:
        acc = jax.nn.sigmoid(acc)
    o_ref[...] = acc.astype(o_ref.dtype)


def projection(h, w, *, tn, sigmoid_out, name):
    N, D = h.shape
    C = w.shape[1]
    tm = _tile(N, 1024)
    return pl.pallas_call(
        functools.partial(_proj_kernel, sigmoid_out=sigmoid_out),
        grid=(N // tm, C // tn),
        in_specs=[
            pl.BlockSpec((tm, D), lambda i, j: (i, 0)),
            pl.BlockSpec((D, tn), lambda i, j: (0, j)),
        ],
        out_specs=pl.BlockSpec((tm, tn), lambda i, j: (i, j)),
        out_shape=jax.ShapeDtypeStruct((N, C), BF16),
        compiler_params=_params(("parallel", "arbitrary")),
        name=name,
    )(h, w)


def _na_bias_table(rel_bias):
    R = NA_ROWS_PER_BLOCK
    KR = NA_KEY_BLOCKS * R
    q_rows = np.stack([np.arange(R), R + np.arange(R), 2 * R + np.arange(R)])
    w_start = np.stack([np.zeros(R, np.int64), np.arange(R), np.full(R, KR - NA_WIN_H)])
    col = np.arange(GRID_W)
    col_start = np.clip(col - NA_WIN_W // 2, 0, GRID_W - NA_WIN_W)
    k_row = np.arange(KR)
    k_col = np.arange(GRID_W)
    row_ok = (k_row[None, None, :] >= w_start[:, :, None]) & (k_row[None, None, :] < w_start[:, :, None] + NA_WIN_H)
    col_ok = (k_col[None, :] >= col_start[:, None]) & (k_col[None, :] < col_start[:, None] + NA_WIN_W)
    dr = k_row[None, None, :] - q_rows[:, :, None] + (NA_WIN_H - 1)
    dc = k_col[None, :] - col[:, None] + (NA_WIN_W - 1)
    H = rel_bias.shape[0]
    pick_c = ((dc[None] == np.arange(2 * NA_WIN_W - 1)[:, None, None]) & col_ok[None]).astype(np.float32)
    by_col = jnp.einsum('hab,bqk->haqk', rel_bias.astype(F32), pick_c, precision=lax.Precision.HIGHEST)
    by_col = jnp.where(col_ok, by_col, NEG)
    masked = jnp.full((H, GRID_W, GRID_W), NEG, F32)
    types = []
    for t in range(3):
        q_blocks = []
        for qr in range(R):
            q_blocks.append(jnp.concatenate(
                [by_col[:, int(dr[t, qr, kr])] if row_ok[t, qr, kr] else masked for kr in range(KR)], axis=2))
        types.append(jnp.concatenate(q_blocks, axis=1))
    return jnp.stack(types)


def _na_kernel(q_ref, k0_ref, k1_ref, k2_ref, v0_ref, v1_ref, v2_ref, bias_ref, o_ref):
    lane = lax.broadcasted_iota(jnp.int32, (NA_BLOCK, LANE), 1)
    lo = lane < HEAD_DIM
    scale = HEAD_DIM ** -0.5
    for u in range(N_HEADS // 2):
        cs = slice(u * LANE, (u + 1) * LANE)
        q = q_ref[:, cs] * scale
        ks = (k0_ref[:, cs], k1_ref[:, cs], k2_ref[:, cs])
        vs = (v0_ref[:, cs], v1_ref[:, cs], v2_ref[:, cs])
        outs = []
        for hh in range(2):
            qh = jnp.where(lo if hh == 0 else ~lo, q, jnp.zeros_like(q))
            s = jnp.concatenate(
                [lax.dot_general(qh, k, (((1,), (1,)), ((), ())), preferred_element_type=F32) for k in ks], axis=1)
            s = s + bias_ref[2 * u + hh]
            m = jnp.max(s, axis=1, keepdims=True)
            p = jnp.exp(s - m)
            l = jnp.sum(p, axis=1, keepdims=True)
            pb = p.astype(BF16)
            o = None
            for j in range(NA_KEY_BLOCKS):
                t = jnp.dot(pb[:, j * NA_BLOCK:(j + 1) * NA_BLOCK], vs[j], preferred_element_type=F32)
                o = t if o is None else o + t
            outs.append(o / l)
        o_ref[:, cs] = jnp.where(lo, outs[0], outs[1]).astype(o_ref.dtype)


def neighborhood_attention(mix, bias_table, batch, seq):
    N = mix.shape[0]
    W = N_HEADS * HEAD_DIM
    nb = seq // NA_BLOCK
    assert seq % NA_BLOCK == 0 and nb >= NA_KEY_BLOCKS

    def kv_spec(colblk, j):
        return pl.BlockSpec(
            (NA_BLOCK, W),
            lambda b, i: (b * nb + jnp.clip(i - 1, 0, nb - NA_KEY_BLOCKS) + j, colblk))

    def bias_idx(b, i):
        return (jnp.where(i == 0, 0, jnp.where(i == nb - 1, 2, 1)), 0, 0, 0)

    return pl.pallas_call(
        _na_kernel,
        grid=(batch, nb),
        in_specs=[pl.BlockSpec((NA_BLOCK, W), lambda b, i: (b * nb + i, 0))]
        + [kv_spec(1, j) for j in range(NA_KEY_BLOCKS)]
        + [kv_spec(2, j) for j in range(NA_KEY_BLOCKS)]
        + [pl.BlockSpec((None, N_HEADS, NA_BLOCK, NA_KEY_BLOCKS * NA_BLOCK), bias_idx)],
        out_specs=pl.BlockSpec((NA_BLOCK, W), lambda b, i: (b * nb + i, 0)),
        out_shape=jax.ShapeDtypeStruct((N, W), BF16),
        compiler_params=_params(("parallel", "arbitrary")),
        name="neighborhood_attention",
    )(mix, mix, mix, mix, mix, mix, mix, bias_table)


def _rope_angles(pos, dim):
    inv_freq = ROPE_THETA ** (-jnp.arange(0, dim, 2, dtype=F32) / dim)
    ang = pos.astype(F32)[:, None] * inv_freq[None, :]
    return jnp.concatenate([ang, ang], axis=-1)


def _rot_tables(cos, sin, scale):
    n = cos.shape[1]
    first = (np.arange(n) % 32) < 16
    sa = jnp.where(first[None, :], -sin, 0.0)
    sb = jnp.where(first[None, :], 0.0, sin)
    return cos * scale, sa * scale, sb * scale


def _mla_tables(seq):
    pos = jnp.arange(seq)
    ang = _rope_angles(pos, MLA_ROPE)
    cos, sin = jnp.cos(ang), jnp.sin(ang)
    pad_hi = LANE - MLA_NOPE - MLA_ROPE

    def place(t, nope_val):
        return jnp.concatenate(
            [jnp.full((seq, MLA_NOPE), nope_val, F32), t, jnp.zeros((seq, pad_hi), F32)], axis=1)

    qs = (MLA_NOPE + MLA_ROPE) ** -0.5 * LOG2E
    qc, qa, qb = _rot_tables(cos, sin, qs)
    kc, ka, kb = _rot_tables(cos, sin, 1.0)
    q_tab = jnp.stack([place(qc, qs), place(qa, 0.0), place(qb, 0.0)])
    k_tab = jnp.stack([place(kc, 0.0), place(ka, 0.0), place(kb, 0.0)])
    return q_tab, k_tab


def _gqa_tables(seq):
    pos = jnp.arange(seq)
    half = HEAD_DIM // 2
    ar = _rope_angles(pos // GRID_W, half)
    ac = _rope_angles(pos % GRID_W, half)
    cos = jnp.concatenate([jnp.cos(ar), jnp.cos(ac)] * 2, axis=1)
    sin = jnp.concatenate([jnp.sin(ar), jnp.sin(ac)] * 2, axis=1)
    qs = HEAD_DIM ** -0.5 * LOG2E
    return jnp.stack(_rot_tables(cos, sin, qs)), jnp.stack(_rot_tables(cos, sin, 1.0))


def _apply_rot(x, tab_ref):
    n = x.shape[1]
    return (x * tab_ref[0] + pltpu.roll(x, n - 16, 1) * tab_ref[1] + pltpu.roll(x, 16, 1) * tab_ref[2])


def _rms(x, g):
    ms = jnp.mean(x * x, axis=-1, keepdims=True)
    return x * lax.rsqrt(ms + EPS) * g


def _mla_prep_kernel(cq_ref, ckv_ref, kr_ref, gq_ref, gkv_ref, wq_ref, wkn_ref, wv_ref, qtab_ref, ktab_ref,
                     q_out, k_out, v_out):
    cqn = _rms(cq_ref[...].astype(F32), gq_ref[...]).astype(BF16)
    ckvn = _rms(ckv_ref[...].astype(F32), gkv_ref[...]).astype(BF16)
    qf = jnp.dot(cqn, wq_ref[...], preferred_element_type=F32)
    kn = jnp.dot(ckvn, wkn_ref[...], preferred_element_type=F32)
    tm = cq_ref.shape[0]
    hi_lane = lax.broadcasted_iota(jnp.int32, (tm, LANE), 1) >= MLA_V
    vf = jnp.dot(ckvn, wv_ref[...], preferred_element_type=F32)
    k_rope = _apply_rot(kr_ref[...].astype(F32), ktab_ref)
    for h in range(N_HEADS):
        cs = slice(h * LANE, (h + 1) * LANE)
        q_out[:, cs] = _apply_rot(qf[:, cs], qtab_ref).astype(q_out.dtype)
        k_out[:, cs] = (kn[:, cs] + k_rope).astype(k_out.dtype)
        v_out[:, cs] = jnp.where(hi_lane, 1.0, vf[:, cs]).astype(v_out.dtype)


def mla_prepare(mix, g_q, g_kv, wq, wkn, wv, q_tab, k_tab, seq):
    N = mix.shape[0]
    tm = _tile(seq, 512)
    per_seq = seq // tm
    qr, kvr = wq.shape[0], wkn.shape[0]
    cq_blk = 1536 // qr
    ckv_blk = 2560 // kvr
    kr_blk = 2816 // LANE
    const = lambda i: (0, 0)
    tab = pl.BlockSpec((3, tm, LANE), lambda i: (0, i % per_seq, 0))
    return pl.pallas_call(
        _mla_prep_kernel,
        grid=(N // tm,),
        in_specs=[
            pl.BlockSpec((tm, qr), lambda i: (i, cq_blk)),
            pl.BlockSpec((tm, kvr), lambda i: (i, ckv_blk)),
            pl.BlockSpec((tm, LANE), lambda i: (i, kr_blk)),
            pl.BlockSpec((1, qr), const),
            pl.BlockSpec((1, kvr), const),
            pl.BlockSpec(wq.shape, const),
            pl.BlockSpec(wkn.shape, const),
            pl.BlockSpec(wv.shape, const),
            tab, tab,
        ],
        out_specs=[
            pl.BlockSpec((tm, N_HEADS * LANE), lambda i: (i, 0)),
            pl.BlockSpec((tm, N_HEADS * LANE), lambda i: (i, 0)),
            pl.BlockSpec((tm, N_HEADS * LANE), lambda i: (i, 0)),
        ],
        out_shape=[jax.ShapeDtypeStruct((N, N_HEADS * LANE), BF16)] * 3,
        compiler_params=_params(("parallel",)),
        name="mla_prepare",
    )(mix, mix, mix, g_q.reshape(1, qr), g_kv.reshape(1, kvr), wq, wkn, wv, q_tab, k_tab)


def _head_rms(x, g, seg_ref):
    sq = x * x
    hi = sq.astype(BF16)
    lo = (sq - hi.astype(F32)).astype(BF16)
    ms = (jnp.dot(hi, seg_ref[...], preferred_element_type=F32)
          + jnp.dot(lo, seg_ref[...], preferred_element_type=F32)) * (1.0 / HEAD_DIM)
    return x * lax.rsqrt(ms + EPS) * g


def _gqa_prep_kernel(q_ref, k_ref, v_ref, gq_ref, gk_ref, segq_ref, segk_ref, qtab_ref, ktab_ref,
                     q_out, k_out, v_out):
    tm = q_ref.shape[0]
    lane = lax.broadcasted_iota(jnp.int32, (tm, LANE), 1)
    lo = lane < HEAD_DIM
    for g in range(GQA_KV_HEADS):
        cs = slice(g * LANE, (g + 1) * LANE)
        v_out[:, cs] = jnp.where(lo, v_ref[:, cs], jnp.ones((tm, LANE), v_ref.dtype))
    qn = _head_rms(q_ref[...].astype(F32), gq_ref[...], segq_ref)
    for u in range(N_HEADS // 2):
        y = _apply_rot(qn[:, u * LANE:(u + 1) * LANE], qtab_ref)
        q_out[:, (2 * u) * LANE:(2 * u + 1) * LANE] = jnp.where(lo, y, 0.0).astype(q_out.dtype)
        q_out[:, (2 * u + 1) * LANE:(2 * u + 2) * LANE] = jnp.where(lo, 0.0, y).astype(q_out.dtype)
    kn = _head_rms(k_ref[...].astype(F32), gk_ref[...], segk_ref)
    k_out[...] = _apply_rot(kn, ktab_ref).astype(k_out.dtype)


def gqa_prepare(mix, g_q, g_k, q_tab, k_tab, seq):
    N = mix.shape[0]
    tm = _tile(seq, 512)
    per_seq = seq // tm
    W = N_HEADS * HEAD_DIM
    KW = GQA_KV_HEADS * HEAD_DIM
    VW = GQA_KV_HEADS * LANE
    const = lambda i: (0, 0)
    seg = lambda n: jnp.asarray(np.kron(np.eye(n // HEAD_DIM), np.ones((HEAD_DIM, HEAD_DIM))), BF16)
    tab = pl.BlockSpec((3, tm, LANE), lambda i: (0, i % per_seq, 0))
    return pl.pallas_call(
        _gqa_prep_kernel,
        grid=(N // tm,),
        in_specs=[
            pl.BlockSpec((tm, W), lambda i: (i, 2048 // W)),
            pl.BlockSpec((tm, KW), lambda i: (i, 2944 // KW)),
            pl.BlockSpec((tm, VW), lambda i: (i, 3072 // VW)),
            pl.BlockSpec((1, W), const),
            pl.BlockSpec((1, KW), const),
            pl.BlockSpec((W, W), const),
            pl.BlockSpec((KW, KW), const),
            tab, tab,
        ],
        out_specs=[
            pl.BlockSpec((tm, N_HEADS * LANE), lambda i: (i, 0)),
            pl.BlockSpec((tm, KW), lambda i: (i, 0)),
            pl.BlockSpec((tm, VW), lambda i: (i, 0)),
        ],
        out_shape=[
            jax.ShapeDtypeStruct((N, N_HEADS * LANE), BF16),
            jax.ShapeDtypeStruct((N, KW), BF16),
            jax.ShapeDtypeStruct((N, VW), BF16),
        ],
        compiler_params=_params(("parallel",)),
        name="gqa_prepare",
    )(mix, mix, mix, jnp.tile(g_q, N_HEADS).reshape(1, W), jnp.tile(g_k, GQA_KV_HEADS).reshape(1, KW),
      seg(W), seg(KW), q_tab, k_tab)


def _flash_kernel(q_ref, k_ref, v_ref, o_ref, m_sc, acc_sc, *, shared_k, v_groups):
    kv = pl.program_id(2)
    tq = q_ref.shape[0]
    tk = k_ref.shape[0]

    @pl.when(kv == 0)
    def _():
        m_sc[...] = jnp.full(m_sc.shape, -jnp.inf, F32)
        acc_sc[...] = jnp.zeros(acc_sc.shape, F32)

    for h in range(N_HEADS):
        q = q_ref[:, h * LANE:(h + 1) * LANE]
        k = k_ref[...] if shared_k else k_ref[:, h * LANE:(h + 1) * LANE]
        g = h % v_groups
        v = v_ref[:, g * LANE:(g + 1) * LANE]
        s = lax.dot_general(q, k, (((1,), (1,)), ((), ())), preferred_element_type=F32)
        m_prev = m_sc[h]
        m_new = jnp.maximum(m_prev, jnp.max(s, axis=1, keepdims=True))
        alpha = jnp.exp2(m_prev - m_new)
        p = jnp.exp2(s - jnp.concatenate([m_new] * (tk // LANE), axis=1))
        acc_sc[h] = alpha * acc_sc[h] + jnp.dot(p.astype(BF16), v, preferred_element_type=F32)
        m_sc[h] = m_new

    @pl.when(kv == pl.num_programs(2) - 1)
    def _():
        lane = lax.broadcasted_iota(jnp.int32, (tq, LANE), 1)
        lo = lane < HEAD_DIM
        for u in range(N_HEADS // 2):
            a = acc_sc[2 * u]
            b = acc_sc[2 * u + 1]
            a = a / pltpu.roll(a, HEAD_DIM, 1)
            b = pltpu.roll(b, HEAD_DIM, 1) / b
            o_ref[:, u * LANE:(u + 1) * LANE] = jnp.where(lo, a, b).astype(o_ref.dtype)


def flash_attention(q, k, v, batch, seq, *, shared_k, name):
    N = q.shape[0]
    tq = _tile(seq, 1024)
    tk = _tile(seq, 2048)
    nq, nk = seq // tq, seq // tk
    kw, vw = k.shape[1], v.shape[1]
    ow = N_HEADS * HEAD_DIM
    return pl.pallas_call(
        functools.partial(_flash_kernel, shared_k=shared_k, v_groups=vw // LANE),
        grid=(batch, nq, nk),
        in_specs=[
            pl.BlockSpec((tq, N_HEADS * LANE), lambda b, i, j: (b * nq + i, 0)),
            pl.BlockSpec((tk, kw), lambda b, i, j: (b * nk + j, 0)),
            pl.BlockSpec((tk, vw), lambda b, i, j: (b * nk + j, 0)),
        ],
        out_specs=pl.BlockSpec((tq, ow), lambda b, i, j: (b * nq + i, 0)),
        out_shape=jax.ShapeDtypeStruct((N, ow), BF16),
        scratch_shapes=[pltpu.VMEM((N_HEADS, tq, LANE), F32)] * 2,
        compiler_params=_params(("parallel", "parallel", "arbitrary")),
        name=name,
    )(q, k, v)


def _merge_kernel(oa_ref, ob_ref, oc_ref, sg_ref, x_ref, gt_ref, wb_ref, wo_ref, o_ref):
    D = x_ref.shape[1]
    merged = None
    for i, o_r in enumerate((oa_ref, ob_ref, oc_ref)):
        t = jnp.dot(o_r[...], wb_ref[i], preferred_element_type=F32)
        t = sg_ref[:, i * D:(i + 1) * D].astype(F32) * t
        merged = t if merged is None else merged + t
    mix = jnp.dot(merged.astype(BF16), wo_ref[...], preferred_element_type=F32)
    o_ref[...] = x_ref[...] + gt_ref[...] * mix


def merge_branches(o_a, o_b, o_c, sgate, x, gt, w_branch, w_out, seq):
    N, D = x.shape
    W = o_a.shape[1]
    tm = _tile(seq, 256)
    per_seq = seq // tm
    row = lambda i: (i, 0)
    return pl.pallas_call(
        _merge_kernel,
        grid=(N // tm,),
        in_specs=[
            pl.BlockSpec((tm, W), row), pl.BlockSpec((tm, W), row), pl.BlockSpec((tm, W), row),
            pl.BlockSpec((tm, N_BRANCH * D), row),
            pl.BlockSpec((tm, D), row),
            pl.BlockSpec((None, 1, D), lambda i: (i // per_seq, 0, 0)),
            pl.BlockSpec((N_BRANCH, W, D), lambda i: (0, 0, 0)),
            pl.BlockSpec((D, D), lambda i: (0, 0)),
        ],
        out_specs=pl.BlockSpec((tm, D), row),
        out_shape=jax.ShapeDtypeStruct((N, D), F32),
        compiler_params=_params(("parallel",)),
        name="merge_branches",
    )(o_a, o_b, o_c, sgate, x, gt, w_branch, w_out)


def _router_kernel(x_ref, g_ref, sc_ref, sh_ref, wr_ref, h_out, aff_out):
    h = _norm_mod(x_ref[...], g_ref[...], sc_ref[...], sh_ref[...]).astype(BF16)
    h_out[...] = h
    logits = lax.dot_general(wr_ref[...], h, (((1,), (1,)), ((), ())), preferred_element_type=F32)
    m = jnp.max(logits, axis=0, keepdims=True)
    p = jnp.exp(logits - m)
    aff_out[...] = p / jnp.sum(p, axis=0, keepdims=True)


def router(x, g, sc, sh, w_router_t, seq):
    N, D = x.shape
    E = w_router_t.shape[0]
    tm = _tile(seq, 512)
    per_seq = seq // tm
    return pl.pallas_call(
        _router_kernel,
        grid=(N // tm,),
        in_specs=[
            pl.BlockSpec((tm, D), lambda i: (i, 0)),
            pl.BlockSpec((1, D), lambda i: (0, 0)),
            pl.BlockSpec((None, 1, D), lambda i: (i // per_seq, 0, 0)),
            pl.BlockSpec((None, 1, D), lambda i: (i // per_seq, 0, 0)),
            pl.BlockSpec((E, D), lambda i: (0, 0)),
        ],
        out_specs=[pl.BlockSpec((tm, D), lambda i: (i, 0)), pl.BlockSpec((E, tm), lambda i: (0, i))],
        out_shape=[jax.ShapeDtypeStruct((N, D), BF16), jax.ShapeDtypeStruct((E, N), F32)],
        compiler_params=_params(("parallel",)),
        name="router",
    )(x, g.reshape(1, D), sc, sh, w_router_t)


MOE_WIN = 512
MOE_ALIGN = 16
MOE_CHUNK = 128
MOE_TILE = 512
_SEG_PIECES = (512, 256, 128, 64, 32, 16)


def _round_up(x, m):
    return (x + (m - 1)) // m * m


def _select_kernel(aff_ref, slot_ref, wbase_ref, wcnt_ref, *, cap, chunks_per_win):
    a = aff_ref[...]
    nc = a.shape[0]
    bits = pltpu.bitcast(a, jnp.int32)

    def search(i, t):
        cand = t | jnp.left_shift(jnp.int32(1), 30 - i)
        n_ge = jnp.sum((bits >= cand).astype(jnp.int32))
        return jnp.where(n_ge >= cap, cand, t)

    thr = lax.fori_loop(0, 31, search, jnp.int32(0))
    gt = bits > thr
    eq = bits == thr
    need = (cap - jnp.sum(gt.astype(jnp.int32))).astype(F32)

    li = lax.broadcasted_iota(jnp.int32, (LANE, LANE), 0)
    lj = lax.broadcasted_iota(jnp.int32, (LANE, LANE), 1)
    tri = jnp.where(li <= lj, 1.0, 0.0).astype(BF16)
    ones = jnp.ones((LANE, LANE), BF16)
    ci = lax.broadcasted_iota(jnp.int32, (nc, nc), 0)
    cj = lax.broadcasted_iota(jnp.int32, (nc, nc), 1)
    shift = chunks_per_win.bit_length() - 1
    wi, wj = jnp.right_shift(ci, shift), jnp.right_shift(cj, shift)
    as_bf = lambda m: jnp.where(m, 1.0, 0.0).astype(BF16)
    dot = lambda x, y: jnp.dot(x, y, preferred_element_type=F32)

    eq_b = as_bf(eq)
    eq_rank = dot(as_bf(cj < ci), dot(eq_b, ones).astype(BF16)) + dot(eq_b, tri) - eq_b.astype(F32)
    sel = gt | (eq & (eq_rank < need))
    sel_b = as_bf(sel)
    incl = dot(sel_b, tri)
    tot = dot(sel_b, ones).astype(BF16)
    in_win = dot(as_bf((cj < ci) & (wj == wi)), tot)
    wcnt = dot(as_bf(wj == wi), tot)
    wpad = jnp.ceil(wcnt * (1.0 / MOE_ALIGN)) * MOE_ALIGN
    first = (cj & (chunks_per_win - 1)) == 0
    wbase = dot(as_bf((wj < wi) & first), wpad.astype(BF16))
    slot = wbase + in_win + incl - sel_b.astype(F32)
    slot_ref[...] = jnp.where(sel, slot, -1.0).astype(jnp.int32)
    wbase_ref[...] = wbase.astype(jnp.int32)
    wcnt_ref[...] = wcnt.astype(jnp.int32)


def select_tokens(aff, cap):
    E, N = aff.shape
    nc = N // LANE
    cpw = MOE_WIN // LANE
    blk = pl.BlockSpec((None, nc, LANE), lambda e: (e, 0, 0))
    out = jax.ShapeDtypeStruct((E, nc, LANE), jnp.int32)
    slot, wbase, wcnt = pl.pallas_call(
        functools.partial(_select_kernel, cap=cap, chunks_per_win=cpw),
        grid=(E,),
        in_specs=[blk],
        out_specs=[blk, blk, blk],
        out_shape=[out, out, out],
        compiler_params=_params(("parallel",)),
        name="select_tokens",
    )(aff.reshape(E, nc, LANE))
    return slot.reshape(E, N), wbase[:, ::cpw, 0], wcnt[:, ::cpw, 0]


def _for_segment_pieces(n_rows, make_copy, act):
    off = jnp.int32(0)
    for piece in _SEG_PIECES:
        has = (n_rows & piece) != 0

        @pl.when(has)
        def _(off=off, piece=piece):
            act(make_copy(pl.multiple_of(off, MOE_ALIGN), piece))

        off = off + jnp.where(has, piece, 0)


MOE_GROUP = 4


def _dispatch_kernel(base_ref, cnt_ref, h_ref, slot_ref, xe_hbm, stage, xstage, zero_sc, sems, xsem, *, nw, ne):
    w = pl.program_id(0)
    cur = w % 2
    win = h_ref.shape[0]
    row_id = lax.broadcasted_iota(jnp.int32, (MOE_CHUNK, win), 0)

    def seg_rows(ww, ee):
        return _round_up(cnt_ref[ee * nw + ww], MOE_ALIGN)

    def head_copy(buf, ww, ee):
        b = base_ref[ee * nw + ww]

        def make(off, piece):
            return pltpu.make_async_copy(stage.at[buf, pl.ds(pl.multiple_of(ee * MOE_CHUNK + off, MOE_ALIGN), piece)],
                                         xe_hbm.at[ee, pl.ds(pl.multiple_of(b + off, MOE_ALIGN), piece)],
                                         sems.at[buf])
        return make

    def for_heads(buf, ww, act):
        for ee in range(ne):
            _for_segment_pieces(jnp.minimum(seg_rows(ww, ee), MOE_CHUNK), head_copy(buf, ww, ee), act)

    @pl.when(w == 0)
    def _():
        zero_sc[...] = jnp.zeros(zero_sc.shape, zero_sc.dtype)

    @pl.when(w >= 2)
    def _():
        for_heads(cur, w - 2, lambda c: c.wait())

    h = h_ref[...]

    def onehot(ee, k):
        want = base_ref[ee * nw + w] + k * MOE_CHUNK + row_id
        return jnp.where(slot_ref[ee:ee + 1, :] == want, 1.0, 0.0).astype(BF16)

    for g in range(ne // MOE_GROUP):
        sel = jnp.concatenate([onehot(g * MOE_GROUP + i, 0) for i in range(MOE_GROUP)], axis=0)
        stage[cur, g * MOE_GROUP * MOE_CHUNK:(g + 1) * MOE_GROUP * MOE_CHUNK, :] = jnp.dot(
            sel, h, preferred_element_type=F32).astype(stage.dtype)
    for_heads(cur, w, lambda c: c.start())

    for ee in range(ne):
        rows = seg_rows(w, ee)
        b = base_ref[ee * nw + w]

        def tail(k, carry, ee=ee, rows=rows, b=b):
            xstage[...] = jnp.dot(onehot(ee, k), h, preferred_element_type=F32).astype(xstage.dtype)

            def make(off, piece):
                return pltpu.make_async_copy(
                    xstage.at[pl.ds(off, piece)],
                    xe_hbm.at[ee, pl.ds(pl.multiple_of(b + k * MOE_CHUNK + off, MOE_ALIGN), piece)], xsem)
            n = jnp.minimum(rows - k * MOE_CHUNK, MOE_CHUNK)
            _for_segment_pieces(n, make, lambda c: c.start())
            _for_segment_pieces(n, make, lambda c: c.wait())
            return carry

        lax.fori_loop(1, (rows + MOE_CHUNK - 1) // MOE_CHUNK, tail, 0)

    @pl.when(w == nw - 1)
    def _():
        for ee in range(ne):
            used = base_ref[ee * nw + w] + seg_rows(w, ee)
            fill = _round_up(used, MOE_TILE) - used

            def zmake(off, piece, ee=ee, used=used):
                return pltpu.make_async_copy(zero_sc.at[pl.ds(0, piece)],
                                             xe_hbm.at[ee, pl.ds(pl.multiple_of(used + off, MOE_ALIGN), piece)], xsem)
            _for_segment_pieces(fill, zmake, lambda c: c.start())
            _for_segment_pieces(fill, zmake, lambda c: c.wait())
        for_heads(cur, w, lambda c: c.wait())
        for_heads(1 - cur, w - 1, lambda c: c.wait())


def dispatch_tokens(h, slot, base, cnt, rows_per_expert):
    N, D = h.shape
    E = slot.shape[0]
    nw = N // MOE_WIN
    assert nw >= 2 and E % MOE_GROUP == 0 and MOE_TILE <= _SEG_PIECES[0]
    return pl.pallas_call(
        functools.partial(_dispatch_kernel, nw=nw, ne=E),
        grid_spec=pltpu.PrefetchScalarGridSpec(
            num_scalar_prefetch=2,
            grid=(nw,),
            in_specs=[
                pl.BlockSpec((MOE_WIN, D), lambda w, *_: (w, 0)),
                pl.BlockSpec((E, MOE_WIN), lambda w, *_: (0, w)),
            ],
            out_specs=pl.BlockSpec(memory_space=pl.ANY),
            scratch_shapes=[
                pltpu.VMEM((2, E * MOE_CHUNK, D), BF16),
                pltpu.VMEM((MOE_CHUNK, D), BF16),
                pltpu.VMEM((MOE_TILE, D), BF16),
                pltpu.SemaphoreType.DMA((2,)),
                pltpu.SemaphoreType.DMA,
            ],
        ),
        out_shape=jax.ShapeDtypeStruct((E, rows_per_expert, D), BF16),
        compiler_params=_params(("arbitrary",)),
        name="dispatch_tokens",
    )(base.reshape(-1), cnt.reshape(-1), h, slot)


def _expert_kernel(used_ref, x_ref, wg_ref, wu_ref, wd_ref, o_ref):
    @pl.when(pl.program_id(1) * x_ref.shape[0] < used_ref[pl.program_id(0)])
    def _():
        x = x_ref[...]
        a = jnp.dot(x, wg_ref[...], preferred_element_type=F32)
        u = jnp.dot(x, wu_ref[...], preferred_element_type=F32)
        hmid = (a * jax.nn.sigmoid(a) * u).astype(BF16)
        o_ref[...] = jnp.dot(hmid, wd_ref[...], preferred_element_type=F32).astype(o_ref.dtype)


def expert_ffn(xe, used, w_gate, w_up, w_down, layer):
    E, R, D = xe.shape
    F = w_gate.shape[3]
    return pl.pallas_call(
        _expert_kernel,
        grid_spec=pltpu.PrefetchScalarGridSpec(
            num_scalar_prefetch=1,
            grid=(E, R // MOE_TILE),
            in_specs=[
                pl.BlockSpec((None, MOE_TILE, D), lambda e, i, *_: (e, i, 0)),
                pl.BlockSpec((None, None, D, F), lambda e, i, *_: (layer, e, 0, 0)),
                pl.BlockSpec((None, None, D, F), lambda e, i, *_: (layer, e, 0, 0)),
                pl.BlockSpec((None, None, F, D), lambda e, i, *_: (layer, e, 0, 0)),
            ],
            out_specs=pl.BlockSpec((None, MOE_TILE, D), lambda e, i, *_: (e, i, 0)),
        ),
        out_shape=jax.ShapeDtypeStruct((E, R, D), BF16),
        compiler_params=_params(("parallel", "arbitrary")),
        name="expert_ffn",
    )(used, xe, w_gate, w_up, w_down)


def _combine_kernel(base_ref, cnt_ref, lim_ref, x_ref, gt_ref, slot_ref, aff_ref, g_ref, ye_hbm, o_ref,
                    buf, xbuf, sems, xsem, *, nw, ne, final_norm):
    w = pl.program_id(0)
    cur = w % 2
    win = x_ref.shape[0]
    col = lax.broadcasted_iota(jnp.int32, (win, MOE_CHUNK), 1)

    def chunk_start(ww, ee, k):
        b = base_ref[ee * nw + ww]
        return pl.multiple_of(jnp.minimum(b + k * MOE_CHUNK, lim_ref[ee] - MOE_CHUNK), MOE_ALIGN)

    def head_copy(slot_id, ww, ee):
        return pltpu.make_async_copy(ye_hbm.at[ee, pl.ds(chunk_start(ww, ee, 0), MOE_CHUNK)],
                                     buf.at[slot_id, ee * MOE_CHUNK:(ee + 1) * MOE_CHUNK], sems.at[slot_id])

    @pl.when(w == 0)
    def _():
        for ee in range(ne):
            head_copy(cur, w, ee).start()

    @pl.when(w + 1 < nw)
    def _():
        for ee in range(ne):
            head_copy(1 - cur, w + 1, ee).start()

    def weights(ee, k):
        scol = slot_ref[:, ee:ee + 1]
        gate = aff_ref[:, ee:ee + 1]
        if not (isinstance(k, int) and k == 0):
            gate = jnp.where(scol >= base_ref[ee * nw + w] + k * MOE_CHUNK, gate, 0.0)
        return jnp.where(scol == chunk_start(w, ee, k) + col, gate, 0.0).astype(BF16)

    for ee in range(ne):
        head_copy(cur, w, ee).wait()
    o_ref[...] = jnp.dot(jnp.concatenate([weights(ee, 0) for ee in range(ne)], axis=1), buf[cur],
                         preferred_element_type=F32)

    for ee in range(ne):
        def extra(k, carry, ee=ee):
            cp = pltpu.make_async_copy(ye_hbm.at[ee, pl.ds(chunk_start(w, ee, k), MOE_CHUNK)], xbuf, xsem)
            cp.start()
            cp.wait()
            o_ref[...] += jnp.dot(weights(ee, k), xbuf[...], preferred_element_type=F32)
            return carry

        lax.fori_loop(1, (cnt_ref[ee * nw + w] + MOE_CHUNK - 1) // MOE_CHUNK, extra, 0)

    x = x_ref[...] + gt_ref[...] * o_ref[...]
    if final_norm:
        ms = jnp.mean(x * x, axis=-1, keepdims=True)
        x = x * lax.rsqrt(ms + EPS) * g_ref[...]
    o_ref[...] = x


def combine_experts(x, gt, slot_t, aff_t, ye, base, cnt, lim, g_final, seq, *, final_norm):
    N, D = x.shape
    E = slot_t.shape[1]
    nw = N // MOE_WIN
    per_seq = seq // MOE_WIN
    return pl.pallas_call(
        functools.partial(_combine_kernel, nw=nw, ne=E, final_norm=final_norm),
        grid_spec=pltpu.PrefetchScalarGridSpec(
            num_scalar_prefetch=3,
            grid=(nw,),
            in_specs=[
                pl.BlockSpec((MOE_WIN, D), lambda w, *_: (w, 0)),
                pl.BlockSpec((None, 1, D), lambda w, *_: (w // per_seq, 0, 0)),
                pl.BlockSpec((MOE_WIN, E), lambda w, *_: (w, 0)),
                pl.BlockSpec((MOE_WIN, E), lambda w, *_: (w, 0)),
                pl.BlockSpec((1, D), lambda w, *_: (0, 0)),
                pl.BlockSpec(memory_space=pl.ANY),
            ],
            out_specs=pl.BlockSpec((MOE_WIN, D), lambda w, *_: (w, 0)),
            scratch_shapes=[
                pltpu.VMEM((2, E * MOE_CHUNK, D), BF16),
                pltpu.VMEM((MOE_CHUNK, D), BF16),
                pltpu.SemaphoreType.DMA((2,)),
                pltpu.SemaphoreType.DMA,
            ],
        ),
        out_shape=jax.ShapeDtypeStruct((N, D), F32),
        compiler_params=_params(("arbitrary",)),
        name="combine_experts",
    )(base.reshape(-1), cnt.reshape(-1), lim, x, gt, slot_t, aff_t, g_final.reshape(1, D), ye)


def _prepare_layer(l, p, D):
    w_in = p['w_in'][l]
    W = N_HEADS * HEAD_DIM
    KW = GQA_KV_HEADS * HEAD_DIM
    qr = p['mla_w_uq'].shape[1]
    kvr = p['mla_w_ukv'].shape[1]
    o = 0
    seg = {}
    for name, width in (('a', 3 * W), ('cq', qr), ('ckv', kvr), ('kr', MLA_ROPE), ('gq', W), ('gk', KW), ('gv', KW),
                        ('gate', N_BRANCH * D)):
        seg[name] = w_in[:, o:o + width]
        o += width
    gq_cols = np.concatenate([np.arange(h * HEAD_DIM, (h + 1) * HEAD_DIM) for h in GQA_HEAD_ORDER])
    zeros = lambda n: jnp.zeros((D, n), w_in.dtype)
    gv = [p_ for g in range(GQA_KV_HEADS)
          for p_ in (seg['gv'][:, g * HEAD_DIM:(g + 1) * HEAD_DIM], zeros(LANE - HEAD_DIM))]
    w_mix = jnp.concatenate(
        [seg['a'], seg['cq'], seg['gq'][:, gq_cols], seg['ckv'],
         zeros(MLA_NOPE), seg['kr'], zeros(LANE - MLA_NOPE - MLA_ROPE), seg['gk']] + gv, axis=1).astype(BF16)

    w_uq = p['mla_w_uq'][l].reshape(qr, N_HEADS, MLA_NOPE + MLA_ROPE)
    wq = jnp.pad(w_uq, ((0, 0), (0, 0), (0, LANE - MLA_NOPE - MLA_ROPE))).reshape(qr, N_HEADS * LANE).astype(BF16)
    w_ukv = p['mla_w_ukv'][l].reshape(kvr, N_HEADS, MLA_NOPE + MLA_V)
    wkn = jnp.pad(w_ukv[:, :, :MLA_NOPE], ((0, 0), (0, 0), (0, LANE - MLA_NOPE))).reshape(kvr, N_HEADS * LANE)
    wv = jnp.pad(w_ukv[:, :, MLA_NOPE:], ((0, 0), (0, 0), (0, LANE - MLA_V))).reshape(kvr, N_HEADS * LANE)

    w_branch = p['w_branch'][l]
    w_branch = jnp.stack([w_branch[0], w_branch[1], w_branch[2][gq_cols]]).astype(BF16)
    return dict(
        w_mix=w_mix, w_gate_logits=seg['gate'].astype(BF16),
        wq=wq, wkn=wkn.astype(BF16), wv=wv.astype(BF16),
        w_branch=w_branch, w_out=p['w_out'][l].astype(BF16),
        w_router_t=p['w_router'][l].T.astype(BF16),
    )


def _trunk(x3, mod, p, layers, tables):
    B, T, D = x3.shape
    N = B * T
    x = x3.reshape(N, D)
    depth = len(layers)
    for l in range(depth):
        w = layers[l]
        sh1, sc1, gt1, sh2, sc2, gt2 = [mod[l][:, None, i * D:(i + 1) * D] for i in range(6)]
        h1 = norm_modulate(x, p['g_mix'][l], sc1, sh1, T)
        mix = projection(h1, w['w_mix'], tn=w['w_mix'].shape[1] // 2, sigmoid_out=False, name="mix_projection")
        sgate = projection(h1, w['w_gate_logits'], tn=_tile(N_BRANCH * D, 1536), sigmoid_out=True,
                           name="gate_projection")
        o_a = neighborhood_attention(mix, tables['na_bias'][l], B, T)
        qm, km, vm = mla_prepare(mix, p['mla_g_q'][l], p['mla_g_kv'][l], w['wq'], w['wkn'], w['wv'],
                                 tables['mla_q'][T], tables['mla_k'][T], T)
        o_b = flash_attention(qm, km, vm, B, T, shared_k=False, name="mla_attention")
        qg, kg, vg = gqa_prepare(mix, p['gqa_g_q'][l], p['gqa_g_k'][l], tables['gqa_q'][T], tables['gqa_k'][T], T)
        o_c = flash_attention(qg, kg, vg, B, T, shared_k=True, name="gqa_attention")
        x = merge_branches(o_a, o_b, o_c, sgate, x, gt1, w['w_branch'], w['w_out'], T)

        h, aff = router(x, p['g_ffn'][l], sc2, sh2, w['w_router_t'], T)
        cap = EC_CAPACITY * N // N_EXPERTS
        slot, base, cnt = select_tokens(aff, cap)
        used = base[:, -1] + _round_up(cnt[:, -1], MOE_ALIGN)
        rows = _round_up(cap + MOE_ALIGN * (N // MOE_WIN), MOE_TILE)
        xe = dispatch_tokens(h, slot, base, cnt, rows)
        ye = expert_ffn(xe, used, p['w_exp_gate'], p['w_exp_up'], p['w_exp_down'], l)
        x = combine_experts(x, gt2, slot.T, aff.T, ye, base, cnt, _round_up(used, MOE_TILE), p['g_final'], T,
                            final_norm=l == depth - 1)
    return x.reshape(B, T, D)


def kernel(x_prompt, x_sample, c_prompt, c_sample, w_ada, b_ada, g_mix, g_ffn, w_in, na_rel_bias, mla_g_q, mla_w_uq,
           mla_g_kv, mla_w_ukv, gqa_g_q, gqa_g_k, w_branch, w_out, w_router, w_exp_gate, w_exp_up, w_exp_down,
           g_final):
    p = dict(g_mix=g_mix, g_ffn=g_ffn, w_in=w_in, mla_g_q=mla_g_q, mla_w_uq=mla_w_uq, mla_g_kv=mla_g_kv,
             mla_w_ukv=mla_w_ukv, gqa_g_q=gqa_g_q, gqa_g_k=gqa_g_k, w_branch=w_branch, w_out=w_out,
             w_router=w_router, w_exp_gate=w_exp_gate, w_exp_up=w_exp_up, w_exp_down=w_exp_down, g_final=g_final)
    depth, D = g_mix.shape
    bp, bs = c_prompt.shape[0], c_sample.shape[0]
    rows = -(-(bp + bs) // 8) * 8
    c_all = jnp.concatenate([c_prompt, c_sample, jnp.zeros((rows - bp - bs, D), F32)], axis=0)
    mod = ada_modulation(c_all, w_ada, b_ada)
    layers = [_prepare_layer(l, p, D) for l in range(depth)]
    for name in ('w_exp_gate', 'w_exp_up', 'w_exp_down'):
        p[name] = p[name].astype(BF16)
    tables = dict(na_bias=[_na_bias_table(na_rel_bias[l]) for l in range(depth)],
                  mla_q={}, mla_k={}, gqa_q={}, gqa_k={})
    for T in {x_prompt.shape[1], x_sample.shape[1]}:
        tables['mla_q'][T], tables['mla_k'][T] = _mla_tables(T)
        tables['gqa_q'][T], tables['gqa_k'][T] = _gqa_tables(T)
    y_prompt = _trunk(x_prompt, mod[:, :bp], p, layers, tables)
    y_sample = _trunk(x_sample, mod[:, bp:bp + bs], p, layers, tables)
    return (y_prompt, y_sample)
```

```python
import functools
import math

import numpy as np
import jax
import jax.numpy as jnp
from jax import lax
from jax.experimental import pallas as pl
from jax.experimental.pallas import tpu as pltpu

F32 = jnp.float32
BF16 = jnp.bfloat16

GRID_W = 64
HEAD_DIM = 64
N_HEADS = 8
NA_WIN_H = 8
NA_WIN_W = 16
MLA_NOPE = 64
MLA_ROPE = 32
MLA_V = 64
GQA_KV_HEADS = 2
N_BRANCH = 3
N_EXPERTS = 16
EC_CAPACITY = 2
ROPE_THETA = 10000.0
EPS = 1e-6
LOG2E = 1.4426950408889634

LANE = 128
NA_ROWS_PER_BLOCK = 4
NA_BLOCK = NA_ROWS_PER_BLOCK * GRID_W
NA_KEY_BLOCKS = 3
NEG = -1e30
VMEM_LIMIT = 56 * 1024 * 1024

GQA_HEAD_ORDER = (0, 4, 1, 5, 2, 6, 3, 7)


def _params(sem):
    return pltpu.CompilerParams(dimension_semantics=sem, vmem_limit_bytes=VMEM_LIMIT)


def _tile(n, pref):
    t = min(n, pref)
    while n % t:
        t //= 2
    return t


def _ada_kernel(c_ref, w_ref, b_ref, o_ref):
    c = c_ref[...]
    a = (c * jax.nn.sigmoid(c)).astype(BF16)
    o_ref[...] = jnp.dot(a, w_ref[...].astype(BF16), preferred_element_type=F32) + b_ref[...]


def ada_modulation(c, w_ada, b_ada):
    L, D, D6 = w_ada.shape
    R = c.shape[0]
    tn = _tile(D6, 1024)
    return pl.pallas_call(
        _ada_kernel,
        grid=(L, D6 // tn),
        in_specs=[
            pl.BlockSpec((R, D), lambda l, j: (0, 0)),
            pl.BlockSpec((None, D, tn), lambda l, j: (l, 0, j)),
            pl.BlockSpec((None, 1, tn), lambda l, j: (l, 0, j)),
        ],
        out_specs=pl.BlockSpec((None, R, tn), lambda l, j: (l, 0, j)),
        out_shape=jax.ShapeDtypeStruct((L, R, D6), F32),
        compiler_params=_params(("parallel", "parallel")),
        name="ada_modulation",
    )(c, w_ada, b_ada.reshape(L, 1, D6))


def _norm_mod(x, g, sc, sh):
    ms = jnp.mean(x * x, axis=-1, keepdims=True)
    y = x * lax.rsqrt(ms + EPS)
    return (y * g) * (1.0 + sc) + sh


def _norm_mod_kernel(x_ref, g_ref, sc_ref, sh_ref, h_ref):
    h_ref[...] = _norm_mod(x_ref[...], g_ref[...], sc_ref[...], sh_ref[...]).astype(h_ref.dtype)


def norm_modulate(x, g, sc, sh, seq):
    N, D = x.shape
    tm = _tile(seq, 512)
    per_seq = seq // tm
    return pl.pallas_call(
        _norm_mod_kernel,
        grid=(N // tm,),
        in_specs=[
            pl.BlockSpec((tm, D), lambda i: (i, 0)),
            pl.BlockSpec((1, D), lambda i: (0, 0)),
            pl.BlockSpec((None, 1, D), lambda i: (i // per_seq, 0, 0)),
            pl.BlockSpec((None, 1, D), lambda i: (i // per_seq, 0, 0)),
        ],
        out_specs=pl.BlockSpec((tm, D), lambda i: (i, 0)),
        out_shape=jax.ShapeDtypeStruct((N, D), BF16),
        compiler_params=_params(("parallel",)),
        name="norm_modulate",
    )(x, g.reshape(1, D), sc, sh)


def _proj_kernel(h_ref, w_ref, o_ref, *, sigmoid_out):
    acc = jnp.dot(h_ref[...], w_ref[...], preferred_element_type=F32)
    if sigmoid_out:
        acc = jax.nn.sigmoid(acc)
    o_ref[...] = acc.astype(o_ref.dtype)


def projection(h, w, *, tn, sigmoid_out, name):
    N, D = h.shape
    C = w.shape[1]
    tm = _tile(N, 1024)
    return pl.pallas_call(
        functools.partial(_proj_kernel, sigmoid_out=sigmoid_out),
        grid=(N // tm, C // tn),
        in_specs=[
            pl.BlockSpec((tm, D), lambda i, j: (i, 0)),
            pl.BlockSpec((D, tn), lambda i, j: (0, j)),
        ],
        out_specs=pl.BlockSpec((tm, tn), lambda i, j: (i, j)),
        out_shape=jax.ShapeDtypeStruct((N, C), BF16),
        compiler_params=_params(("parallel", "arbitrary")),
        name=name,
    )(h, w)


def _na_bias_table(rel_bias):
    R = NA_ROWS_PER_BLOCK
    KR = NA_KEY_BLOCKS * R
    q_rows = np.stack([np.arange(R), R + np.arange(R), 2 * R + np.arange(R)])
    w_start = np.stack([np.zeros(R, np.int64), np.arange(R), np.full(R, KR - NA_WIN_H)])
    col = np.arange(GRID_W)
    col_start = np.clip(col - NA_WIN_W // 2, 0, GRID_W - NA_WIN_W)
    k_row = np.arange(KR)
    k_col = np.arange(GRID_W)
    row_ok = (k_row[None, None, :] >= w_start[:, :, None]) & (k_row[None, None, :] < w_start[:, :, None] + NA_WIN_H)
    col_ok = (k_col[None, :] >= col_start[:, None]) & (k_col[None, :] < col_start[:, None] + NA_WIN_W)
    dr = k_row[None, None, :] - q_rows[:, :, None] + (NA_WIN_H - 1)
    dc = k_col[None, :] - col[:, None] + (NA_WIN_W - 1)
    H = rel_bias.shape[0]
    pick_c = ((dc[None] == np.arange(2 * NA_WIN_W - 1)[:, None, None]) & col_ok[None]).astype(np.float32)
    by_col = jnp.einsum('hab,bqk->haqk', rel_bias.astype(F32), pick_c, precision=lax.Precision.HIGHEST)
    by_col = jnp.where(col_ok, by_col, NEG)
    masked = jnp.full((H, GRID_W, GRID_W), NEG, F32)
    types = []
    for t in range(3):
        q_blocks = []
        for qr in range(R):
            q_blocks.append(jnp.concatenate(
                [by_col[:, int(dr[t, qr, kr])] if row_ok[t, qr, kr] else masked for kr in range(KR)], axis=2))
        types.append(jnp.concatenate(q_blocks, axis=1))
    return jnp.stack(types)


def _na_kernel(q_ref, k0_ref, k1_ref, k2_ref, v0_ref, v1_ref, v2_ref, bias_ref, o_ref):
    lane = lax.broadcasted_iota(jnp.int32, (NA_BLOCK, LANE), 1)
    lo = lane < HEAD_DIM
    scale = HEAD_DIM ** -0.5
    for u in range(N_HEADS // 2):
        cs = slice(u * LANE, (u + 1) * LANE)
        q = q_ref[:, cs] * scale
        ks = (k0_ref[:, cs], k1_ref[:, cs], k2_ref[:, cs])
        vs = (v0_ref[:, cs], v1_ref[:, cs], v2_ref[:, cs])
        outs = []
        for hh in range(2):
            qh = jnp.where(lo if hh == 0 else ~lo, q, jnp.zeros_like(q))
            s = jnp.concatenate(
                [lax.dot_general(qh, k, (((1,), (1,)), ((), ())), preferred_element_type=F32) for k in ks], axis=1)
            s = s + bias_ref[2 * u + hh]
            m = jnp.max(s, axis=1, keepdims=True)
            p = jnp.exp(s - m)
            l = jnp.sum(p, axis=1, keepdims=True)
            pb = p.astype(BF16)
            o = None
            for j in range(NA_KEY_BLOCKS):
                t = jnp.dot(pb[:, j * NA_BLOCK:(j + 1) * NA_BLOCK], vs[j], preferred_element_type=F32)
                o = t if o is None else o + t
            outs.append(o / l)
        o_ref[:, cs] = jnp.where(lo, outs[0], outs[1]).astype(o_ref.dtype)


def neighborhood_attention(mix, bias_table, batch, seq):
    N = mix.shape[0]
    W = N_HEADS * HEAD_DIM
    nb = seq // NA_BLOCK
    assert seq % NA_BLOCK == 0 and nb >= NA_KEY_BLOCKS

    def kv_spec(colblk, j):
        return pl.BlockSpec(
            (NA_BLOCK, W),
            lambda b, i: (b * nb + jnp.clip(i - 1, 0, nb - NA_KEY_BLOCKS) + j, colblk))

    def bias_idx(b, i):
        return (jnp.where(i == 0, 0, jnp.where(i == nb - 1, 2, 1)), 0, 0, 0)

    return pl.pallas_call(
        _na_kernel,
        grid=(batch, nb),
        in_specs=[pl.BlockSpec((NA_BLOCK, W), lambda b, i: (b * nb + i, 0))]
        + [kv_spec(1, j) for j in range(NA_KEY_BLOCKS)]
        + [kv_spec(2, j) for j in range(NA_KEY_BLOCKS)]
        + [pl.BlockSpec((None, N_HEADS, NA_BLOCK, NA_KEY_BLOCKS * NA_BLOCK), bias_idx)],
        out_specs=pl.BlockSpec((NA_BLOCK, W), lambda b, i: (b * nb + i, 0)),
        out_shape=jax.ShapeDtypeStruct((N, W), BF16),
        compiler_params=_params(("parallel", "arbitrary")),
        name="neighborhood_attention",
    )(mix, mix, mix, mix, mix, mix, mix, bias_table)


def _rope_angles(pos, dim):
    inv_freq = ROPE_THETA ** (-jnp.arange(0, dim, 2, dtype=F32) / dim)
    ang = pos.astype(F32)[:, None] * inv_freq[None, :]
    return jnp.concatenate([ang, ang], axis=-1)


def _rot_tables(cos, sin, scale):
    n = cos.shape[1]
    first = (np.arange(n) % 32) < 16
    sa = jnp.where(first[None, :], -sin, 0.0)
    sb = jnp.where(first[None, :], 0.0, sin)
    return cos * scale, sa * scale, sb * scale


def _mla_tables(seq):
    pos = jnp.arange(seq)
    ang = _rope_angles(pos, MLA_ROPE)
    cos, sin = jnp.cos(ang), jnp.sin(ang)
    pad_hi = LANE - MLA_NOPE - MLA_ROPE

    def place(t, nope_val):
        return jnp.concatenate(
            [jnp.full((seq, MLA_NOPE), nope_val, F32), t, jnp.zeros((seq, pad_hi), F32)], axis=1)

    qs = (MLA_NOPE + MLA_ROPE) ** -0.5 * LOG2E
    qc, qa, qb = _rot_tables(cos, sin, qs)
    kc, ka, kb = _rot_tables(cos, sin, 1.0)
    q_tab = jnp.stack([place(qc, qs), place(qa, 0.0), place(qb, 0.0)])
    k_tab = jnp.stack([place(kc, 0.0), place(ka, 0.0), place(kb, 0.0)])
    return q_tab, k_tab


def _gqa_tables(seq):
    pos = jnp.arange(seq)
    half = HEAD_DIM // 2
    ar = _rope_angles(pos // GRID_W, half)
    ac = _rope_angles(pos % GRID_W, half)
    cos = jnp.concatenate([jnp.cos(ar), jnp.cos(ac)] * 2, axis=1)
    sin = jnp.concatenate([jnp.sin(ar), jnp.sin(ac)] * 2, axis=1)
    qs = HEAD_DIM ** -0.5 * LOG2E
    return jnp.stack(_rot_tables(cos, sin, qs)), jnp.stack(_rot_tables(cos, sin, 1.0))


def _apply_rot(x, tab_ref):
    n = x.shape[1]
    return (x * tab_ref[0] + pltpu.roll(x, n - 16, 1) * tab_ref[1] + pltpu.roll(x, 16, 1) * tab_ref[2])


def _rms(x, g):
    ms = jnp.mean(x * x, axis=-1, keepdims=True)
    return x * lax.rsqrt(ms + EPS) * g


def _mla_prep_kernel(cq_ref, ckv_ref, kr_ref, gq_ref, gkv_ref, wq_ref, wkn_ref, wv_ref, qtab_ref, ktab_ref,
                     q_out, k_out, v_out):
    cqn = _rms(cq_ref[...].astype(F32), gq_ref[...]).astype(BF16)
    ckvn = _rms(ckv_ref[...].astype(F32), gkv_ref[...]).astype(BF16)
    qf = jnp.dot(cqn, wq_ref[...], preferred_element_type=F32)
    kn = jnp.dot(ckvn, wkn_ref[...], preferred_element_type=F32)
    tm = cq_ref.shape[0]
    hi_lane = lax.broadcasted_iota(jnp.int32, (tm, LANE), 1) >= MLA_V
    vf = jnp.dot(ckvn, wv_ref[...], preferred_element_type=F32)
    k_rope = _apply_rot(kr_ref[...].astype(F32), ktab_ref)
    for h in range(N_HEADS):
        cs = slice(h * LANE, (h + 1) * LANE)
        q_out[:, cs] = _apply_rot(qf[:, cs], qtab_ref).astype(q_out.dtype)
        k_out[:, cs] = (kn[:, cs] + k_rope).astype(k_out.dtype)
        v_out[:, cs] = jnp.where(hi_lane, 1.0, vf[:, cs]).astype(v_out.dtype)


def mla_prepare(mix, g_q, g_kv, wq, wkn, wv, q_tab, k_tab, seq):
    N = mix.shape[0]
    tm = _tile(seq, 512)
    per_seq = seq // tm
    qr, kvr = wq.shape[0], wkn.shape[0]
    cq_blk = 1536 // qr
    ckv_blk = 2560 // kvr
    kr_blk = 2816 // LANE
    const = lambda i: (0, 0)
    tab = pl.BlockSpec((3, tm, LANE), lambda i: (0, i % per_seq, 0))
    return pl.pallas_call(
        _mla_prep_kernel,
        grid=(N // tm,),
        in_specs=[
            pl.BlockSpec((tm, qr), lambda i: (i, cq_blk)),
            pl.BlockSpec((tm, kvr), lambda i: (i, ckv_blk)),
            pl.BlockSpec((tm, LANE), lambda i: (i, kr_blk)),
            pl.BlockSpec((1, qr), const),
            pl.BlockSpec((1, kvr), const),
            pl.BlockSpec(wq.shape, const),
            pl.BlockSpec(wkn.shape, const),
            pl.BlockSpec(wv.shape, const),
            tab, tab,
        ],
        out_specs=[
            pl.BlockSpec((tm, N_HEADS * LANE), lambda i: (i, 0)),
            pl.BlockSpec((tm, N_HEADS * LANE), lambda i: (i, 0)),
            pl.BlockSpec((tm, N_HEADS * LANE), lambda i: (i, 0)),
        ],
        out_shape=[jax.ShapeDtypeStruct((N, N_HEADS * LANE), BF16)] * 3,
        compiler_params=_params(("parallel",)),
        name="mla_prepare",
    )(mix, mix, mix, g_q.reshape(1, qr), g_kv.reshape(1, kvr), wq, wkn, wv, q_tab, k_tab)


def _head_rms(x, g, seg_ref):
    sq = x * x
    hi = sq.astype(BF16)
    lo = (sq - hi.astype(F32)).astype(BF16)
    ms = (jnp.dot(hi, seg_ref[...], preferred_element_type=F32)
          + jnp.dot(lo, seg_ref[...], preferred_element_type=F32)) * (1.0 / HEAD_DIM)
    return x * lax.rsqrt(ms + EPS) * g


def _gqa_prep_kernel(q_ref, k_ref, v_ref, gq_ref, gk_ref, segq_ref, segk_ref, qtab_ref, ktab_ref,
                     q_out, k_out, v_out):
    tm = q_ref.shape[0]
    lane = lax.broadcasted_iota(jnp.int32, (tm, LANE), 1)
    lo = lane < HEAD_DIM
    for g in range(GQA_KV_HEADS):
        cs = slice(g * LANE, (g + 1) * LANE)
        v_out[:, cs] = jnp.where(lo, v_ref[:, cs], jnp.ones((tm, LANE), v_ref.dtype))
    qn = _head_rms(q_ref[...].astype(F32), gq_ref[...], segq_ref)
    for u in range(N_HEADS // 2):
        y = _apply_rot(qn[:, u * LANE:(u + 1) * LANE], qtab_ref)
        q_out[:, (2 * u) * LANE:(2 * u + 1) * LANE] = jnp.where(lo, y, 0.0).astype(q_out.dtype)
        q_out[:, (2 * u + 1) * LANE:(2 * u + 2) * LANE] = jnp.where(lo, 0.0, y).astype(q_out.dtype)
    kn = _head_rms(k_ref[...].astype(F32), gk_ref[...], segk_ref)
    k_out[...] = _apply_rot(kn, ktab_ref).astype(k_out.dtype)


def gqa_prepare(mix, g_q, g_k, q_tab, k_tab, seq):
    N = mix.shape[0]
    tm = _tile(seq, 512)
    per_seq = seq // tm
    W = N_HEADS * HEAD_DIM
    KW = GQA_KV_HEADS * HEAD_DIM
    VW = GQA_KV_HEADS * LANE
    const = lambda i: (0, 0)
    seg = lambda n: jnp.asarray(np.kron(np.eye(n // HEAD_DIM), np.ones((HEAD_DIM, HEAD_DIM))), BF16)
    tab = pl.BlockSpec((3, tm, LANE), lambda i: (0, i % per_seq, 0))
    return pl.pallas_call(
        _gqa_prep_kernel,
        grid=(N // tm,),
        in_specs=[
            pl.BlockSpec((tm, W), lambda i: (i, 2048 // W)),
            pl.BlockSpec((tm, KW), lambda i: (i, 2944 // KW)),
            pl.BlockSpec((tm, VW), lambda i: (i, 3072 // VW)),
            pl.BlockSpec((1, W), const),
            pl.BlockSpec((1, KW), const),
            pl.BlockSpec((W, W), const),
            pl.BlockSpec((KW, KW), const),
            tab, tab,
        ],
        out_specs=[
            pl.BlockSpec((tm, N_HEADS * LANE), lambda i: (i, 0)),
            pl.BlockSpec((tm, KW), lambda i: (i, 0)),
            pl.BlockSpec((tm, VW), lambda i: (i, 0)),
        ],
        out_shape=[
            jax.ShapeDtypeStruct((N, N_HEADS * LANE), BF16),
            jax.ShapeDtypeStruct((N, KW), BF16),
            jax.ShapeDtypeStruct((N, VW), BF16),
        ],
        compiler_params=_params(("parallel",)),
        name="gqa_prepare",
    )(mix, mix, mix, jnp.tile(g_q, N_HEADS).reshape(1, W), jnp.tile(g_k, GQA_KV_HEADS).reshape(1, KW),
      seg(W), seg(KW), q_tab, k_tab)


def _flash_kernel(q_ref, k_ref, v_ref, o_ref, m_sc, acc_sc, *, shared_k, v_groups):
    kv = pl.program_id(2)
    tq = q_ref.shape[0]
    tk = k_ref.shape[0]

    @pl.when(kv == 0)
    def _():
        m_sc[...] = jnp.full(m_sc.shape, -jnp.inf, F32)
        acc_sc[...] = jnp.zeros(acc_sc.shape, F32)

    for h in range(N_HEADS):
        q = q_ref[:, h * LANE:(h + 1) * LANE]
        k = k_ref[...] if shared_k else k_ref[:, h * LANE:(h + 1) * LANE]
        g = h % v_groups
        v = v_ref[:, g * LANE:(g + 1) * LANE]
        s = lax.dot_general(q, k, (((1,), (1,)), ((), ())), preferred_element_type=F32)
        m_prev = m_sc[h]
        m_new = jnp.maximum(m_prev, jnp.max(s, axis=1, keepdims=True))
        alpha = jnp.exp2(m_prev - m_new)
        p = jnp.exp2(s - jnp.concatenate([m_new] * (tk // LANE), axis=1))
        acc_sc[h] = alpha * acc_sc[h] + jnp.dot(p.astype(BF16), v, preferred_element_type=F32)
        m_sc[h] = m_new

    @pl.when(kv == pl.num_programs(2) - 1)
    def _():
        lane = lax.broadcasted_iota(jnp.int32, (tq, LANE), 1)
        lo = lane < HEAD_DIM
        for u in range(N_HEADS // 2):
            a = acc_sc[2 * u]
            b = acc_sc[2 * u + 1]
            a = a / pltpu.roll(a, HEAD_DIM, 1)
            b = pltpu.roll(b, HEAD_DIM, 1) / b
            o_ref[:, u * LANE:(u + 1) * LANE] = jnp.where(lo, a, b).astype(o_ref.dtype)


def flash_attention(q, k, v, batch, seq, *, shared_k, name):
    N = q.shape[0]
    tq = _tile(seq, 1024)
    tk = _tile(seq, 2048)
    nq, nk = seq // tq, seq // tk
    kw, vw = k.shape[1], v.shape[1]
    ow = N_HEADS * HEAD_DIM
    return pl.pallas_call(
        functools.partial(_flash_kernel, shared_k=shared_k, v_groups=vw // LANE),
        grid=(batch, nq, nk),
        in_specs=[
            pl.BlockSpec((tq, N_HEADS * LANE), lambda b, i, j: (b * nq + i, 0)),
            pl.BlockSpec((tk, kw), lambda b, i, j: (b * nk + j, 0)),
            pl.BlockSpec((tk, vw), lambda b, i, j: (b * nk + j, 0)),
        ],
        out_specs=pl.BlockSpec((tq, ow), lambda b, i, j: (b * nq + i, 0)),
        out_shape=jax.ShapeDtypeStruct((N, ow), BF16),
        scratch_shapes=[pltpu.VMEM((N_HEADS, tq, LANE), F32)] * 2,
        compiler_params=_params(("parallel", "parallel", "arbitrary")),
        name=name,
    )(q, k, v)


def _merge_kernel(oa_ref, ob_ref, oc_ref, sg_ref, x_ref, gt_ref, wb_ref, wo_ref, o_ref):
    D = x_ref.shape[1]
    merged = None
    for i, o_r in enumerate((oa_ref, ob_ref, oc_ref)):
        t = jnp.dot(o_r[...], wb_ref[i], preferred_element_type=F32)
        t = sg_ref[:, i * D:(i + 1) * D].astype(F32) * t
        merged = t if merged is None else merged + t
    mix = jnp.dot(merged.astype(BF16), wo_ref[...], preferred_element_type=F32)
    o_ref[...] = x_ref[...] + gt_ref[...] * mix


def merge_branches(o_a, o_b, o_c, sgate, x, gt, w_branch, w_out, seq):
    N, D = x.shape
    W = o_a.shape[1]
    tm = _tile(seq, 256)
    per_seq = seq // tm
    row = lambda i: (i, 0)
    return pl.pallas_call(
        _merge_kernel,
        grid=(N // tm,),
        in_specs=[
            pl.BlockSpec((tm, W), row), pl.BlockSpec((tm, W), row), pl.BlockSpec((tm, W), row),
            pl.BlockSpec((tm, N_BRANCH * D), row),
            pl.BlockSpec((tm, D), row),
            pl.BlockSpec((None, 1, D), lambda i: (i // per_seq, 0, 0)),
            pl.BlockSpec((N_BRANCH, W, D), lambda i: (0, 0, 0)),
            pl.BlockSpec((D, D), lambda i: (0, 0)),
        ],
        out_specs=pl.BlockSpec((tm, D), row),
        out_shape=jax.ShapeDtypeStruct((N, D), F32),
        compiler_params=_params(("parallel",)),
        name="merge_branches",
    )(o_a, o_b, o_c, sgate, x, gt, w_branch, w_out)


def _router_kernel(x_ref, g_ref, sc_ref, sh_ref, wr_ref, h_out, aff_out):
    h = _norm_mod(x_ref[...], g_ref[...], sc_ref[...], sh_ref[...]).astype(BF16)
    h_out[...] = h
    logits = lax.dot_general(wr_ref[...], h, (((1,), (1,)), ((), ())), preferred_element_type=F32)
    m = jnp.max(logits, axis=0, keepdims=True)
    p = jnp.exp(logits - m)
    aff_out[...] = p / jnp.sum(p, axis=0, keepdims=True)


def router(x, g, sc, sh, w_router_t, seq):
    N, D = x.shape
    E = w_router_t.shape[0]
    tm = _tile(seq, 512)
    per_seq = seq // tm
    return pl.pallas_call(
        _router_kernel,
        grid=(N // tm,),
        in_specs=[
            pl.BlockSpec((tm, D), lambda i: (i, 0)),
            pl.BlockSpec((1, D), lambda i: (0, 0)),
            pl.BlockSpec((None, 1, D), lambda i: (i // per_seq, 0, 0)),
            pl.BlockSpec((None, 1, D), lambda i: (i // per_seq, 0, 0)),
            pl.BlockSpec((E, D), lambda i: (0, 0)),
        ],
        out_specs=[pl.BlockSpec((tm, D), lambda i: (i, 0)), pl.BlockSpec((E, tm), lambda i: (0, i))],
        out_shape=[jax.ShapeDtypeStruct((N, D), BF16), jax.ShapeDtypeStruct((E, N), F32)],
        compiler_params=_params(("parallel",)),
        name="router",
    )(x, g.reshape(1, D), sc, sh, w_router_t)


MOE_WIN = 256
MOE_ALIGN = 16
MOE_CHUNK = 64
MOE_TILE = 512
_SEG_PIECES = (512, 256, 128, 64, 32, 16)


def _round_up(x, m):
    return (x + (m - 1)) // m * m


def _select_kernel(aff_ref, slot_ref, wbase_ref, wcnt_ref, *, cap, chunks_per_win):
    a = aff_ref[...]
    nc = a.shape[0]
    bits = pltpu.bitcast(a, jnp.int32)

    def search(i, t):
        cand = t | jnp.left_shift(jnp.int32(1), 30 - i)
        n_ge = jnp.sum((bits >= cand).astype(jnp.int32))
        return jnp.where(n_ge >= cap, cand, t)

    thr = lax.fori_loop(0, 31, search, jnp.int32(0))
    gt = bits > thr
    eq = bits == thr
    need = (cap - jnp.sum(gt.astype(jnp.int32))).astype(F32)

    li = lax.broadcasted_iota(jnp.int32, (LANE, LANE), 0)
    lj = lax.broadcasted_iota(jnp.int32, (LANE, LANE), 1)
    tri = jnp.where(li <= lj, 1.0, 0.0).astype(BF16)
    ones = jnp.ones((LANE, LANE), BF16)
    ci = lax.broadcasted_iota(jnp.int32, (nc, nc), 0)
    cj = lax.broadcasted_iota(jnp.int32, (nc, nc), 1)
    shift = chunks_per_win.bit_length() - 1
    wi, wj = jnp.right_shift(ci, shift), jnp.right_shift(cj, shift)
    as_bf = lambda m: jnp.where(m, 1.0, 0.0).astype(BF16)
    dot = lambda x, y: jnp.dot(x, y, preferred_element_type=F32)

    eq_b = as_bf(eq)
    eq_rank = dot(as_bf(cj < ci), dot(eq_b, ones).astype(BF16)) + dot(eq_b, tri) - eq_b.astype(F32)
    sel = gt | (eq & (eq_rank < need))
    sel_b = as_bf(sel)
    incl = dot(sel_b, tri)
    tot = dot(sel_b, ones).astype(BF16)
    in_win = dot(as_bf((cj < ci) & (wj == wi)), tot)
    wcnt = dot(as_bf(wj == wi), tot)
    wpad = jnp.ceil(wcnt * (1.0 / MOE_ALIGN)) * MOE_ALIGN
    first = (cj & (chunks_per_win - 1)) == 0
    wbase = dot(as_bf((wj < wi) & first), wpad.astype(BF16))
    slot = wbase + in_win + incl - sel_b.astype(F32)
    slot_ref[...] = jnp.where(sel, slot, -1.0).astype(jnp.int32)
    wbase_ref[...] = wbase.astype(jnp.int32)
    wcnt_ref[...] = wcnt.astype(jnp.int32)


def select_tokens(aff, cap):
    E, N = aff.shape
    nc = N // LANE
    cpw = MOE_WIN // LANE
    blk = pl.BlockSpec((None, nc, LANE), lambda e: (e, 0, 0))
    out = jax.ShapeDtypeStruct((E, nc, LANE), jnp.int32)
    slot, wbase, wcnt = pl.pallas_call(
        functools.partial(_select_kernel, cap=cap, chunks_per_win=cpw),
        grid=(E,),
        in_specs=[blk],
        out_specs=[blk, blk, blk],
        out_shape=[out, out, out],
        compiler_params=_params(("parallel",)),
        name="select_tokens",
    )(aff.reshape(E, nc, LANE))
    return slot.reshape(E, N), wbase[:, ::cpw, 0], wcnt[:, ::cpw, 0]


def _for_segment_pieces(n_rows, make_copy, act):
    off = jnp.int32(0)
    for piece in _SEG_PIECES:
        has = (n_rows & piece) != 0

        @pl.when(has)
        def _(off=off, piece=piece):
            act(make_copy(pl.multiple_of(off, MOE_ALIGN), piece))

        off = off + jnp.where(has, piece, 0)


MOE_GROUP = 8


def _dispatch_kernel(base_ref, cnt_ref, h_ref, slot_ref, xe_hbm, stage, xstage, zero_sc, sems, xsem, *, nw, ne):
    w = pl.program_id(0)
    cur = w % 2
    win = h_ref.shape[0]
    row_id = lax.broadcasted_iota(jnp.int32, (MOE_CHUNK, win), 0)

    def seg_rows(ww, ee):
        return _round_up(cnt_ref[ee * nw + ww], MOE_ALIGN)

    def head_copy(buf, ww, ee):
        b = base_ref[ee * nw + ww]

        def make(off, piece):
            return pltpu.make_async_copy(stage.at[buf, pl.ds(pl.multiple_of(ee * MOE_CHUNK + off, MOE_ALIGN), piece)],
                                         xe_hbm.at[ee, pl.ds(pl.multiple_of(b + off, MOE_ALIGN), piece)],
                                         sems.at[buf])
        return make

    def for_heads(buf, ww, act):
        for ee in range(ne):
            _for_segment_pieces(jnp.minimum(seg_rows(ww, ee), MOE_CHUNK), head_copy(buf, ww, ee), act)

    @pl.when(w == 0)
    def _():
        zero_sc[...] = jnp.zeros(zero_sc.shape, zero_sc.dtype)

    @pl.when(w >= 2)
    def _():
        for_heads(cur, w - 2, lambda c: c.wait())

    h = h_ref[...]

    def onehot(ee, k):
        want = base_ref[ee * nw + w] + k * MOE_CHUNK + row_id
        return jnp.where(slot_ref[ee:ee + 1, :] == want, 1.0, 0.0).astype(BF16)

    for g in range(ne // MOE_GROUP):
        sel = jnp.concatenate([onehot(g * MOE_GROUP + i, 0) for i in range(MOE_GROUP)], axis=0)
        stage[cur, g * MOE_GROUP * MOE_CHUNK:(g + 1) * MOE_GROUP * MOE_CHUNK, :] = jnp.dot(
            sel, h, preferred_element_type=F32).astype(stage.dtype)
    for_heads(cur, w, lambda c: c.start())

    for ee in range(ne):
        rows = seg_rows(w, ee)
        b = base_ref[ee * nw + w]

        def tail(k, carry, ee=ee, rows=rows, b=b):
            xstage[...] = jnp.dot(onehot(ee, k), h, preferred_element_type=F32).astype(xstage.dtype)

            def make(off, piece):
                return pltpu.make_async_copy(
                    xstage.at[pl.ds(off, piece)],
                    xe_hbm.at[ee, pl.ds(pl.multiple_of(b + k * MOE_CHUNK + off, MOE_ALIGN), piece)], xsem)
            n = jnp.minimum(rows - k * MOE_CHUNK, MOE_CHUNK)
            _for_segment_pieces(n, make, lambda c: c.start())
            _for_segment_pieces(n, make, lambda c: c.wait())
            return carry

        lax.fori_loop(1, (rows + MOE_CHUNK - 1) // MOE_CHUNK, tail, 0)

    @pl.when(w == nw - 1)
    def _():
        for ee in range(ne):
            used = base_ref[ee * nw + w] + seg_rows(w, ee)
            fill = _round_up(used, MOE_TILE) - used

            def zmake(off, piece, ee=ee, used=used):
                return pltpu.make_async_copy(zero_sc.at[pl.ds(0, piece)],
                                             xe_hbm.at[ee, pl.ds(pl.multiple_of(used + off, MOE_ALIGN), piece)], xsem)
            _for_segment_pieces(fill, zmake, lambda c: c.start())
            _for_segment_pieces(fill, zmake, lambda c: c.wait())
        for_heads(cur, w, lambda c: c.wait())
        for_heads(1 - cur, w - 1, lambda c: c.wait())


def dispatch_tokens(h, slot, base, cnt, rows_per_expert):
    N, D = h.shape
    E = slot.shape[0]
    nw = N // MOE_WIN
    assert nw >= 2 and E % MOE_GROUP == 0 and MOE_TILE <= _SEG_PIECES[0]
    return pl.pallas_call(
        functools.partial(_dispatch_kernel, nw=nw, ne=E),
        grid_spec=pltpu.PrefetchScalarGridSpec(
            num_scalar_prefetch=2,
            grid=(nw,),
            in_specs=[
                pl.BlockSpec((MOE_WIN, D), lambda w, *_: (w, 0)),
                pl.BlockSpec((E, MOE_WIN), lambda w, *_: (0, w)),
            ],
            out_specs=pl.BlockSpec(memory_space=pl.ANY),
            scratch_shapes=[
                pltpu.VMEM((2, E * MOE_CHUNK, D), BF16),
                pltpu.VMEM((MOE_CHUNK, D), BF16),
                pltpu.VMEM((MOE_TILE, D), BF16),
                pltpu.SemaphoreType.DMA((2,)),
                pltpu.SemaphoreType.DMA,
            ],
        ),
        out_shape=jax.ShapeDtypeStruct((E, rows_per_expert, D), BF16),
        compiler_params=_params(("arbitrary",)),
        name="dispatch_tokens",
    )(base.reshape(-1), cnt.reshape(-1), h, slot)


def _expert_kernel(used_ref, x_ref, wg_ref, wu_ref, wd_ref, o_ref):
    @pl.when(pl.program_id(1) * x_ref.shape[0] < used_ref[pl.program_id(0)])
    def _():
        x = x_ref[...]
        a = jnp.dot(x, wg_ref[...], preferred_element_type=F32)
        u = jnp.dot(x, wu_ref[...], preferred_element_type=F32)
        hmid = (a * jax.nn.sigmoid(a) * u).astype(BF16)
        o_ref[...] = jnp.dot(hmid, wd_ref[...], preferred_element_type=F32).astype(o_ref.dtype)


def expert_ffn(xe, used, w_gate, w_up, w_down, layer):
    E, R, D = xe.shape
    F = w_gate.shape[3]
    return pl.pallas_call(
        _expert_kernel,
        grid_spec=pltpu.PrefetchScalarGridSpec(
            num_scalar_prefetch=1,
            grid=(E, R // MOE_TILE),
            in_specs=[
                pl.BlockSpec((None, MOE_TILE, D), lambda e, i, *_: (e, i, 0)),
                pl.BlockSpec((None, None, D, F), lambda e, i, *_: (layer, e, 0, 0)),
                pl.BlockSpec((None, None, D, F), lambda e, i, *_: (layer, e, 0, 0)),
                pl.BlockSpec((None, None, F, D), lambda e, i, *_: (layer, e, 0, 0)),
            ],
            out_specs=pl.BlockSpec((None, MOE_TILE, D), lambda e, i, *_: (e, i, 0)),
        ),
        out_shape=jax.ShapeDtypeStruct((E, R, D), BF16),
        compiler_params=_params(("parallel", "arbitrary")),
        name="expert_ffn",
    )(used, xe, w_gate, w_up, w_down)


def _combine_kernel(base_ref, cnt_ref, lim_ref, x_ref, gt_ref, slot_ref, aff_ref, g_ref, ye_hbm, o_ref,
                    buf, xbuf, sems, xsem, *, nw, ne, final_norm):
    w = pl.program_id(0)
    cur = w % 2
    win = x_ref.shape[0]
    col = lax.broadcasted_iota(jnp.int32, (win, MOE_CHUNK), 1)

    def chunk_start(ww, ee, k):
        b = base_ref[ee * nw + ww]
        return pl.multiple_of(jnp.minimum(b + k * MOE_CHUNK, lim_ref[ee] - MOE_CHUNK), MOE_ALIGN)

    def head_copy(slot_id, ww, ee):
        return pltpu.make_async_copy(ye_hbm.at[ee, pl.ds(chunk_start(ww, ee, 0), MOE_CHUNK)],
                                     buf.at[slot_id, ee * MOE_CHUNK:(ee + 1) * MOE_CHUNK], sems.at[slot_id])

    @pl.when(w == 0)
    def _():
        for ee in range(ne):
            head_copy(cur, w, ee).start()

    @pl.when(w + 1 < nw)
    def _():
        for ee in range(ne):
            head_copy(1 - cur, w + 1, ee).start()

    def weights(ee, k):
        scol = slot_ref[:, ee:ee + 1]
        gate = aff_ref[:, ee:ee + 1]
        if not (isinstance(k, int) and k == 0):
            gate = jnp.where(scol >= base_ref[ee * nw + w] + k * MOE_CHUNK, gate, 0.0)
        return jnp.where(scol == chunk_start(w, ee, k) + col, gate, 0.0).astype(BF16)

    lane = lax.broadcasted_iota(jnp.int32, (win, LANE), 1)
    per_group = LANE // MOE_CHUNK

    def head_weights(gi):
        e0 = gi * per_group
        scol, gate, start = slot_ref[:, e0:e0 + 1], aff_ref[:, e0:e0 + 1], chunk_start(w, e0, 0)
        for j in range(1, per_group):
            upper = lane >= j * MOE_CHUNK
            scol = jnp.where(upper, slot_ref[:, e0 + j:e0 + j + 1], scol)
            gate = jnp.where(upper, aff_ref[:, e0 + j:e0 + j + 1], gate)
            start = jnp.where(upper, chunk_start(w, e0 + j, 0) - j * MOE_CHUNK, start)
        return jnp.where(scol == start + lane, gate, 0.0).astype(BF16)

    for ee in range(ne):
        head_copy(cur, w, ee).wait()
    o_ref[...] = jnp.dot(jnp.concatenate([head_weights(gi) for gi in range(ne // per_group)], axis=1), buf[cur],
                         preferred_element_type=F32)

    for ee in range(ne):
        def extra(k, carry, ee=ee):
            cp = pltpu.make_async_copy(ye_hbm.at[ee, pl.ds(chunk_start(w, ee, k), MOE_CHUNK)], xbuf, xsem)
            cp.start()
            cp.wait()
            o_ref[...] += jnp.dot(weights(ee, k), xbuf[...], preferred_element_type=F32)
            return carry

        lax.fori_loop(1, (cnt_ref[ee * nw + w] + MOE_CHUNK - 1) // MOE_CHUNK, extra, 0)

    x = x_ref[...] + gt_ref[...] * o_ref[...]
    if final_norm:
        ms = jnp.mean(x * x, axis=-1, keepdims=True)
        x = x * lax.rsqrt(ms + EPS) * g_ref[...]
    o_ref[...] = x


def combine_experts(x, gt, slot_t, aff_t, ye, base, cnt, lim, g_final, seq, *, final_norm):
    N, D = x.shape
    E = slot_t.shape[1]
    nw = N // MOE_WIN
    per_seq = seq // MOE_WIN
    return pl.pallas_call(
        functools.partial(_combine_kernel, nw=nw, ne=E, final_norm=final_norm),
        grid_spec=pltpu.PrefetchScalarGridSpec(
            num_scalar_prefetch=3,
            grid=(nw,),
            in_specs=[
                pl.BlockSpec((MOE_WIN, D), lambda w, *_: (w, 0)),
                pl.BlockSpec((None, 1, D), lambda w, *_: (w // per_seq, 0, 0)),
                pl.BlockSpec((MOE_WIN, E), lambda w, *_: (w, 0)),
                pl.BlockSpec((MOE_WIN, E), lambda w, *_: (w, 0)),
                pl.BlockSpec((1, D), lambda w, *_: (0, 0)),
                pl.BlockSpec(memory_space=pl.ANY),
            ],
            out_specs=pl.BlockSpec((MOE_WIN, D), lambda w, *_: (w, 0)),
            scratch_shapes=[
                pltpu.VMEM((2, E * MOE_CHUNK, D), BF16),
                pltpu.VMEM((MOE_CHUNK, D), BF16),
                pltpu.SemaphoreType.DMA((2,)),
                pltpu.SemaphoreType.DMA,
            ],
        ),
        out_shape=jax.ShapeDtypeStruct((N, D), F32),
        compiler_params=_params(("arbitrary",)),
        name="combine_experts",
    )(base.reshape(-1), cnt.reshape(-1), lim, x, gt, slot_t, aff_t, g_final.reshape(1, D), ye)


def _prepare_layer(l, p, D):
    w_in = p['w_in'][l]
    W = N_HEADS * HEAD_DIM
    KW = GQA_KV_HEADS * HEAD_DIM
    qr = p['mla_w_uq'].shape[1]
    kvr = p['mla_w_ukv'].shape[1]
    o = 0
    seg = {}
    for name, width in (('a', 3 * W), ('cq', qr), ('ckv', kvr), ('kr', MLA_ROPE), ('gq', W), ('gk', KW), ('gv', KW),
                        ('gate', N_BRANCH * D)):
        seg[name] = w_in[:, o:o + width]
        o += width
    gq_cols = np.concatenate([np.arange(h * HEAD_DIM, (h + 1) * HEAD_DIM) for h in GQA_HEAD_ORDER])
    zeros = lambda n: jnp.zeros((D, n), w_in.dtype)
    gv = [p_ for g in range(GQA_KV_HEADS)
          for p_ in (seg['gv'][:, g * HEAD_DIM:(g + 1) * HEAD_DIM], zeros(LANE - HEAD_DIM))]
    w_mix = jnp.concatenate(
        [seg['a'], seg['cq'], seg['gq'][:, gq_cols], seg['ckv'],
         zeros(MLA_NOPE), seg['kr'], zeros(LANE - MLA_NOPE - MLA_ROPE), seg['gk']] + gv, axis=1).astype(BF16)

    w_uq = p['mla_w_uq'][l].reshape(qr, N_HEADS, MLA_NOPE + MLA_ROPE)
    wq = jnp.pad(w_uq, ((0, 0), (0, 0), (0, LANE - MLA_NOPE - MLA_ROPE))).reshape(qr, N_HEADS * LANE).astype(BF16)
    w_ukv = p['mla_w_ukv'][l].reshape(kvr, N_HEADS, MLA_NOPE + MLA_V)
    wkn = jnp.pad(w_ukv[:, :, :MLA_NOPE], ((0, 0), (0, 0), (0, LANE - MLA_NOPE))).reshape(kvr, N_HEADS * LANE)
    wv = jnp.pad(w_ukv[:, :, MLA_NOPE:], ((0, 0), (0, 0), (0, LANE - MLA_V))).reshape(kvr, N_HEADS * LANE)

    w_branch = p['w_branch'][l]
    w_branch = jnp.stack([w_branch[0], w_branch[1], w_branch[2][gq_cols]]).astype(BF16)
    return dict(
        w_mix=w_mix, w_gate_logits=seg['gate'].astype(BF16),
        wq=wq, wkn=wkn.astype(BF16), wv=wv.astype(BF16),
        w_branch=w_branch, w_out=p['w_out'][l].astype(BF16),
        w_router_t=p['w_router'][l].T.astype(BF16),
    )


def _trunk(x3, mod, p, layers, tables):
    B, T, D = x3.shape
    N = B * T
    x = x3.reshape(N, D)
    depth = len(layers)
    for l in range(depth):
        w = layers[l]
        sh1, sc1, gt1, sh2, sc2, gt2 = [mod[l][:, None, i * D:(i + 1) * D] for i in range(6)]
        h1 = norm_modulate(x, p['g_mix'][l], sc1, sh1, T)
        mix = projection(h1, w['w_mix'], tn=w['w_mix'].shape[1] // 2, sigmoid_out=False, name="mix_projection")
        sgate = projection(h1, w['w_gate_logits'], tn=_tile(N_BRANCH * D, 1536), sigmoid_out=True,
                           name="gate_projection")
        o_a = neighborhood_attention(mix, tables['na_bias'][l], B, T)
        qm, km, vm = mla_prepare(mix, p['mla_g_q'][l], p['mla_g_kv'][l], w['wq'], w['wkn'], w['wv'],
                                 tables['mla_q'][T], tables['mla_k'][T], T)
        o_b = flash_attention(qm, km, vm, B, T, shared_k=False, name="mla_attention")
        qg, kg, vg = gqa_prepare(mix, p['gqa_g_q'][l], p['gqa_g_k'][l], tables['gqa_q'][T], tables['gqa_k'][T], T)
        o_c = flash_attention(qg, kg, vg, B, T, shared_k=True, name="gqa_attention")
        x = merge_branches(o_a, o_b, o_c, sgate, x, gt1, w['w_branch'], w['w_out'], T)

        h, aff = router(x, p['g_ffn'][l], sc2, sh2, w['w_router_t'], T)
        cap = EC_CAPACITY * N // N_EXPERTS
        slot, base, cnt = select_tokens(aff, cap)
        used = base[:, -1] + _round_up(cnt[:, -1], MOE_ALIGN)
        rows = _round_up(cap + MOE_ALIGN * (N // MOE_WIN), MOE_TILE)
        xe = dispatch_tokens(h, slot, base, cnt, rows)
        ye = expert_ffn(xe, used, p['w_exp_gate'], p['w_exp_up'], p['w_exp_down'], l)
        x = combine_experts(x, gt2, slot.T, aff.T, ye, base, cnt, _round_up(used, MOE_TILE), p['g_final'], T,
                            final_norm=l == depth - 1)
    return x.reshape(B, T, D)


def kernel(x_prompt, x_sample, c_prompt, c_sample, w_ada, b_ada, g_mix, g_ffn, w_in, na_rel_bias, mla_g_q, mla_w_uq,
           mla_g_kv, mla_w_ukv, gqa_g_q, gqa_g_k, w_branch, w_out, w_router, w_exp_gate, w_exp_up, w_exp_down,
           g_final):
    p = dict(g_mix=g_mix, g_ffn=g_ffn, w_in=w_in, mla_g_q=mla_g_q, mla_w_uq=mla_w_uq, mla_g_kv=mla_g_kv,
             mla_w_ukv=mla_w_ukv, gqa_g_q=gqa_g_q, gqa_g_k=gqa_g_k, w_branch=w_branch, w_out=w_out,
             w_router=w_router, w_exp_gate=w_exp_gate, w_exp_up=w_exp_up, w_exp_down=w_exp_down, g_final=g_final)
    depth, D = g_mix.shape
    bp, bs = c_prompt.shape[0], c_sample.shape[0]
    rows = -(-(bp + bs) // 8) * 8
    c_all = jnp.concatenate([c_prompt, c_sample, jnp.zeros((rows - bp - bs, D), F32)], axis=0)
    mod = ada_modulation(c_all, w_ada, b_ada)
    layers = [_prepare_layer(l, p, D) for l in range(depth)]
    for name in ('w_exp_gate', 'w_exp_up', 'w_exp_down'):
        p[name] = p[name].astype(BF16)
    tables = dict(na_bias=[_na_bias_table(na_rel_bias[l]) for l in range(depth)],
                  mla_q={}, mla_k={}, gqa_q={}, gqa_k={})
    for T in {x_prompt.shape[1], x_sample.shape[1]}:
        tables['mla_q'][T], tables['mla_k'][T] = _mla_tables(T)
        tables['gqa_q'][T], tables['gqa_k'][T] = _gqa_tables(T)
    y_prompt = _trunk(x_prompt, mod[:, :bp], p, layers, tables)
    y_sample = _trunk(x_sample, mod[:, bp:bp + bs], p, layers, tables)
    return (y_prompt, y_sample)
```

```python
import functools
import math

import numpy as np
import jax
import jax.numpy as jnp
from jax import lax
from jax.experimental import pallas as pl
from jax.experimental.pallas import tpu as pltpu

F32 = jnp.float32
BF16 = jnp.bfloat16

GRID_W = 64
HEAD_DIM = 64
N_HEADS = 8
NA_WIN_H = 8
NA_WIN_W = 16
MLA_NOPE = 64
MLA_ROPE = 32
MLA_V = 64
GQA_KV_HEADS = 2
N_BRANCH = 3
N_EXPERTS = 16
EC_CAPACITY = 2
ROPE_THETA = 10000.0
EPS = 1e-6
LOG2E = 1.4426950408889634

LANE = 128
NA_ROWS_PER_BLOCK = 4
NA_BLOCK = NA_ROWS_PER_BLOCK * GRID_W
NA_KEY_BLOCKS = 3
NEG = -1e30
VMEM_LIMIT = 56 * 1024 * 1024

GQA_HEAD_ORDER = (0, 4, 1, 5, 2, 6, 3, 7)


def _params(sem):
    return pltpu.CompilerParams(dimension_semantics=sem, vmem_limit_bytes=VMEM_LIMIT)


def _tile(n, pref):
    t = min(n, pref)
    while n % t:
        t //= 2
    return t


def _ada_kernel(c_ref, w_ref, b_ref, o_ref):
    c = c_ref[...]
    a = (c * jax.nn.sigmoid(c)).astype(BF16)
    o_ref[...] = jnp.dot(a, w_ref[...].astype(BF16), preferred_element_type=F32) + b_ref[...]


def ada_modulation(c, w_ada, b_ada):
    L, D, D6 = w_ada.shape
    R = c.shape[0]
    tn = _tile(D6, 1024)
    return pl.pallas_call(
        _ada_kernel,
        grid=(L, D6 // tn),
        in_specs=[
            pl.BlockSpec((R, D), lambda l, j: (0, 0)),
            pl.BlockSpec((None, D, tn), lambda l, j: (l, 0, j)),
            pl.BlockSpec((None, 1, tn), lambda l, j: (l, 0, j)),
        ],
        out_specs=pl.BlockSpec((None, R, tn), lambda l, j: (l, 0, j)),
        out_shape=jax.ShapeDtypeStruct((L, R, D6), F32),
        compiler_params=_params(("parallel", "parallel")),
        name="ada_modulation",
    )(c, w_ada, b_ada.reshape(L, 1, D6))


def _norm_mod(x, g, sc, sh):
    ms = jnp.mean(x * x, axis=-1, keepdims=True)
    y = x * lax.rsqrt(ms + EPS)
    return (y * g) * (1.0 + sc) + sh


def _norm_mod_kernel(x_ref, g_ref, sc_ref, sh_ref, h_ref):
    h_ref[...] = _norm_mod(x_ref[...], g_ref[...], sc_ref[...], sh_ref[...]).astype(h_ref.dtype)


def norm_modulate(x, g, sc, sh, seq):
    N, D = x.shape
    tm = _tile(seq, 512)
    per_seq = seq // tm
    return pl.pallas_call(
        _norm_mod_kernel,
        grid=(N // tm,),
        in_specs=[
            pl.BlockSpec((tm, D), lambda i: (i, 0)),
            pl.BlockSpec((1, D), lambda i: (0, 0)),
            pl.BlockSpec((None, 1, D), lambda i: (i // per_seq, 0, 0)),
            pl.BlockSpec((None, 1, D), lambda i: (i // per_seq, 0, 0)),
        ],
        out_specs=pl.BlockSpec((tm, D), lambda i: (i, 0)),
        out_shape=jax.ShapeDtypeStruct((N, D), BF16),
        compiler_params=_params(("parallel",)),
        name="norm_modulate",
    )(x, g.reshape(1, D), sc, sh)


def _proj_kernel(h_ref, w_ref, o_ref, *, sigmoid_out):
    acc = jnp.dot(h_ref[...], w_ref[...], preferred_element_type=F32)
    if sigmoid_out:
        acc = jax.nn.sigmoid(acc)
    o_ref[...] = acc.astype(o_ref.dtype)


def projection(h, w, *, tn, sigmoid_out, name):
    N, D = h.shape
    C = w.shape[1]
    tm = _tile(N, 1024)
    return pl.pallas_call(
        functools.partial(_proj_kernel, sigmoid_out=sigmoid_out),
        grid=(N // tm, C // tn),
        in_specs=[
            pl.BlockSpec((tm, D), lambda i, j: (i, 0)),
            pl.BlockSpec((D, tn), lambda i, j: (0, j)),
        ],
        out_specs=pl.BlockSpec((tm, tn), lambda i, j: (i, j)),
        out_shape=jax.ShapeDtypeStruct((N, C), BF16),
        compiler_params=_params(("parallel", "arbitrary")),
        name=name,
    )(h, w)


def _na_bias_table(rel_bias):
    R = NA_ROWS_PER_BLOCK
    KR = NA_KEY_BLOCKS * R
    q_rows = np.stack([np.arange(R), R + np.arange(R), 2 * R + np.arange(R)])
    w_start = np.stack([np.zeros(R, np.int64), np.arange(R), np.full(R, KR - NA_WIN_H)])
    col = np.arange(GRID_W)
    col_start = np.clip(col - NA_WIN_W // 2, 0, GRID_W - NA_WIN_W)
    k_row = np.arange(KR)
    k_col = np.arange(GRID_W)
    row_ok = (k_row[None, None, :] >= w_start[:, :, None]) & (k_row[None, None, :] < w_start[:, :, None] + NA_WIN_H)
    col_ok = (k_col[None, :] >= col_start[:, None]) & (k_col[None, :] < col_start[:, None] + NA_WIN_W)
    dr = k_row[None, None, :] - q_rows[:, :, None] + (NA_WIN_H - 1)
    dc = k_col[None, :] - col[:, None] + (NA_WIN_W - 1)
    H = rel_bias.shape[0]
    pick_c = ((dc[None] == np.arange(2 * NA_WIN_W - 1)[:, None, None]) & col_ok[None]).astype(np.float32)
    by_col = jnp.einsum('hab,bqk->haqk', rel_bias.astype(F32), pick_c, precision=lax.Precision.HIGHEST)
    by_col = jnp.where(col_ok, by_col, NEG)
    masked = jnp.full((H, GRID_W, GRID_W), NEG, F32)
    types = []
    for t in range(3):
        q_blocks = []
        for qr in range(R):
            q_blocks.append(jnp.concatenate(
                [by_col[:, int(dr[t, qr, kr])] if row_ok[t, qr, kr] else masked for kr in range(KR)], axis=2))
        types.append(jnp.concatenate(q_blocks, axis=1))
    return jnp.stack(types)


def _na_kernel(q_ref, k0_ref, k1_ref, k2_ref, v0_ref, v1_ref, v2_ref, bias_ref, o_ref):
    lane = lax.broadcasted_iota(jnp.int32, (NA_BLOCK, LANE), 1)
    lo = lane < HEAD_DIM
    scale = HEAD_DIM ** -0.5
    for u in range(N_HEADS // 2):
        cs = slice(u * LANE, (u + 1) * LANE)
        q = q_ref[:, cs] * scale
        ks = (k0_ref[:, cs], k1_ref[:, cs], k2_ref[:, cs])
        vs = (v0_ref[:, cs], v1_ref[:, cs], v2_ref[:, cs])
        outs = []
        for hh in range(2):
            qh = jnp.where(lo if hh == 0 else ~lo, q, jnp.zeros_like(q))
            s = jnp.concatenate(
                [lax.dot_general(qh, k, (((1,), (1,)), ((), ())), preferred_element_type=F32) for k in ks], axis=1)
            s = s + bias_ref[2 * u + hh]
            m = jnp.max(s, axis=1, keepdims=True)
            p = jnp.exp(s - m)
            l = jnp.sum(p, axis=1, keepdims=True)
            pb = p.astype(BF16)
            o = None
            for j in range(NA_KEY_BLOCKS):
                t = jnp.dot(pb[:, j * NA_BLOCK:(j + 1) * NA_BLOCK], vs[j], preferred_element_type=F32)
                o = t if o is None else o + t
            outs.append(o / l)
        o_ref[:, cs] = jnp.where(lo, outs[0], outs[1]).astype(o_ref.dtype)


def neighborhood_attention(mix, bias_table, batch, seq):
    N = mix.shape[0]
    W = N_HEADS * HEAD_DIM
    nb = seq // NA_BLOCK
    assert seq % NA_BLOCK == 0 and nb >= NA_KEY_BLOCKS

    def kv_spec(colblk, j):
        return pl.BlockSpec(
            (NA_BLOCK, W),
            lambda b, i: (b * nb + jnp.clip(i - 1, 0, nb - NA_KEY_BLOCKS) + j, colblk))

    def bias_idx(b, i):
        return (jnp.where(i == 0, 0, jnp.where(i == nb - 1, 2, 1)), 0, 0, 0)

    return pl.pallas_call(
        _na_kernel,
        grid=(batch, nb),
        in_specs=[pl.BlockSpec((NA_BLOCK, W), lambda b, i: (b * nb + i, 0))]
        + [kv_spec(1, j) for j in range(NA_KEY_BLOCKS)]
        + [kv_spec(2, j) for j in range(NA_KEY_BLOCKS)]
        + [pl.BlockSpec((None, N_HEADS, NA_BLOCK, NA_KEY_BLOCKS * NA_BLOCK), bias_idx)],
        out_specs=pl.BlockSpec((NA_BLOCK, W), lambda b, i: (b * nb + i, 0)),
        out_shape=jax.ShapeDtypeStruct((N, W), BF16),
        compiler_params=_params(("parallel", "arbitrary")),
        name="neighborhood_attention",
    )(mix, mix, mix, mix, mix, mix, mix, bias_table)


def _rope_angles(pos, dim):
    inv_freq = ROPE_THETA ** (-jnp.arange(0, dim, 2, dtype=F32) / dim)
    ang = pos.astype(F32)[:, None] * inv_freq[None, :]
    return jnp.concatenate([ang, ang], axis=-1)


def _rot_tables(cos, sin, scale):
    n = cos.shape[1]
    first = (np.arange(n) % 32) < 16
    sa = jnp.where(first[None, :], -sin, 0.0)
    sb = jnp.where(first[None, :], 0.0, sin)
    return cos * scale, sa * scale, sb * scale


def _mla_tables(seq):
    pos = jnp.arange(seq)
    ang = _rope_angles(pos, MLA_ROPE)
    cos, sin = jnp.cos(ang), jnp.sin(ang)
    pad_hi = LANE - MLA_NOPE - MLA_ROPE

    def place(t, nope_val):
        return jnp.concatenate(
            [jnp.full((seq, MLA_NOPE), nope_val, F32), t, jnp.zeros((seq, pad_hi), F32)], axis=1)

    qs = (MLA_NOPE + MLA_ROPE) ** -0.5 * LOG2E
    qc, qa, qb = _rot_tables(cos, sin, qs)
    kc, ka, kb = _rot_tables(cos, sin, 1.0)
    q_tab = jnp.stack([place(qc, qs), place(qa, 0.0), place(qb, 0.0)])
    k_tab = jnp.stack([place(kc, 0.0), place(ka, 0.0), place(kb, 0.0)])
    return q_tab, k_tab


def _gqa_tables(seq):
    pos = jnp.arange(seq)
    half = HEAD_DIM // 2
    ar = _rope_angles(pos // GRID_W, half)
    ac = _rope_angles(pos % GRID_W, half)
    cos = jnp.concatenate([jnp.cos(ar), jnp.cos(ac)] * 2, axis=1)
    sin = jnp.concatenate([jnp.sin(ar), jnp.sin(ac)] * 2, axis=1)
    qs = HEAD_DIM ** -0.5 * LOG2E
    return jnp.stack(_rot_tables(cos, sin, qs)), jnp.stack(_rot_tables(cos, sin, 1.0))


def _apply_rot(x, tab_ref):
    n = x.shape[1]
    return (x * tab_ref[0] + pltpu.roll(x, n - 16, 1) * tab_ref[1] + pltpu.roll(x, 16, 1) * tab_ref[2])


def _rms(x, g):
    ms = jnp.mean(x * x, axis=-1, keepdims=True)
    return x * lax.rsqrt(ms + EPS) * g


def _mla_prep_kernel(cq_ref, ckv_ref, kr_ref, gq_ref, gkv_ref, wq_ref, wkn_ref, wv_ref, qtab_ref, ktab_ref,
                     q_out, k_out, v_out):
    cqn = _rms(cq_ref[...].astype(F32), gq_ref[...]).astype(BF16)
    ckvn = _rms(ckv_ref[...].astype(F32), gkv_ref[...]).astype(BF16)
    qf = jnp.dot(cqn, wq_ref[...], preferred_element_type=F32)
    kn = jnp.dot(ckvn, wkn_ref[...], preferred_element_type=F32)
    tm = cq_ref.shape[0]
    hi_lane = lax.broadcasted_iota(jnp.int32, (tm, LANE), 1) >= MLA_V
    vf = jnp.dot(ckvn, wv_ref[...], preferred_element_type=F32)
    k_rope = _apply_rot(kr_ref[...].astype(F32), ktab_ref)
    for h in range(N_HEADS):
        cs = slice(h * LANE, (h + 1) * LANE)
        q_out[:, cs] = _apply_rot(qf[:, cs], qtab_ref).astype(q_out.dtype)
        k_out[:, cs] = (kn[:, cs] + k_rope).astype(k_out.dtype)
        v_out[:, cs] = jnp.where(hi_lane, 1.0, vf[:, cs]).astype(v_out.dtype)


def mla_prepare(mix, g_q, g_kv, wq, wkn, wv, q_tab, k_tab, seq):
    N = mix.shape[0]
    tm = _tile(seq, 512)
    per_seq = seq // tm
    qr, kvr = wq.shape[0], wkn.shape[0]
    cq_blk = 1536 // qr
    ckv_blk = 2560 // kvr
    kr_blk = 2816 // LANE
    const = lambda i: (0, 0)
    tab = pl.BlockSpec((3, tm, LANE), lambda i: (0, i % per_seq, 0))
    return pl.pallas_call(
        _mla_prep_kernel,
        grid=(N // tm,),
        in_specs=[
            pl.BlockSpec((tm, qr), lambda i: (i, cq_blk)),
            pl.BlockSpec((tm, kvr), lambda i: (i, ckv_blk)),
            pl.BlockSpec((tm, LANE), lambda i: (i, kr_blk)),
            pl.BlockSpec((1, qr), const),
            pl.BlockSpec((1, kvr), const),
            pl.BlockSpec(wq.shape, const),
            pl.BlockSpec(wkn.shape, const),
            pl.BlockSpec(wv.shape, const),
            tab, tab,
        ],
        out_specs=[
            pl.BlockSpec((tm, N_HEADS * LANE), lambda i: (i, 0)),
            pl.BlockSpec((tm, N_HEADS * LANE), lambda i: (i, 0)),
            pl.BlockSpec((tm, N_HEADS * LANE), lambda i: (i, 0)),
        ],
        out_shape=[jax.ShapeDtypeStruct((N, N_HEADS * LANE), BF16)] * 3,
        compiler_params=_params(("parallel",)),
        name="mla_prepare",
    )(mix, mix, mix, g_q.reshape(1, qr), g_kv.reshape(1, kvr), wq, wkn, wv, q_tab, k_tab)


def _head_rms(x, g, seg_ref):
    sq = x * x
    hi = sq.astype(BF16)
    lo = (sq - hi.astype(F32)).astype(BF16)
    ms = (jnp.dot(hi, seg_ref[...], preferred_element_type=F32)
          + jnp.dot(lo, seg_ref[...], preferred_element_type=F32)) * (1.0 / HEAD_DIM)
    return x * lax.rsqrt(ms + EPS) * g


def _gqa_prep_kernel(q_ref, k_ref, v_ref, gq_ref, gk_ref, segq_ref, segk_ref, qtab_ref, ktab_ref,
                     q_out, k_out, v_out):
    tm = q_ref.shape[0]
    lane = lax.broadcasted_iota(jnp.int32, (tm, LANE), 1)
    lo = lane < HEAD_DIM
    for g in range(GQA_KV_HEADS):
        cs = slice(g * LANE, (g + 1) * LANE)
        v_out[:, cs] = jnp.where(lo, v_ref[:, cs], jnp.ones((tm, LANE), v_ref.dtype))
    qn = _head_rms(q_ref[...].astype(F32), gq_ref[...], segq_ref)
    for u in range(N_HEADS // 2):
        y = _apply_rot(qn[:, u * LANE:(u + 1) * LANE], qtab_ref)
        q_out[:, (2 * u) * LANE:(2 * u + 1) * LANE] = jnp.where(lo, y, 0.0).astype(q_out.dtype)
        q_out[:, (2 * u + 1) * LANE:(2 * u + 2) * LANE] = jnp.where(lo, 0.0, y).astype(q_out.dtype)
    kn = _head_rms(k_ref[...].astype(F32), gk_ref[...], segk_ref)
    k_out[...] = _apply_rot(kn, ktab_ref).astype(k_out.dtype)


def gqa_prepare(mix, g_q, g_k, q_tab, k_tab, seq):
    N = mix.shape[0]
    tm = _tile(seq, 512)
    per_seq = seq // tm
    W = N_HEADS * HEAD_DIM
    KW = GQA_KV_HEADS * HEAD_DIM
    VW = GQA_KV_HEADS * LANE
    const = lambda i: (0, 0)
    seg = lambda n: jnp.asarray(np.kron(np.eye(n // HEAD_DIM), np.ones((HEAD_DIM, HEAD_DIM))), BF16)
    tab = pl.BlockSpec((3, tm, LANE), lambda i: (0, i % per_seq, 0))
    return pl.pallas_call(
        _gqa_prep_kernel,
        grid=(N // tm,),
        in_specs=[
            pl.BlockSpec((tm, W), lambda i: (i, 2048 // W)),
            pl.BlockSpec((tm, KW), lambda i: (i, 2944 // KW)),
            pl.BlockSpec((tm, VW), lambda i: (i, 3072 // VW)),
            pl.BlockSpec((1, W), const),
            pl.BlockSpec((1, KW), const),
            pl.BlockSpec((W, W), const),
            pl.BlockSpec((KW, KW), const),
            tab, tab,
        ],
        out_specs=[
            pl.BlockSpec((tm, N_HEADS * LANE), lambda i: (i, 0)),
            pl.BlockSpec((tm, KW), lambda i: (i, 0)),
            pl.BlockSpec((tm, VW), lambda i: (i, 0)),
        ],
        out_shape=[
            jax.ShapeDtypeStruct((N, N_HEADS * LANE), BF16),
            jax.ShapeDtypeStruct((N, KW), BF16),
            jax.ShapeDtypeStruct((N, VW), BF16),
        ],
        compiler_params=_params(("parallel",)),
        name="gqa_prepare",
    )(mix, mix, mix, jnp.tile(g_q, N_HEADS).reshape(1, W), jnp.tile(g_k, GQA_KV_HEADS).reshape(1, KW),
      seg(W), seg(KW), q_tab, k_tab)


def _flash_kernel(q_ref, k_ref, v_ref, o_ref, m_sc, acc_sc, *, shared_k, v_groups):
    kv = pl.program_id(2)
    tq = q_ref.shape[0]
    tk = k_ref.shape[0]

    @pl.when(kv == 0)
    def _():
        m_sc[...] = jnp.full(m_sc.shape, -jnp.inf, F32)
        acc_sc[...] = jnp.zeros(acc_sc.shape, F32)

    for h in range(N_HEADS):
        q = q_ref[:, h * LANE:(h + 1) * LANE]
        k = k_ref[...] if shared_k else k_ref[:, h * LANE:(h + 1) * LANE]
        g = h % v_groups
        v = v_ref[:, g * LANE:(g + 1) * LANE]
        s = lax.dot_general(q, k, (((1,), (1,)), ((), ())), preferred_element_type=F32)
        m_prev = m_sc[h]
        m_new = jnp.maximum(m_prev, jnp.max(s, axis=1, keepdims=True))
        alpha = jnp.exp2(m_prev - m_new)
        p = jnp.exp2(s - jnp.concatenate([m_new] * (tk // LANE), axis=1))
        acc_sc[h] = alpha * acc_sc[h] + jnp.dot(p.astype(BF16), v, preferred_element_type=F32)
        m_sc[h] = m_new

    @pl.when(kv == pl.num_programs(2) - 1)
    def _():
        lane = lax.broadcasted_iota(jnp.int32, (tq, LANE), 1)
        lo = lane < HEAD_DIM
        for u in range(N_HEADS // 2):
            a = acc_sc[2 * u]
            b = acc_sc[2 * u + 1]
            a = a / pltpu.roll(a, HEAD_DIM, 1)
            b = pltpu.roll(b, HEAD_DIM, 1) / b
            o_ref[:, u * LANE:(u + 1) * LANE] = jnp.where(lo, a, b).astype(o_ref.dtype)


def flash_attention(q, k, v, batch, seq, *, shared_k, name):
    N = q.shape[0]
    tq = _tile(seq, 1024)
    tk = _tile(seq, 2048)
    nq, nk = seq // tq, seq // tk
    kw, vw = k.shape[1], v.shape[1]
    ow = N_HEADS * HEAD_DIM
    return pl.pallas_call(
        functools.partial(_flash_kernel, shared_k=shared_k, v_groups=vw // LANE),
        grid=(batch, nq, nk),
        in_specs=[
            pl.BlockSpec((tq, N_HEADS * LANE), lambda b, i, j: (b * nq + i, 0)),
            pl.BlockSpec((tk, kw), lambda b, i, j: (b * nk + j, 0)),
            pl.BlockSpec((tk, vw), lambda b, i, j: (b * nk + j, 0)),
        ],
        out_specs=pl.BlockSpec((tq, ow), lambda b, i, j: (b * nq + i, 0)),
        out_shape=jax.ShapeDtypeStruct((N, ow), BF16),
        scratch_shapes=[pltpu.VMEM((N_HEADS, tq, LANE), F32)] * 2,
        compiler_params=_params(("parallel", "parallel", "arbitrary")),
        name=name,
    )(q, k, v)


def _merge_kernel(oa_ref, ob_ref, oc_ref, sg_ref, x_ref, gt_ref, wb_ref, wo_ref, g2_ref, sc2_ref, sh2_ref,
                  o_ref, h_ref):
    D = x_ref.shape[1]
    merged = None
    for i, o_r in enumerate((oa_ref, ob_ref, oc_ref)):
        t = jnp.dot(o_r[...], wb_ref[i], preferred_element_type=F32)
        t = sg_ref[:, i * D:(i + 1) * D].astype(F32) * t
        merged = t if merged is None else merged + t
    mix = jnp.dot(merged.astype(BF16), wo_ref[...], preferred_element_type=F32)
    x = x_ref[...] + gt_ref[...] * mix
    o_ref[...] = x
    h_ref[...] = _norm_mod(x, g2_ref[...], sc2_ref[...], sh2_ref[...]).astype(h_ref.dtype)


def merge_branches(o_a, o_b, o_c, sgate, x, gt, w_branch, w_out, g2, sc2, sh2, seq):
    N, D = x.shape
    W = o_a.shape[1]
    tm = _tile(seq, 256)
    per_seq = seq // tm
    row = lambda i: (i, 0)
    per_b = lambda i: (i // per_seq, 0, 0)
    return pl.pallas_call(
        _merge_kernel,
        grid=(N // tm,),
        in_specs=[
            pl.BlockSpec((tm, W), row), pl.BlockSpec((tm, W), row), pl.BlockSpec((tm, W), row),
            pl.BlockSpec((tm, N_BRANCH * D), row),
            pl.BlockSpec((tm, D), row),
            pl.BlockSpec((None, 1, D), lambda i: (i // per_seq, 0, 0)),
            pl.BlockSpec((N_BRANCH, W, D), lambda i: (0, 0, 0)),
            pl.BlockSpec((D, D), lambda i: (0, 0)),
            pl.BlockSpec((1, D), lambda i: (0, 0)),
            pl.BlockSpec((None, 1, D), per_b),
            pl.BlockSpec((None, 1, D), per_b),
        ],
        out_specs=[pl.BlockSpec((tm, D), row), pl.BlockSpec((tm, D), row)],
        out_shape=[jax.ShapeDtypeStruct((N, D), F32), jax.ShapeDtypeStruct((N, D), BF16)],
        compiler_params=_params(("parallel",)),
        name="merge_branches",
    )(o_a, o_b, o_c, sgate, x, gt, w_branch, w_out, g2.reshape(1, D), sc2, sh2)


def _router_kernel(h_ref, wr_ref, aff_out):
    logits = lax.dot_general(wr_ref[...], h_ref[...], (((1,), (1,)), ((), ())), preferred_element_type=F32)
    m = jnp.max(logits, axis=0, keepdims=True)
    p = jnp.exp(logits - m)
    aff_out[...] = p / jnp.sum(p, axis=0, keepdims=True)


def router(h, w_router_t):
    N, D = h.shape
    E = w_router_t.shape[0]
    tm = _tile(N, 1024)
    return pl.pallas_call(
        _router_kernel,
        grid=(N // tm,),
        in_specs=[pl.BlockSpec((tm, D), lambda i: (i, 0)), pl.BlockSpec((E, D), lambda i: (0, 0))],
        out_specs=pl.BlockSpec((E, tm), lambda i: (0, i)),
        out_shape=jax.ShapeDtypeStruct((E, N), F32),
        compiler_params=_params(("parallel",)),
        name="router",
    )(h, w_router_t)


MOE_WIN = 512
MOE_ALIGN = 16
MOE_CHUNK = 128
MOE_TILE = 512
_SEG_PIECES = (512, 256, 128, 64, 32, 16)


def _round_up(x, m):
    return (x + (m - 1)) // m * m


def _select_kernel(aff_ref, slot_ref, wbase_ref, wcnt_ref, *, cap, chunks_per_win):
    a = aff_ref[...]
    nc = a.shape[0]
    bits = pltpu.bitcast(a, jnp.int32)

    def search(i, t):
        cand = t | jnp.left_shift(jnp.int32(1), 30 - i)
        n_ge = jnp.sum((bits >= cand).astype(jnp.int32))
        return jnp.where(n_ge >= cap, cand, t)

    thr = lax.fori_loop(0, 31, search, jnp.int32(0))
    gt = bits > thr
    eq = bits == thr
    need = (cap - jnp.sum(gt.astype(jnp.int32))).astype(F32)

    li = lax.broadcasted_iota(jnp.int32, (LANE, LANE), 0)
    lj = lax.broadcasted_iota(jnp.int32, (LANE, LANE), 1)
    tri = jnp.where(li <= lj, 1.0, 0.0).astype(BF16)
    ones = jnp.ones((LANE, LANE), BF16)
    ci = lax.broadcasted_iota(jnp.int32, (nc, nc), 0)
    cj = lax.broadcasted_iota(jnp.int32, (nc, nc), 1)
    shift = chunks_per_win.bit_length() - 1
    wi, wj = jnp.right_shift(ci, shift), jnp.right_shift(cj, shift)
    as_bf = lambda m: jnp.where(m, 1.0, 0.0).astype(BF16)
    dot = lambda x, y: jnp.dot(x, y, preferred_element_type=F32)

    eq_b = as_bf(eq)
    eq_rank = dot(as_bf(cj < ci), dot(eq_b, ones).astype(BF16)) + dot(eq_b, tri) - eq_b.astype(F32)
    sel = gt | (eq & (eq_rank < need))
    sel_b = as_bf(sel)
    incl = dot(sel_b, tri)
    tot = dot(sel_b, ones).astype(BF16)
    in_win = dot(as_bf((cj < ci) & (wj == wi)), tot)
    wcnt = dot(as_bf(wj == wi), tot)
    wpad = jnp.ceil(wcnt * (1.0 / MOE_ALIGN)) * MOE_ALIGN
    first = (cj & (chunks_per_win - 1)) == 0
    wbase = dot(as_bf((wj < wi) & first), wpad.astype(BF16))
    slot = wbase + in_win + incl - sel_b.astype(F32)
    slot_ref[...] = jnp.where(sel, slot, -1.0).astype(jnp.int32)
    wbase_ref[...] = wbase.astype(jnp.int32)
    wcnt_ref[...] = wcnt.astype(jnp.int32)


def select_tokens(aff, cap):
    E, N = aff.shape
    nc = N // LANE
    cpw = MOE_WIN // LANE
    blk = pl.BlockSpec((None, nc, LANE), lambda e: (e, 0, 0))
    out = jax.ShapeDtypeStruct((E, nc, LANE), jnp.int32)
    slot, wbase, wcnt = pl.pallas_call(
        functools.partial(_select_kernel, cap=cap, chunks_per_win=cpw),
        grid=(E,),
        in_specs=[blk],
        out_specs=[blk, blk, blk],
        out_shape=[out, out, out],
        compiler_params=_params(("parallel",)),
        name="select_tokens",
    )(aff.reshape(E, nc, LANE))
    return slot.reshape(E, N), wbase[:, ::cpw, 0], wcnt[:, ::cpw, 0]


def _for_segment_pieces(n_rows, make_copy, act):
    off = jnp.int32(0)
    for piece in _SEG_PIECES:
        has = (n_rows & piece) != 0

        @pl.when(has)
        def _(off=off, piece=piece):
            act(make_copy(pl.multiple_of(off, MOE_ALIGN), piece))

        off = off + jnp.where(has, piece, 0)


MOE_GROUP = 4


def _dispatch_kernel(base_ref, cnt_ref, h_ref, slot_ref, xe_hbm, stage, xstage, zero_sc, sems, xsem, *, nw, ne):
    w = pl.program_id(0)
    cur = w % 2
    win = h_ref.shape[0]
    row_id = lax.broadcasted_iota(jnp.int32, (MOE_CHUNK, win), 0)

    def seg_rows(ww, ee):
        return _round_up(cnt_ref[ee * nw + ww], MOE_ALIGN)

    def head_copy(buf, ww, ee):
        b = base_ref[ee * nw + ww]

        def make(off, piece):
            return pltpu.make_async_copy(stage.at[buf, pl.ds(pl.multiple_of(ee * MOE_CHUNK + off, MOE_ALIGN), piece)],
                                         xe_hbm.at[ee, pl.ds(pl.multiple_of(b + off, MOE_ALIGN), piece)],
                                         sems.at[buf])
        return make

    def for_heads(buf, ww, act):
        for ee in range(ne):
            _for_segment_pieces(jnp.minimum(seg_rows(ww, ee), MOE_CHUNK), head_copy(buf, ww, ee), act)

    @pl.when(w == 0)
    def _():
        zero_sc[...] = jnp.zeros(zero_sc.shape, zero_sc.dtype)

    @pl.when(w >= 2)
    def _():
        for_heads(cur, w - 2, lambda c: c.wait())

    h = h_ref[...]

    def onehot(ee, k):
        want = base_ref[ee * nw + w] + k * MOE_CHUNK + row_id
        return jnp.where(slot_ref[ee:ee + 1, :] == want, 1.0, 0.0).astype(BF16)

    for g in range(ne // MOE_GROUP):
        sel = jnp.concatenate([onehot(g * MOE_GROUP + i, 0) for i in range(MOE_GROUP)], axis=0)
        stage[cur, g * MOE_GROUP * MOE_CHUNK:(g + 1) * MOE_GROUP * MOE_CHUNK, :] = jnp.dot(
            sel, h, preferred_element_type=F32).astype(stage.dtype)
    for_heads(cur, w, lambda c: c.start())

    for ee in range(ne):
        rows = seg_rows(w, ee)
        b = base_ref[ee * nw + w]

        def tail(k, carry, ee=ee, rows=rows, b=b):
            xstage[...] = jnp.dot(onehot(ee, k), h, preferred_element_type=F32).astype(xstage.dtype)

            def make(off, piece):
                return pltpu.make_async_copy(
                    xstage.at[pl.ds(off, piece)],
                    xe_hbm.at[ee, pl.ds(pl.multiple_of(b + k * MOE_CHUNK + off, MOE_ALIGN), piece)], xsem)
            n = jnp.minimum(rows - k * MOE_CHUNK, MOE_CHUNK)
            _for_segment_pieces(n, make, lambda c: c.start())
            _for_segment_pieces(n, make, lambda c: c.wait())
            return carry

        lax.fori_loop(1, (rows + MOE_CHUNK - 1) // MOE_CHUNK, tail, 0)

    @pl.when(w == nw - 1)
    def _():
        for ee in range(ne):
            used = base_ref[ee * nw + w] + seg_rows(w, ee)
            fill = _round_up(used, MOE_TILE) - used

            def zmake(off, piece, ee=ee, used=used):
                return pltpu.make_async_copy(zero_sc.at[pl.ds(0, piece)],
                                             xe_hbm.at[ee, pl.ds(pl.multiple_of(used + off, MOE_ALIGN), piece)], xsem)
            _for_segment_pieces(fill, zmake, lambda c: c.start())
            _for_segment_pieces(fill, zmake, lambda c: c.wait())
        for_heads(cur, w, lambda c: c.wait())
        for_heads(1 - cur, w - 1, lambda c: c.wait())


def dispatch_tokens(h, slot, base, cnt, rows_per_expert):
    N, D = h.shape
    E = slot.shape[0]
    nw = N // MOE_WIN
    assert nw >= 2 and E % MOE_GROUP == 0 and MOE_TILE <= _SEG_PIECES[0]
    return pl.pallas_call(
        functools.partial(_dispatch_kernel, nw=nw, ne=E),
        grid_spec=pltpu.PrefetchScalarGridSpec(
            num_scalar_prefetch=2,
            grid=(nw,),
            in_specs=[
                pl.BlockSpec((MOE_WIN, D), lambda w, *_: (w, 0)),
                pl.BlockSpec((E, MOE_WIN), lambda w, *_: (0, w)),
            ],
            out_specs=pl.BlockSpec(memory_space=pl.ANY),
            scratch_shapes=[
                pltpu.VMEM((2, E * MOE_CHUNK, D), BF16),
                pltpu.VMEM((MOE_CHUNK, D), BF16),
                pltpu.VMEM((MOE_TILE, D), BF16),
                pltpu.SemaphoreType.DMA((2,)),
                pltpu.SemaphoreType.DMA,
            ],
        ),
        out_shape=jax.ShapeDtypeStruct((E, rows_per_expert, D), BF16),
        compiler_params=_params(("arbitrary",)),
        name="dispatch_tokens",
    )(base.reshape(-1), cnt.reshape(-1), h, slot)


def _expert_kernel(used_ref, x_ref, wg_ref, wu_ref, wd_ref, o_ref):
    @pl.when(pl.program_id(1) * x_ref.shape[0] < used_ref[pl.program_id(0)])
    def _():
        x = x_ref[...]
        a = jnp.dot(x, wg_ref[...], preferred_element_type=F32)
        u = jnp.dot(x, wu_ref[...], preferred_element_type=F32)
        hmid = (a * jax.nn.sigmoid(a) * u).astype(BF16)
        o_ref[...] = jnp.dot(hmid, wd_ref[...], preferred_element_type=F32).astype(o_ref.dtype)


def expert_ffn(xe, used, w_gate, w_up, w_down, layer):
    E, R, D = xe.shape
    F = w_gate.shape[3]
    return pl.pallas_call(
        _expert_kernel,
        grid_spec=pltpu.PrefetchScalarGridSpec(
            num_scalar_prefetch=1,
            grid=(E, R // MOE_TILE),
            in_specs=[
                pl.BlockSpec((None, MOE_TILE, D), lambda e, i, *_: (e, i, 0)),
                pl.BlockSpec((None, None, D, F), lambda e, i, *_: (layer, e, 0, 0)),
                pl.BlockSpec((None, None, D, F), lambda e, i, *_: (layer, e, 0, 0)),
                pl.BlockSpec((None, None, F, D), lambda e, i, *_: (layer, e, 0, 0)),
            ],
            out_specs=pl.BlockSpec((None, MOE_TILE, D), lambda e, i, *_: (e, i, 0)),
        ),
        out_shape=jax.ShapeDtypeStruct((E, R, D), BF16),
        compiler_params=_params(("parallel", "arbitrary")),
        name="expert_ffn",
    )(used, xe, w_gate, w_up, w_down)


def _combine_kernel(base_ref, cnt_ref, lim_ref, x_ref, gt_ref, slot_ref, aff_ref, g_ref, ye_hbm, o_ref,
                    buf, xbuf, sems, xsem, *, nw, ne, final_norm):
    w = pl.program_id(0)
    cur = w % 2
    win = x_ref.shape[0]
    col = lax.broadcasted_iota(jnp.int32, (win, MOE_CHUNK), 1)

    def chunk_start(ww, ee, k):
        b = base_ref[ee * nw + ww]
        return pl.multiple_of(jnp.minimum(b + k * MOE_CHUNK, lim_ref[ee] - MOE_CHUNK), MOE_ALIGN)

    def head_copy(slot_id, ww, ee):
        return pltpu.make_async_copy(ye_hbm.at[ee, pl.ds(chunk_start(ww, ee, 0), MOE_CHUNK)],
                                     buf.at[slot_id, ee * MOE_CHUNK:(ee + 1) * MOE_CHUNK], sems.at[slot_id])

    @pl.when(w == 0)
    def _():
        for ee in range(ne):
            head_copy(cur, w, ee).start()

    @pl.when(w + 1 < nw)
    def _():
        for ee in range(ne):
            head_copy(1 - cur, w + 1, ee).start()

    def weights(ee, k):
        scol = slot_ref[:, ee:ee + 1]
        gate = aff_ref[:, ee:ee + 1]
        if not (isinstance(k, int) and k == 0):
            gate = jnp.where(scol >= base_ref[ee * nw + w] + k * MOE_CHUNK, gate, 0.0)
        return jnp.where(scol == chunk_start(w, ee, k) + col, gate, 0.0).astype(BF16)

    for ee in range(ne):
        head_copy(cur, w, ee).wait()
    o_ref[...] = jnp.dot(jnp.concatenate([weights(ee, 0) for ee in range(ne)], axis=1), buf[cur],
                         preferred_element_type=F32)

    for ee in range(ne):
        def extra(k, carry, ee=ee):
            cp = pltpu.make_async_copy(ye_hbm.at[ee, pl.ds(chunk_start(w, ee, k), MOE_CHUNK)], xbuf, xsem)
            cp.start()
            cp.wait()
            o_ref[...] += jnp.dot(weights(ee, k), xbuf[...], preferred_element_type=F32)
            return carry

        lax.fori_loop(1, (cnt_ref[ee * nw + w] + MOE_CHUNK - 1) // MOE_CHUNK, extra, 0)

    x = x_ref[...] + gt_ref[...] * o_ref[...]
    if final_norm:
        ms = jnp.mean(x * x, axis=-1, keepdims=True)
        x = x * lax.rsqrt(ms + EPS) * g_ref[...]
    o_ref[...] = x


def combine_experts(x, gt, slot_t, aff_t, ye, base, cnt, lim, g_final, seq, *, final_norm):
    N, D = x.shape
    E = slot_t.shape[1]
    nw = N // MOE_WIN
    per_seq = seq // MOE_WIN
    return pl.pallas_call(
        functools.partial(_combine_kernel, nw=nw, ne=E, final_norm=final_norm),
        grid_spec=pltpu.PrefetchScalarGridSpec(
            num_scalar_prefetch=3,
            grid=(nw,),
            in_specs=[
                pl.BlockSpec((MOE_WIN, D), lambda w, *_: (w, 0)),
                pl.BlockSpec((None, 1, D), lambda w, *_: (w // per_seq, 0, 0)),
                pl.BlockSpec((MOE_WIN, E), lambda w, *_: (w, 0)),
                pl.BlockSpec((MOE_WIN, E), lambda w, *_: (w, 0)),
                pl.BlockSpec((1, D), lambda w, *_: (0, 0)),
                pl.BlockSpec(memory_space=pl.ANY),
            ],
            out_specs=pl.BlockSpec((MOE_WIN, D), lambda w, *_: (w, 0)),
            scratch_shapes=[
                pltpu.VMEM((2, E * MOE_CHUNK, D), BF16),
                pltpu.VMEM((MOE_CHUNK, D), BF16),
                pltpu.SemaphoreType.DMA((2,)),
                pltpu.SemaphoreType.DMA,
            ],
        ),
        out_shape=jax.ShapeDtypeStruct((N, D), F32),
        compiler_params=_params(("arbitrary",)),
        name="combine_experts",
    )(base.reshape(-1), cnt.reshape(-1), lim, x, gt, slot_t, aff_t, g_final.reshape(1, D), ye)


def _prepare_layer(l, p, D):
    w_in = p['w_in'][l]
    W = N_HEADS * HEAD_DIM
    KW = GQA_KV_HEADS * HEAD_DIM
    qr = p['mla_w_uq'].shape[1]
    kvr = p['mla_w_ukv'].shape[1]
    o = 0
    seg = {}
    for name, width in (('a', 3 * W), ('cq', qr), ('ckv', kvr), ('kr', MLA_ROPE), ('gq', W), ('gk', KW), ('gv', KW),
                        ('gate', N_BRANCH * D)):
        seg[name] = w_in[:, o:o + width]
        o += width
    gq_cols = np.concatenate([np.arange(h * HEAD_DIM, (h + 1) * HEAD_DIM) for h in GQA_HEAD_ORDER])
    zeros = lambda n: jnp.zeros((D, n), w_in.dtype)
    gv = [p_ for g in range(GQA_KV_HEADS)
          for p_ in (seg['gv'][:, g * HEAD_DIM:(g + 1) * HEAD_DIM], zeros(LANE - HEAD_DIM))]
    w_mix = jnp.concatenate(
        [seg['a'], seg['cq'], seg['gq'][:, gq_cols], seg['ckv'],
         zeros(MLA_NOPE), seg['kr'], zeros(LANE - MLA_NOPE - MLA_ROPE), seg['gk']] + gv, axis=1).astype(BF16)

    w_uq = p['mla_w_uq'][l].reshape(qr, N_HEADS, MLA_NOPE + MLA_ROPE)
    wq = jnp.pad(w_uq, ((0, 0), (0, 0), (0, LANE - MLA_NOPE - MLA_ROPE))).reshape(qr, N_HEADS * LANE).astype(BF16)
    w_ukv = p['mla_w_ukv'][l].reshape(kvr, N_HEADS, MLA_NOPE + MLA_V)
    wkn = jnp.pad(w_ukv[:, :, :MLA_NOPE], ((0, 0), (0, 0), (0, LANE - MLA_NOPE))).reshape(kvr, N_HEADS * LANE)
    wv = jnp.pad(w_ukv[:, :, MLA_NOPE:], ((0, 0), (0, 0), (0, LANE - MLA_V))).reshape(kvr, N_HEADS * LANE)

    w_branch = p['w_branch'][l]
    w_branch = jnp.stack([w_branch[0], w_branch[1], w_branch[2][gq_cols]]).astype(BF16)
    return dict(
        w_mix=w_mix, w_gate_logits=seg['gate'].astype(BF16),
        wq=wq, wkn=wkn.astype(BF16), wv=wv.astype(BF16),
        w_branch=w_branch, w_out=p['w_out'][l].astype(BF16),
        w_router_t=p['w_router'][l].T.astype(BF16),
    )


def _trunk(x3, mod, p, layers, tables):
    B, T, D = x3.shape
    N = B * T
    x = x3.reshape(N, D)
    depth = len(layers)
    for l in range(depth):
        w = layers[l]
        sh1, sc1, gt1, sh2, sc2, gt2 = [mod[l][:, None, i * D:(i + 1) * D] for i in range(6)]
        h1 = norm_modulate(x, p['g_mix'][l], sc1, sh1, T)
        mix = projection(h1, w['w_mix'], tn=w['w_mix'].shape[1] // 2, sigmoid_out=False, name="mix_projection")
        sgate = projection(h1, w['w_gate_logits'], tn=_tile(N_BRANCH * D, 1536), sigmoid_out=True,
                           name="gate_projection")
        o_a = neighborhood_attention(mix, tables['na_bias'][l], B, T)
        qm, km, vm = mla_prepare(mix, p['mla_g_q'][l], p['mla_g_kv'][l], w['wq'], w['wkn'], w['wv'],
                                 tables['mla_q'][T], tables['mla_k'][T], T)
        o_b = flash_attention(qm, km, vm, B, T, shared_k=False, name="mla_attention")
        qg, kg, vg = gqa_prepare(mix, p['gqa_g_q'][l], p['gqa_g_k'][l], tables['gqa_q'][T], tables['gqa_k'][T], T)
        o_c = flash_attention(qg, kg, vg, B, T, shared_k=True, name="gqa_attention")
        x, h = merge_branches(o_a, o_b, o_c, sgate, x, gt1, w['w_branch'], w['w_out'], p['g_ffn'][l], sc2, sh2, T)
        aff = router(h, w['w_router_t'])
        cap = EC_CAPACITY * N // N_EXPERTS
        slot, base, cnt = select_tokens(aff, cap)
        used = base[:, -1] + _round_up(cnt[:, -1], MOE_ALIGN)
        rows = _round_up(cap + MOE_ALIGN * (N // MOE_WIN), MOE_TILE)
        xe = dispatch_tokens(h, slot, base, cnt, rows)
        ye = expert_ffn(xe, used, p['w_exp_gate'], p['w_exp_up'], p['w_exp_down'], l)
        x = combine_experts(x, gt2, slot.T, aff.T, ye, base, cnt, _round_up(used, MOE_TILE), p['g_final'], T,
                            final_norm=l == depth - 1)
    return x.reshape(B, T, D)


def kernel(x_prompt, x_sample, c_prompt, c_sample, w_ada, b_ada, g_mix, g_ffn, w_in, na_rel_bias, mla_g_q, mla_w_uq,
           mla_g_kv, mla_w_ukv, gqa_g_q, gqa_g_k, w_branch, w_out, w_router, w_exp_gate, w_exp_up, w_exp_down,
           g_final):
    p = dict(g_mix=g_mix, g_ffn=g_ffn, w_in=w_in, mla_g_q=mla_g_q, mla_w_uq=mla_w_uq, mla_g_kv=mla_g_kv,
             mla_w_ukv=mla_w_ukv, gqa_g_q=gqa_g_q, gqa_g_k=gqa_g_k, w_branch=w_branch, w_out=w_out,
             w_router=w_router, w_exp_gate=w_exp_gate, w_exp_up=w_exp_up, w_exp_down=w_exp_down, g_final=g_final)
    depth, D = g_mix.shape
    bp, bs = c_prompt.shape[0], c_sample.shape[0]
    rows = -(-(bp + bs) // 8) * 8
    c_all = jnp.concatenate([c_prompt, c_sample, jnp.zeros((rows - bp - bs, D), F32)], axis=0)
    mod = ada_modulation(c_all, w_ada, b_ada)
    layers = [_prepare_layer(l, p, D) for l in range(depth)]
    for name in ('w_exp_gate', 'w_exp_up', 'w_exp_down'):
        p[name] = p[name].astype(BF16)
    tables = dict(na_bias=[_na_bias_table(na_rel_bias[l]) for l in range(depth)],
                  mla_q={}, mla_k={}, gqa_q={}, gqa_k={})
    for T in {x_prompt.shape[1], x_sample.shape[1]}:
        tables['mla_q'][T], tables['mla_k'][T] = _mla_tables(T)
        tables['gqa_q'][T], tables['gqa_k'][T] = _gqa_tables(T)
    y_prompt = _trunk(x_prompt, mod[:, :bp], p, layers, tables)
    y_sample = _trunk(x_sample, mod[:, bp:bp + bs], p, layers, tables)
    return (y_prompt, y_sample)
```
